```python
import math
import jax, jax.numpy as jnp
from jax import lax
import numpy as np

D_MODEL = 1024
BATCH = 4
SEQ = 8192
DEPTH = 4

CHUNK = 64
N_MIXERS = 4
N_A = len(range(0, DEPTH, N_MIXERS))
N_B = len(range(1, DEPTH, N_MIXERS))
N_C = len(range(2, DEPTH, N_MIXERS))
N_D = len(range(3, DEPTH, N_MIXERS))
EPS = 1e-6
Q_BLOCK = 128

MLA_HEADS = 16
MLA_NOPE = 64
MLA_ROPE = 32
MLA_QK = MLA_NOPE + MLA_ROPE
MLA_V = 64
MLA_Q_RANK = 384
MLA_KV_RANK = 256
ROPE_BASE = 10000.0

ML_HEADS = 4
ML_QK = D_MODEL // 8
ML_V = D_MODEL // 4
GATE_CAP = 15.0

GLA_HEADS = 4
GLA_K = D_MODEL // 8
GLA_V = D_MODEL // 4
GLA_GATE_RANK = 16
GLA_TAU = 16.0

CONV_WIDTH = 31

D_FF = 4 * D_MODEL

kernel_name = 'hybrid_streaming_encoder_block'


def rms_norm(x, g):
    xf = x.astype(jnp.float32)
    y = xf * lax.rsqrt(jnp.mean(xf * xf, axis=-1, keepdims=True) + EPS)
    return (y * g.astype(jnp.float32)).astype(x.dtype)


def layer_norm(x, g, b):
    xf = x.astype(jnp.float32)
    mu = jnp.mean(xf, axis=-1, keepdims=True)
    var = jnp.mean(jnp.square(xf - mu), axis=-1, keepdims=True)
    y = (xf - mu) * lax.rsqrt(var + EPS) * g.astype(jnp.float32) + b.astype(jnp.float32)
    return y.astype(x.dtype)


def apply_rope(x, pos):
    half = x.shape[-1] // 2
    inv_freq = ROPE_BASE ** (-jnp.arange(half, dtype=jnp.float32) / half)
    ang = pos.astype(jnp.float32)[:, None] * inv_freq[None, :]
    cos = jnp.cos(ang)[:, None, :]
    sin = jnp.sin(ang)[:, None, :]
    xf = x.astype(jnp.float32)
    x1, x2 = xf[..., :half], xf[..., half:]
    return jnp.concatenate([x1 * cos - x2 * sin, x2 * cos + x1 * sin], axis=-1).astype(x.dtype)


def _to_chunks(t, heads, dh):
    b, s = t.shape[0], t.shape[1]
    return t.astype(jnp.float32).reshape(b, s // CHUNK, CHUNK, heads, dh).transpose(1, 0, 3, 2, 4)


def _from_chunks(o):
    nc, b, h, l, dv = o.shape
    return o.transpose(1, 0, 3, 2, 4).reshape(b, nc * l, h, dv)


def chunk_causal_attention(q, k, v):
    b, s, h, dq = q.shape
    nqb = s // Q_BLOCK
    scale = dq ** -0.5
    key_chunk = jnp.arange(s) // CHUNK
    qb = q.reshape(b, nqb, Q_BLOCK, h, dq).transpose(1, 0, 2, 3, 4)

    def one_block(args):
        qi, bi = args
        q_chunk = (bi * Q_BLOCK + jnp.arange(Q_BLOCK)) // CHUNK
        sc = jnp.einsum('bqhd,bkhd->bhqk', qi, k, preferred_element_type=jnp.float32) * scale
        mask = key_chunk[None, :] <= q_chunk[:, None]
        p = jax.nn.softmax(jnp.where(mask, sc, -jnp.inf), axis=-1)
        return jnp.einsum('bhqk,bkhd->bqhd', p.astype(v.dtype), v)

    o = lax.map(one_block, (qb, jnp.arange(nqb)))
    return o.transpose(1, 0, 2, 3, 4).reshape(b, s, h, v.shape[-1])


def mla_mixer(h, w_dq, q_norm, w_uq, w_dkv, kv_norm, w_ukv, q_gain, k_gain, w_o):
    b, s, _ = h.shape
    pos = jnp.arange(s)
    cq = rms_norm(h @ w_dq, q_norm)
    q = (cq @ w_uq).reshape(b, s, MLA_HEADS, MLA_QK)
    dkv = h @ w_dkv
    ckv = rms_norm(dkv[..., :MLA_KV_RANK], kv_norm)
    k_rope = jnp.broadcast_to(dkv[..., None, MLA_KV_RANK:], (b, s, MLA_HEADS, MLA_ROPE))
    kv = (ckv @ w_ukv).reshape(b, s, MLA_HEADS, MLA_NOPE + MLA_V)
    k = jnp.concatenate([kv[..., :MLA_NOPE], k_rope], axis=-1)
    v = kv[..., MLA_NOPE:]
    q = rms_norm(q, q_gain)
    k = rms_norm(k, k_gain)
    q = jnp.concatenate([q[..., :MLA_NOPE], apply_rope(q[..., MLA_NOPE:], pos)], axis=-1)
    k = jnp.concatenate([k[..., :MLA_NOPE], apply_rope(k[..., MLA_NOPE:], pos)], axis=-1)
    o = chunk_causal_attention(q, k, v)
    return o.reshape(b, s, MLA_HEADS * MLA_V) @ w_o


def mlstm_mixer(h, w_in, w_if, b_if, head_norm, w_o):
    b, s, _ = h.shape
    hq, hv = ML_HEADS * ML_QK, ML_HEADS * ML_V
    q, k, v, o_pre = jnp.split(h @ w_in, [hq, 2 * hq, 2 * hq + hv], axis=-1)
    gates = (h @ w_if + b_if).astype(jnp.float32)
    gates = GATE_CAP * jnp.tanh(gates / GATE_CAP)
    log_i = _to_chunks(gates[..., :ML_HEADS], ML_HEADS, 1)[..., 0]
    log_f = _to_chunks(jax.nn.log_sigmoid(gates[..., ML_HEADS:]), ML_HEADS, 1)[..., 0]
    q = _to_chunks(q, ML_HEADS, ML_QK)
    k = _to_chunks(k, ML_HEADS, ML_QK) * ML_QK ** -0.5
    v = _to_chunks(v, ML_HEADS, ML_V)
    causal = jnp.tril(jnp.ones((CHUNK, CHUNK), dtype=bool))

    def step(carry, xs):
        c_st, n_st, m_st = carry
        qc, kc, vc, li, lf = xs
        bcum = jnp.cumsum(lf, axis=-1)
        inter = bcum + m_st[..., None]
        dmat = bcum[..., :, None] - bcum[..., None, :] + li[..., None, :]
        dmat = jnp.where(causal, dmat, -jnp.inf)
        m_t = jnp.maximum(inter, jnp.max(dmat, axis=-1))
        w_intra = jnp.exp(dmat - m_t[..., None])
        w_inter = jnp.exp(inter - m_t)
        qk = jnp.einsum('bhtd,bhsd->bhts', qc, kc) * w_intra
        num = jnp.einsum('bhts,bhsv->bhtv', qk, vc) + w_inter[..., None] * jnp.einsum('bhtd,bhvd->bhtv', qc, c_st)
        den = jnp.sum(qk, axis=-1) + w_inter * jnp.einsum('bhtd,bhd->bht', qc, n_st)
        hc = num / jnp.maximum(jnp.abs(den), jnp.exp(-m_t))[..., None]
        b_last = bcum[..., -1]
        decay_s = b_last[..., None] - bcum + li
        m_new = jnp.maximum(b_last + m_st, jnp.max(decay_s, axis=-1))
        ws = jnp.exp(decay_s - m_new[..., None])
        wc = jnp.exp(b_last + m_st - m_new)
        c_new = wc[..., None, None] * c_st + jnp.einsum('bhs,bhsv,bhsd->bhvd', ws, vc, kc)
        n_new = wc[..., None] * n_st + jnp.einsum('bhs,bhsd->bhd', ws, kc)
        return (c_new, n_new, m_new), hc

    init = (jnp.zeros((b, ML_HEADS, ML_V, ML_QK), jnp.float32),
            jnp.zeros((b, ML_HEADS, ML_QK), jnp.float32),
            jnp.zeros((b, ML_HEADS), jnp.float32))
    _, hs = lax.scan(step, init, (q, k, v, log_i, log_f))
    hs = rms_norm(_from_chunks(hs), head_norm.reshape(ML_HEADS, ML_V)).reshape(b, s, hv)
    out = jax.nn.sigmoid(o_pre.astype(jnp.float32)) * hs
    return out.astype(h.dtype) @ w_o


def gla_mixer(h, w_in, w_a1, w_a2, b_a, head_norm, w_o):
    b, s, _ = h.shape
    hk, hv = GLA_HEADS * GLA_K, GLA_HEADS * GLA_V
    q, k, v, r = jnp.split(h @ w_in, [hk, 2 * hk, 2 * hk + hv], axis=-1)
    log_a = jax.nn.log_sigmoid(((h @ w_a1) @ w_a2 + b_a).astype(jnp.float32)) / GLA_TAU
    q = _to_chunks(q, GLA_HEADS, GLA_K) * GLA_K ** -0.5
    k = _to_chunks(k, GLA_HEADS, GLA_K)
    v = _to_chunks(v, GLA_HEADS, GLA_V)
    log_a = _to_chunks(log_a, GLA_HEADS, GLA_K)
    causal = jnp.tril(jnp.ones((CHUNK, CHUNK), dtype=bool))

    def step(state, xs):
        qc, kc, vc, la = xs
        bcum = jnp.cumsum(la, axis=-2)
        diff = jnp.where(causal[:, :, None], bcum[..., :, None, :] - bcum[..., None, :, :], -jnp.inf)
        att = jnp.einsum('bhtk,bhtsk,bhsk->bhts', qc, jnp.exp(diff), kc)
        out = jnp.einsum('bhts,bhsv->bhtv', att, vc) + jnp.einsum('bhtk,bhkv->bhtv', qc * jnp.exp(bcum), state)
        b_last = bcum[..., -1:, :]
        new_state = (jnp.exp(b_last[..., 0, :])[..., None] * state
                     + jnp.einsum('bhsk,bhsv->bhkv', kc * jnp.exp(b_last - bcum), vc))
        return new_state, out

    init = jnp.zeros((b, GLA_HEADS, GLA_K, GLA_V), jnp.float32)
    _, o = lax.scan(step, init, (q, k, v, log_a))
    o = rms_norm(_from_chunks(o), head_norm.reshape(GLA_HEADS, GLA_V)).reshape(b, s, hv)
    o = o * jax.nn.silu(r.astype(jnp.float32))
    return o.astype(h.dtype) @ w_o


def conv_mixer(h, w_pw1, b_pw1, w_dw, b_dw, ln_g, ln_b, w_pw2, b_pw2):
    a, g = jnp.split(h @ w_pw1 + b_pw1, 2, axis=-1)
    u = a * jax.nn.sigmoid(g)
    u = lax.conv_general_dilated(u, w_dw[:, None, :].astype(u.dtype), window_strides=(1,),
                                 padding=[(CONV_WIDTH - 1, 0)],
                                 dimension_numbers=('NWC', 'WIO', 'NWC'),
                                 feature_group_count=D_MODEL) + b_dw
    u = jax.nn.silu(layer_norm(u, ln_g, ln_b))
    return u @ w_pw2 + b_pw2


def sqrelu_mlp(h, w1, w2):
    return jnp.square(jax.nn.relu(h @ w1)) @ w2


def _w(key, shape, fan_in):
    return jax.random.normal(key, shape, jnp.float32) * (fan_in ** -0.5)


def _gain(key, shape):
    return 1.0 + 0.02 * jax.random.normal(key, shape, jnp.float32)


def _bias(key, shape):
    return 0.02 * jax.random.normal(key, shape, jnp.float32)


def setup_inputs(seed: int = 0) -> dict:
    key = jax.random.key(seed)
    ks = list(jax.random.split(key, 40))
    D = D_MODEL
    x = jax.random.normal(ks[0], (BATCH, SEQ, D), jnp.float32)
    norm_mix = _gain(ks[1], (DEPTH, D))
    norm_ffn = _gain(ks[2], (DEPTH, D))
    mla_w_dq = _w(ks[3], (N_A, D, MLA_Q_RANK), D)
    mla_q_norm = _gain(ks[4], (N_A, MLA_Q_RANK))
    mla_w_uq = _w(ks[5], (N_A, MLA_Q_RANK, MLA_HEADS * MLA_QK), MLA_Q_RANK)
    mla_w_dkv = _w(ks[6], (N_A, D, MLA_KV_RANK + MLA_ROPE), D)
    mla_kv_norm = _gain(ks[7], (N_A, MLA_KV_RANK))
    mla_w_ukv = _w(ks[8], (N_A, MLA_KV_RANK, MLA_HEADS * (MLA_NOPE + MLA_V)), MLA_KV_RANK)
    mla_q_gain = _gain(ks[9], (N_A, MLA_QK))
    mla_k_gain = _gain(ks[10], (N_A, MLA_QK))
    mla_w_o = _w(ks[11], (N_A, MLA_HEADS * MLA_V, D), MLA_HEADS * MLA_V)
    mlstm_w_in = _w(ks[12], (N_B, D, 2 * ML_HEADS * ML_QK + 2 * ML_HEADS * ML_V), D)
    mlstm_w_if = _w(ks[13], (N_B, D, 2 * ML_HEADS), D)
    mlstm_b_if = jnp.concatenate([_bias(ks[14], (N_B, ML_HEADS)),
                                  3.0 + 0.5 * jax.random.normal(ks[15], (N_B, ML_HEADS), jnp.float32)], axis=-1)
    mlstm_head_norm = _gain(ks[16], (N_B, ML_HEADS * ML_V))
    mlstm_w_o = _w(ks[17], (N_B, ML_HEADS * ML_V, D), ML_HEADS * ML_V)
    gla_w_in = _w(ks[18], (N_C, D, 2 * GLA_HEADS * GLA_K + 2 * GLA_HEADS * GLA_V), D)
    gla_w_a1 = _w(ks[19], (N_C, D, GLA_GATE_RANK), D)
    gla_w_a2 = _w(ks[20], (N_C, GLA_GATE_RANK, GLA_HEADS * GLA_K), GLA_GATE_RANK)
    gla_b_a = _bias(ks[21], (N_C, GLA_HEADS * GLA_K))
    gla_head_norm = _gain(ks[22], (N_C, GLA_HEADS * GLA_V))
    gla_w_o = _w(ks[23], (N_C, GLA_HEADS * GLA_V, D), GLA_HEADS * GLA_V)
    conv_w_pw1 = _w(ks[24], (N_D, D, 2 * D), D)
    conv_b_pw1 = _bias(ks[25], (N_D, 2 * D))
    conv_w_dw = _w(ks[26], (N_D, CONV_WIDTH, D), CONV_WIDTH)
    conv_b_dw = _bias(ks[27], (N_D, D))
    conv_ln_g = _gain(ks[28], (N_D, D))
    conv_ln_b = _bias(ks[29], (N_D, D))
    conv_w_pw2 = _w(ks[30], (N_D, D, D), D)
    conv_b_pw2 = _bias(ks[31], (N_D, D))
    ffn_w1 = _w(ks[32], (DEPTH, D, D_FF), D)
    ffn_w2 = _w(ks[33], (DEPTH, D_FF, D), D_FF)
    return {'x': x, 'norm_mix': norm_mix, 'norm_ffn': norm_ffn,
            'mla_w_dq': mla_w_dq, 'mla_q_norm': mla_q_norm, 'mla_w_uq': mla_w_uq,
            'mla_w_dkv': mla_w_dkv, 'mla_kv_norm': mla_kv_norm, 'mla_w_ukv': mla_w_ukv,
            'mla_q_gain': mla_q_gain, 'mla_k_gain': mla_k_gain, 'mla_w_o': mla_w_o,
            'mlstm_w_in': mlstm_w_in, 'mlstm_w_if': mlstm_w_if, 'mlstm_b_if': mlstm_b_if,
            'mlstm_head_norm': mlstm_head_norm, 'mlstm_w_o': mlstm_w_o,
            'gla_w_in': gla_w_in, 'gla_w_a1': gla_w_a1, 'gla_w_a2': gla_w_a2, 'gla_b_a': gla_b_a,
            'gla_head_norm': gla_head_norm, 'gla_w_o': gla_w_o,
            'conv_w_pw1': conv_w_pw1, 'conv_b_pw1': conv_b_pw1, 'conv_w_dw': conv_w_dw,
            'conv_b_dw': conv_b_dw, 'conv_ln_g': conv_ln_g, 'conv_ln_b': conv_ln_b,
            'conv_w_pw2': conv_w_pw2, 'conv_b_pw2': conv_b_pw2,
            'ffn_w1': ffn_w1, 'ffn_w2': ffn_w2}


def reference(x, norm_mix, norm_ffn,
              mla_w_dq, mla_q_norm, mla_w_uq, mla_w_dkv, mla_kv_norm, mla_w_ukv,
              mla_q_gain, mla_k_gain, mla_w_o,
              mlstm_w_in, mlstm_w_if, mlstm_b_if, mlstm_head_norm, mlstm_w_o,
              gla_w_in, gla_w_a1, gla_w_a2, gla_b_a, gla_head_norm, gla_w_o,
              conv_w_pw1, conv_b_pw1, conv_w_dw, conv_b_dw, conv_ln_g, conv_ln_b,
              conv_w_pw2, conv_b_pw2,
              ffn_w1, ffn_w2):
    for i in range(DEPTH):
        kind = i % N_MIXERS
        j = i // N_MIXERS
        hn = rms_norm(x, norm_mix[i])
        if kind == 0:
            y = mla_mixer(hn, mla_w_dq[j], mla_q_norm[j], mla_w_uq[j], mla_w_dkv[j], mla_kv_norm[j],
                          mla_w_ukv[j], mla_q_gain[j], mla_k_gain[j], mla_w_o[j])
        elif kind == 1:
            y = mlstm_mixer(hn, mlstm_w_in[j], mlstm_w_if[j], mlstm_b_if[j], mlstm_head_norm[j], mlstm_w_o[j])
        elif kind == 2:
            y = gla_mixer(hn, gla_w_in[j], gla_w_a1[j], gla_w_a2[j], gla_b_a[j], gla_head_norm[j], gla_w_o[j])
        else:
            y = conv_mixer(hn, conv_w_pw1[j], conv_b_pw1[j], conv_w_dw[j], conv_b_dw[j],
                           conv_ln_g[j], conv_ln_b[j], conv_w_pw2[j], conv_b_pw2[j])
        x = x + y.astype(x.dtype)
        x = x + sqrelu_mlp(rms_norm(x, norm_ffn[i]), ffn_w1[i], ffn_w2[i]).astype(x.dtype)
    return x
```

```python
import functools
import math

import numpy as np
import jax
import jax.numpy as jnp
from jax import lax
from jax.experimental import pallas as pl
from jax.experimental.pallas import tpu as pltpu

F32 = jnp.float32
BF16 = jnp.bfloat16

D_MODEL = 1024
D_FF = 4 * D_MODEL
EPS = 1e-6
CHUNK = 64

MLA_HEADS = 16
MLA_NOPE = 64
MLA_ROPE = 32
MLA_QK = MLA_NOPE + MLA_ROPE
MLA_V = 64
MLA_Q_RANK = 384
MLA_KV_RANK = 256
ROPE_BASE = 10000.0

ML_HEADS = 4
ML_QK = D_MODEL // 8
ML_V = D_MODEL // 4
GATE_CAP = 15.0

GLA_HEADS = 4
GLA_K = D_MODEL // 8
GLA_V = D_MODEL // 4
GLA_GATE_RANK = 16
GLA_TAU = 16.0

CONV_WIDTH = 31

LANES = 128
SUBLANES = 8
VMEM_LIMIT = 56 * 1024 * 1024

TOK_TILE = 512
FF_TILE = 1024
ATT_TILE = 512
REC_TILE = 256
CONV_HALO = 32
CONV_ROWS = 32
NEG_BIG = -1e30


def _cparams(*sem):
    return pltpu.CompilerParams(dimension_semantics=sem, vmem_limit_bytes=VMEM_LIMIT)


def _const_spec(shape):
    nd = len(shape)
    return pl.BlockSpec(shape, lambda *_: (0,) * nd, pipeline_mode=pl.Buffered(1))


def _dot(a, b):
    return jnp.dot(a, b, preferred_element_type=F32)


def _dot_nt(a, b):
    return lax.dot_general(a, b, (((1,), (1,)), ((), ())), preferred_element_type=F32)


def _dot_tn(a, b):
    return lax.dot_general(a, b, (((0,), (0,)), ((), ())), preferred_element_type=F32)


def _split2(a):
    hi = a.astype(BF16)
    lo = (a - hi.astype(F32)).astype(BF16)
    return hi, lo


def _split3(a):
    hi = a.astype(BF16)
    r = a - hi.astype(F32)
    mid = r.astype(BF16)
    lo = (r - mid.astype(F32)).astype(BF16)
    return hi, mid, lo


def _rms_rows(x, g):
    return x * lax.rsqrt(jnp.mean(x * x, axis=-1, keepdims=True) + EPS) * g


def _log_sigmoid(z):
    return jnp.minimum(z, 0.0) - jnp.log1p(jnp.exp(-jnp.abs(z)))


def _sigmoid(z):
    return 1.0 / (1.0 + jnp.exp(-z))


def _mlp_kernel(*refs, has_bias):
    if has_bias:
        x_ref, a_ref, wo_ref, bo_ref, g_ref, w1_ref, w2_ref, o_ref = refs
    else:
        x_ref, a_ref, wo_ref, g_ref, w1_ref, w2_ref, o_ref = refs
    x1 = x_ref[...] + _dot(a_ref[...], wo_ref[...])
    if has_bias:
        x1 = x1 + bo_ref[...]
    hn = _rms_rows(x1, g_ref[...]).astype(BF16)
    acc = x1
    for c in range(D_FF // FF_TILE):
        h = _dot(hn, w1_ref[:, c * FF_TILE:(c + 1) * FF_TILE])
        h = jnp.maximum(h, 0.0)
        acc = acc + _dot((h * h).astype(BF16), w2_ref[c * FF_TILE:(c + 1) * FF_TILE, :])
    o_ref[...] = acc


def _mlp_call(x, a, w_o, b_o, g, w1, w2):
    t, d = x.shape
    din = a.shape[1]
    has_bias = b_o is not None
    row = lambda i: (i, 0)
    in_specs = [pl.BlockSpec((TOK_TILE, d), row), pl.BlockSpec((TOK_TILE, din), row), _const_spec((din, d))]
    args = [x, a, w_o]
    if has_bias:
        in_specs.append(_const_spec((1, d)))
        args.append(b_o)
    in_specs += [_const_spec((1, d)), _const_spec((d, D_FF)), _const_spec((D_FF, d))]
    args += [g, w1, w2]
    return pl.pallas_call(
        functools.partial(_mlp_kernel, has_bias=has_bias),
        grid=(t // TOK_TILE,),
        in_specs=in_specs,
        out_specs=pl.BlockSpec((TOK_TILE, d), row),
        out_shape=jax.ShapeDtypeStruct((t, d), F32),
        compiler_params=_cparams("parallel"),
        name="mlp",
    )(*args)


def _mla_proj_kernel(x_ref, g_ref, wdq_ref, qn_ref, wuq_ref, wdkv_ref, kvn_ref, wuk_ref, wuv_ref,
                     gq_ref, gk_ref, cos_ref, sa_ref, sb_ref, q_ref, k_ref, v_ref):
    hn = _rms_rows(x_ref[...], g_ref[...]).astype(BF16)
    cq = _rms_rows(_dot(hn, wdq_ref[...]), qn_ref[...]).astype(BF16)
    q = _dot(cq, wuq_ref[...])
    dkv = _dot(hn, wdkv_ref[...])
    ckv = _rms_rows(dkv[:, :MLA_KV_RANK], kvn_ref[...]).astype(BF16)
    kr = dkv[:, MLA_KV_RANK:]
    kn = _dot(ckv, wuk_ref[...])
    v_ref[...] = _dot(ckv, wuv_ref[...]).astype(BF16)
    cos, sa, sb = cos_ref[...], sa_ref[...], sb_ref[...]
    gq, gk = gq_ref[...], gk_ref[...]

    def norm_rope(t, gain):
        t = t * lax.rsqrt(jnp.sum(t * t, axis=-1, keepdims=True) * (1.0 / MLA_QK) + EPS) * gain
        return t * cos + pltpu.roll(t, LANES - MLA_ROPE // 2, 1) * sa + pltpu.roll(t, MLA_ROPE // 2, 1) * sb

    for h in range(MLA_HEADS):
        sl = slice(h * LANES, (h + 1) * LANES)
        q_ref[:, sl] = norm_rope(q[:, sl], gq).astype(BF16)
        k_ref[:, sl] = norm_rope(kn[:, sl] + kr, gk).astype(BF16)


def _attn_kernel(q_ref, k_ref, v_ref, o_ref):
    i = pl.program_id(2)
    tq = tk = ATT_TILE
    row = lax.broadcasted_iota(jnp.int32, (tq, tk), 0)
    col = lax.broadcasted_iota(jnp.int32, (tq, tk), 1)
    diag_mask = (col // CHUNK) <= (row // CHUNK)

    for h in range(2):
        q = q_ref[:, h * LANES:(h + 1) * LANES]

        def block(start, carry, masked):
            m, l, acc = carry
            ks = k_ref[pl.ds(start, tk), h * LANES:(h + 1) * LANES]
            vs = v_ref[pl.ds(start, tk), h * MLA_V:(h + 1) * MLA_V]
            s = _dot_nt(q, ks)
            if masked:
                s = jnp.where(diag_mask, s, NEG_BIG)
            m_new = jnp.maximum(m, jnp.max(s, axis=-1, keepdims=True))
            p = jnp.exp2(s - m_new)
            alpha = jnp.exp2(m - m_new)
            l = alpha * l + jnp.sum(p, axis=-1, keepdims=True)
            acc = alpha * acc + _dot(p.astype(BF16), vs)
            return m_new, l, acc

        init = (jnp.full((tq, 1), NEG_BIG, F32), jnp.zeros((tq, 1), F32), jnp.zeros((tq, MLA_V), F32))
        carry = lax.fori_loop(
            0, i, lambda kb, c: block(pl.multiple_of(kb * tk, tk), c, False), init)
        m, l, acc = block(pl.multiple_of(i * tk, tk), carry, True)
        o_ref[:, h * MLA_V:(h + 1) * MLA_V] = (acc / l).astype(BF16)


def _mla_layer(x, bsz, seq, g_mix, p, g_ffn, w1, w2):
    t, d = x.shape
    hp = MLA_HEADS * LANES
    nt = seq // TOK_TILE
    row = lambda i: (i, 0)
    tab = lambda i: (i % nt, 0)
    q, k, v = pl.pallas_call(
        _mla_proj_kernel,
        grid=(t // TOK_TILE,),
        in_specs=[pl.BlockSpec((TOK_TILE, d), row), _const_spec((1, d)),
                  _const_spec((d, MLA_Q_RANK)), _const_spec((1, MLA_Q_RANK)), _const_spec((MLA_Q_RANK, hp)),
                  _const_spec((d, MLA_KV_RANK + LANES)), _const_spec((1, MLA_KV_RANK)),
                  _const_spec((MLA_KV_RANK, hp)), _const_spec((MLA_KV_RANK, MLA_HEADS * MLA_V)),
                  _const_spec((1, LANES)), _const_spec((1, LANES)),
                  pl.BlockSpec((TOK_TILE, LANES), tab), pl.BlockSpec((TOK_TILE, LANES), tab),
                  pl.BlockSpec((TOK_TILE, LANES), tab)],
        out_specs=[pl.BlockSpec((TOK_TILE, hp), row), pl.BlockSpec((TOK_TILE, hp), row),
                   pl.BlockSpec((TOK_TILE, MLA_HEADS * MLA_V), row)],
        out_shape=[jax.ShapeDtypeStruct((t, hp), BF16), jax.ShapeDtypeStruct((t, hp), BF16),
                   jax.ShapeDtypeStruct((t, MLA_HEADS * MLA_V), BF16)],
        compiler_params=_cparams("parallel"),
        name="mla_proj",
    )(x, g_mix, p["w_dq"], p["q_norm"], p["w_uq"], p["w_dkv"], p["kv_norm"], p["w_uk"], p["w_uv"],
      p["gq"], p["gk"], p["cos"], p["sin_a"], p["sin_b"])

    nq = seq // ATT_TILE
    o = pl.pallas_call(
        _attn_kernel,
        grid=(bsz, MLA_HEADS // 2, nq),
        in_specs=[pl.BlockSpec((ATT_TILE, 2 * LANES), lambda b, h, i: (b * nq + i, h)),
                  pl.BlockSpec((seq, 2 * LANES), lambda b, h, i: (b, h)),
                  pl.BlockSpec((seq, 2 * MLA_V), lambda b, h, i: (b, h))],
        out_specs=pl.BlockSpec((ATT_TILE, 2 * MLA_V), lambda b, h, i: (b * nq + i, h)),
        out_shape=jax.ShapeDtypeStruct((t, MLA_HEADS * MLA_V), BF16),
        compiler_params=_cparams("parallel", "parallel", "arbitrary"),
        name="mla_attn",
    )(q, k, v)
    return _mlp_call(x, o, p["w_o"], None, g_ffn, w1, w2)


def _mla_params(seq, w_dq, q_norm, w_uq, w_dkv, kv_norm, w_ukv, q_gain, k_gain, w_o):
    pad_head = LANES - MLA_QK
    w_uq_p = jnp.pad(w_uq.reshape(MLA_Q_RANK, MLA_HEADS, MLA_QK), ((0, 0), (0, 0), (0, pad_head)))
    w_ukv_r = w_ukv.reshape(MLA_KV_RANK, MLA_HEADS, MLA_NOPE + MLA_V)
    w_uk_p = jnp.pad(w_ukv_r[:, :, :MLA_NOPE], ((0, 0), (0, 0), (0, LANES - MLA_NOPE)))
    w_uv = w_ukv_r[:, :, MLA_NOPE:]
    w_rope = jnp.pad(w_dkv[:, MLA_KV_RANK:], ((0, 0), (MLA_NOPE, pad_head)))
    w_dkv_p = jnp.concatenate([w_dkv[:, :MLA_KV_RANK], w_rope], axis=1)
    qscale = (MLA_QK ** -0.5) * math.log2(math.e)
    gq = jnp.pad(q_gain * qscale, (0, pad_head)).reshape(1, LANES)
    gk = jnp.pad(k_gain, (0, pad_head)).reshape(1, LANES)
    half = MLA_ROPE // 2
    inv_freq = ROPE_BASE ** (-jnp.arange(half, dtype=F32) / half)
    ang = jnp.arange(seq).astype(F32)[:, None] * inv_freq[None, :]
    cos, sin = jnp.cos(ang), jnp.sin(ang)
    zeros = lambda n: jnp.zeros((seq, n), F32)
    cos_t = jnp.concatenate([jnp.ones((seq, MLA_NOPE), F32), cos, cos, jnp.ones((seq, pad_head), F32)], axis=1)
    sin_a = jnp.concatenate([zeros(MLA_NOPE), -sin, zeros(half + pad_head)], axis=1)
    sin_b = jnp.concatenate([zeros(MLA_NOPE + half), sin, zeros(pad_head)], axis=1)
    return dict(w_dq=w_dq.astype(BF16), q_norm=q_norm.reshape(1, -1),
                w_uq=w_uq_p.reshape(MLA_Q_RANK, -1).astype(BF16), w_dkv=w_dkv_p.astype(BF16),
                kv_norm=kv_norm.reshape(1, -1), w_uk=w_uk_p.reshape(MLA_KV_RANK, -1).astype(BF16),
                w_uv=w_uv.reshape(MLA_KV_RANK, -1).astype(BF16), gq=gq, gk=gk,
                cos=cos_t, sin_a=sin_a, sin_b=sin_b, w_o=w_o.astype(BF16))


def _mlstm_proj_kernel(x_ref, g_ref, win_ref, wc_hi_ref, wc_lo_ref, wr_hi_ref, wr_lo_ref, bc_ref, br_ref,
                       q_ref, k_ref, v_ref, o_ref, gc_ref, gr_ref):
    hn = _rms_rows(x_ref[...], g_ref[...])
    hi, lo = _split2(hn)
    y = _dot(hi, win_ref[...])
    hq = ML_HEADS * ML_QK
    q_ref[...] = y[:, :hq].astype(BF16)
    k_ref[...] = (y[:, hq:2 * hq] * (ML_QK ** -0.5)).astype(BF16)
    v_ref[...] = y[:, 2 * hq:2 * hq + ML_HEADS * ML_V].astype(BF16)
    o_ref[...] = y[:, 2 * hq + ML_HEADS * ML_V:].astype(BF16)
    gc_ref[...] = (_dot(hi, wc_hi_ref[...]) + _dot(lo, wc_hi_ref[...]) + _dot(hi, wc_lo_ref[...])) + bc_ref[...]
    gr_ref[...] = (_dot_nt(wr_hi_ref[...], hi) + _dot_nt(wr_hi_ref[...], lo) + _dot_nt(wr_lo_ref[...], hi)) + br_ref[...]


def _cap(g):
    return GATE_CAP * jnp.tanh(g * (1.0 / GATE_CAP))


def _mlstm_rec_kernel(q_ref, k_ref, v_ref, op_ref, gc_ref, gr_ref, hnorm_ref, tril_ref, triu_ref,
                      o_ref, c_ref, m_ref):
    L = REC_TILE

    @pl.when(pl.program_id(1) == 0)
    def _():
        c_ref[...] = jnp.zeros_like(c_ref)
        m_ref[...] = jnp.zeros_like(m_ref)

    gc = _cap(gc_ref[...])
    gr = _cap(gr_ref[...])
    lf_c = _log_sigmoid(gc)
    lf_r = _log_sigmoid(gr)
    c1, c2, c3 = _split3(lf_c)
    r1, r2, r3 = _split3(lf_r)
    tril, triu = tril_ref[...], triu_ref[...]
    bcum_c = _dot(tril, c1) + _dot(tril, c2) + _dot(tril, c3)
    bcum_r = _dot(r1, triu) + _dot(r2, triu) + _dot(r3, triu)
    row = lax.broadcasted_iota(jnp.int32, (L, L), 0)
    col = lax.broadcasted_iota(jnp.int32, (L, L), 1)
    causal = col <= row
    ones = jnp.ones((L, LANES), BF16)

    for h in range(ML_HEADS):
        fh = ML_HEADS + h
        bt = bcum_c[:, fh:fh + 1]
        bs = bcum_r[fh:fh + 1, :]
        li_r = gr[h:h + 1, :]
        li_c = gc[:, h:h + 1]
        m_prev = m_ref[h:h + 1, 0:1]
        q = q_ref[:, h * ML_QK:(h + 1) * ML_QK]
        k = k_ref[:, h * ML_QK:(h + 1) * ML_QK]
        vaug = jnp.concatenate([v_ref[:, h * ML_V:(h + 1) * ML_V], ones], axis=1)
        c_st = c_ref[h]

        dmat = jnp.where(causal, bt - bs + li_r, NEG_BIG)
        inter = bt + m_prev
        m_t = jnp.maximum(inter, jnp.max(dmat, axis=-1, keepdims=True))
        w_intra = jnp.exp(dmat - m_t)
        w_inter = jnp.exp(inter - m_t)
        qk = (_dot_nt(q, k) * w_intra).astype(BF16)
        nd = _dot(qk, vaug) + w_inter * _dot(q, c_st.astype(BF16))
        den = jnp.maximum(jnp.abs(nd[:, ML_V:]), jnp.exp(-m_t))
        hc = nd[:, :ML_V] / jnp.concatenate([den, den], axis=1)

        b_last = bs[:, L - 1:L]
        m_new = jnp.maximum(b_last + m_prev, jnp.max(b_last - bs + li_r, axis=-1, keepdims=True))
        ws = jnp.exp(b_last - bt + li_c - m_new)
        wc = jnp.exp(b_last + m_prev - m_new)
        kw = (k.astype(F32) * ws).astype(BF16)
        c_ref[h] = wc * c_st + _dot_tn(kw, vaug)
        m_ref[h:h + 1, :] = jnp.broadcast_to(m_new, (1, LANES))

        sl = slice(h * ML_V, (h + 1) * ML_V)
        hs = _rms_rows(hc, hnorm_ref[:, sl])
        o_ref[:, sl] = (_sigmoid(op_ref[:, sl].astype(F32)) * hs).astype(BF16)


def _tri_consts(n):
    r = np.arange(n)
    tril = (r[None, :] <= r[:, None]).astype(np.float32)
    return jnp.asarray(tril, BF16), jnp.asarray(tril.T, BF16)


def _mlstm_layer(x, bsz, seq, g_mix, p, g_ffn, w1, w2):
    t, d = x.shape
    hq, hv = ML_HEADS * ML_QK, ML_HEADS * ML_V
    row = lambda i: (i, 0)
    q, k, v, opre, gcol, grow = pl.pallas_call(
        _mlstm_proj_kernel,
        grid=(t // TOK_TILE,),
        in_specs=[pl.BlockSpec((TOK_TILE, d), row), _const_spec((1, d)), _const_spec((d, 2 * hq + 2 * hv)),
                  _const_spec((d, LANES)), _const_spec((d, LANES)),
                  _const_spec((SUBLANES, d)), _const_spec((SUBLANES, d)),
                  _const_spec((1, LANES)), _const_spec((SUBLANES, 1))],
        out_specs=[pl.BlockSpec((TOK_TILE, hq), row), pl.BlockSpec((TOK_TILE, hq), row),
                   pl.BlockSpec((TOK_TILE, hv), row), pl.BlockSpec((TOK_TILE, hv), row),
                   pl.BlockSpec((TOK_TILE, LANES), row), pl.BlockSpec((SUBLANES, TOK_TILE), lambda i: (0, i))],
        out_shape=[jax.ShapeDtypeStruct((t, hq), BF16), jax.ShapeDtypeStruct((t, hq), BF16),
                   jax.ShapeDtypeStruct((t, hv), BF16), jax.ShapeDtypeStruct((t, hv), BF16),
                   jax.ShapeDtypeStruct((t, LANES), F32), jax.ShapeDtypeStruct((SUBLANES, t), F32)],
        compiler_params=_cparams("parallel"),
        name="mlstm_proj",
    )(x, g_mix, p["w_in"], p["wc_hi"], p["wc_lo"], p["wr_hi"], p["wr_lo"], p["b_col"], p["b_row"])

    nc = seq // REC_TILE
    tril, triu = _tri_consts(REC_TILE)
    blk = lambda b, c: (b * nc + c, 0)
    a = pl.pallas_call(
        _mlstm_rec_kernel,
        grid=(bsz, nc),
        in_specs=[pl.BlockSpec((REC_TILE, hq), blk), pl.BlockSpec((REC_TILE, hq), blk),
                  pl.BlockSpec((REC_TILE, hv), blk), pl.BlockSpec((REC_TILE, hv), blk),
                  pl.BlockSpec((REC_TILE, LANES), blk), pl.BlockSpec((SUBLANES, REC_TILE), lambda b, c: (0, b * nc + c)),
                  _const_spec((1, hv)), _const_spec((REC_TILE, REC_TILE)), _const_spec((REC_TILE, REC_TILE))],
        out_specs=pl.BlockSpec((REC_TILE, hv), blk),
        out_shape=jax.ShapeDtypeStruct((t, hv), BF16),
        scratch_shapes=[pltpu.VMEM((ML_HEADS, ML_QK, ML_V + LANES), F32), pltpu.VMEM((SUBLANES, LANES), F32)],
        compiler_params=_cparams("parallel", "arbitrary"),
        name="mlstm_rec",
    )(q, k, v, opre, gcol, grow, p["head_norm"], tril, triu)
    return _mlp_call(x, a, p["w_o"], None, g_ffn, w1, w2)


def _mlstm_params(w_in, w_if, b_if, head_norm, w_o):
    ng = 2 * ML_HEADS
    w_col = jnp.pad(w_if, ((0, 0), (0, LANES - ng)))
    wc_hi = w_col.astype(BF16)
    wc_lo = (w_col - wc_hi.astype(F32)).astype(BF16)
    w_row = w_if.T
    wr_hi = w_row.astype(BF16)
    wr_lo = (w_row - wr_hi.astype(F32)).astype(BF16)
    return dict(w_in=w_in.astype(BF16), wc_hi=wc_hi, wc_lo=wc_lo, wr_hi=wr_hi, wr_lo=wr_lo,
                b_col=jnp.pad(b_if, (0, LANES - ng)).reshape(1, LANES), b_row=b_if.reshape(ng, 1),
                head_norm=head_norm.reshape(1, -1), w_o=w_o.astype(BF16))


def _gla_proj_kernel(x_ref, g_ref, win_ref, wa1_ref, wa2_ref, ba_ref, q_ref, k_ref, v_ref, r_ref, la_ref):
    hn = _rms_rows(x_ref[...], g_ref[...]).astype(BF16)
    y = _dot(hn, win_ref[...])
    hk, hv = GLA_HEADS * GLA_K, GLA_HEADS * GLA_V
    q_ref[...] = (y[:, :hk] * (GLA_K ** -0.5)).astype(BF16)
    k_ref[...] = y[:, hk:2 * hk].astype(BF16)
    v_ref[...] = y[:, 2 * hk:2 * hk + hv].astype(BF16)
    r_ref[...] = y[:, 2 * hk + hv:].astype(BF16)
    z = _dot(_dot(hn, wa1_ref[...]).astype(BF16), wa2_ref[...]) + ba_ref[...]
    la_ref[...] = _log_sigmoid(z) * (1.0 / GLA_TAU)


def _gla_levels():
    return [2 ** j for j in range(1, int(math.log2(REC_TILE)) + 1)]


def _gla_consts():
    n = REC_TILE
    t = np.arange(n)[:, None]
    u = np.arange(n)[None, :]
    mats = []
    for p in _gla_levels():
        first_upper = (t // p) * p + p // 2
        upper = (t % p) >= p // 2
        m_up = (u > first_upper) & (u <= t)
        m_lo = (u > t) & (u <= first_upper)
        mats.append(np.where(upper, m_up, m_lo))
    mats.append(u <= t)
    mats.append(u > t)
    return jnp.asarray(np.stack(mats).astype(np.float32), BF16)


def _gla_rec_kernel(q_ref, k_ref, v_ref, r_ref, la_ref, hnorm_ref, w_ref, o_ref, c_ref):
    L = REC_TILE
    levels = _gla_levels()
    nl = len(levels)

    @pl.when(pl.program_id(1) == 0)
    def _():
        c_ref[...] = jnp.zeros_like(c_ref)

    la_hi, la_lo = _split2(la_ref[...])

    def decay(j):
        w = w_ref[j]
        return jnp.exp(_dot(w, la_hi) + _dot(w, la_lo))

    row = lax.broadcasted_iota(jnp.int32, (L, L), 0)
    col = lax.broadcasted_iota(jnp.int32, (L, L), 1)
    rowk = lax.broadcasted_iota(jnp.int32, (L, GLA_K), 0)
    e_cum = decay(nl)
    e_rev = decay(nl + 1)
    e_lvl = [decay(j) for j in range(nl)]

    for h in range(GLA_HEADS):
        ks = slice(h * GLA_K, (h + 1) * GLA_K)
        vs = slice(h * GLA_V, (h + 1) * GLA_V)
        q = q_ref[:, ks].astype(F32)
        k = k_ref[:, ks].astype(F32)
        v = v_ref[:, vs]
        att = jnp.where(row == col, _dot_nt(q_ref[:, ks], k_ref[:, ks]), 0.0)
        for j, p in enumerate(levels):
            upper = (rowk & (p - 1)) >= (p // 2)
            e = e_lvl[j][:, ks]
            qf = jnp.where(upper, q * e, 0.0).astype(BF16)
            kf = jnp.where(upper, 0.0, k * e).astype(BF16)
            a = _dot_nt(qf, kf)
            if p < L:
                shift = int(math.log2(p))
                a = jnp.where((row >> shift) == (col >> shift), a, 0.0)
            att = att + a
        c_st = c_ref[h]
        o = _dot(att.astype(BF16), v) + _dot_nt((q * e_cum[:, ks]).astype(BF16), c_st.astype(BF16))
        c_ref[h] = e_cum[L - 1:L, ks] * c_st + _dot_tn(v, (k * e_rev[:, ks]).astype(BF16))
        o = _rms_rows(o, hnorm_ref[:, vs])
        rr = r_ref[:, vs].astype(F32)
        o_ref[:, vs] = (o * rr * _sigmoid(rr)).astype(BF16)


def _gla_layer(x, bsz, seq, g_mix, p, g_ffn, w1, w2):
    t, d = x.shape
    hk, hv = GLA_HEADS * GLA_K, GLA_HEADS * GLA_V
    row = lambda i: (i, 0)
    q, k, v, r, la = pl.pallas_call(
        _gla_proj_kernel,
        grid=(t // TOK_TILE,),
        in_specs=[pl.BlockSpec((TOK_TILE, d), row), _const_spec((1, d)), _const_spec((d, 2 * hk + 2 * hv)),
                  _const_spec((d, LANES)), _const_spec((LANES, hk)), _const_spec((1, hk))],
        out_specs=[pl.BlockSpec((TOK_TILE, hk), row), pl.BlockSpec((TOK_TILE, hk), row),
                   pl.BlockSpec((TOK_TILE, hv), row), pl.BlockSpec((TOK_TILE, hv), row),
                   pl.BlockSpec((TOK_TILE, hk), row)],
        out_shape=[jax.ShapeDtypeStruct((t, hk), BF16), jax.ShapeDtypeStruct((t, hk), BF16),
                   jax.ShapeDtypeStruct((t, hv), BF16), jax.ShapeDtypeStruct((t, hv), BF16),
                   jax.ShapeDtypeStruct((t, hk), F32)],
        compiler_params=_cparams("parallel"),
        name="gla_proj",
    )(x, g_mix, p["w_in"], p["w_a1"], p["w_a2"], p["b_a"])

    nc = seq // REC_TILE
    wmats = _gla_consts()
    blk = lambda b, c: (b * nc + c, 0)
    a = pl.pallas_call(
        _gla_rec_kernel,
        grid=(bsz, nc),
        in_specs=[pl.BlockSpec((REC_TILE, hk), blk), pl.BlockSpec((REC_TILE, hk), blk),
                  pl.BlockSpec((REC_TILE, hv), blk), pl.BlockSpec((REC_TILE, hv), blk),
                  pl.BlockSpec((REC_TILE, hk), blk), _const_spec((1, hv)), _const_spec(wmats.shape)],
        out_specs=pl.BlockSpec((REC_TILE, hv), blk),
        out_shape=jax.ShapeDtypeStruct((t, hv), BF16),
        scratch_shapes=[pltpu.VMEM((GLA_HEADS, GLA_V, GLA_K), F32)],
        compiler_params=_cparams("parallel", "arbitrary"),
        name="gla_rec",
    )(q, k, v, r, la, p["head_norm"], wmats)
    return _mlp_call(x, a, p["w_o"], None, g_ffn, w1, w2)


def _gla_params(w_in, w_a1, w_a2, b_a, head_norm, w_o):
    return dict(w_in=w_in.astype(BF16),
                w_a1=jnp.pad(w_a1, ((0, 0), (0, LANES - GLA_GATE_RANK))).astype(BF16),
                w_a2=jnp.pad(w_a2, ((0, LANES - GLA_GATE_RANK), (0, 0))).astype(BF16),
                b_a=b_a.reshape(1, -1), head_norm=head_norm.reshape(1, -1), w_o=w_o.astype(BF16))


def _conv_kernel(x_ref, g_ref, w1_ref, b1_ref, wdw_ref, bdw_ref, lng_ref, lnb_ref, o_ref, u_ref):
    tm = TOK_TILE
    d = D_MODEL

    nbuf = CONV_HALO + tm

    @pl.when(pl.program_id(1) == 0)
    def _():
        u_ref[0, 0:CONV_HALO, :] = jnp.zeros((CONV_HALO, d), F32)

    @pl.when(pl.program_id(1) != 0)
    def _():
        u_ref[0, 0:CONV_HALO, :] = u_ref[0, tm:tm + CONV_HALO, :]

    hn = _rms_rows(x_ref[...], g_ref[...]).astype(BF16)
    y = _dot(hn, w1_ref[...]) + b1_ref[...]
    u_ref[0, CONV_HALO:, :] = y[:, :d] * _sigmoid(y[:, d:])
    full = u_ref[0]
    for s in range(1, SUBLANES):
        u_ref[s] = pltpu.roll(full, nbuf - s, 0)

    lead = CONV_HALO - (CONV_WIDTH - 1)

    def rows(c, carry):
        r0 = pl.multiple_of(c * CONV_ROWS, CONV_ROWS)
        acc = jnp.zeros((CONV_ROWS, d), F32) + bdw_ref[...]
        for j in range(CONV_WIDTH):
            off = lead + j
            start = pl.multiple_of(r0 + (off // SUBLANES) * SUBLANES, SUBLANES)
            acc = acc + u_ref[off % SUBLANES, pl.ds(start, CONV_ROWS), :] * wdw_ref[j:j + 1, :]
        mu = jnp.mean(acc, axis=-1, keepdims=True)
        cen = acc - mu
        var = jnp.mean(cen * cen, axis=-1, keepdims=True)
        z = cen * lax.rsqrt(var + EPS) * lng_ref[...] + lnb_ref[...]
        o_ref[pl.ds(r0, CONV_ROWS), :] = (z * _sigmoid(z)).astype(BF16)
        return carry

    lax.fori_loop(0, tm // CONV_ROWS, rows, 0)


def _conv_layer(x, bsz, seq, g_mix, p, g_ffn, w1, w2):
    t, d = x.shape
    nt = seq // TOK_TILE
    blk = lambda b, i: (b * nt + i, 0)
    c = pl.pallas_call(
        _conv_kernel,
        grid=(bsz, nt),
        in_specs=[pl.BlockSpec((TOK_TILE, d), blk), _const_spec((1, d)), _const_spec((d, 2 * d)),
                  _const_spec((1, 2 * d)), _const_spec((CONV_HALO, d)), _const_spec((1, d)),
                  _const_spec((1, d)), _const_spec((1, d))],
        out_specs=pl.BlockSpec((TOK_TILE, d), blk),
        out_shape=jax.ShapeDtypeStruct((t, d), BF16),
        scratch_shapes=[pltpu.VMEM((SUBLANES, CONV_HALO + TOK_TILE, d), F32)],
        compiler_params=_cparams("parallel", "arbitrary"),
        name="conv",
    )(x, g_mix, p["w_pw1"], p["b_pw1"], p["w_dw"], p["b_dw"], p["ln_g"], p["ln_b"])
    return _mlp_call(x, c, p["w_pw2"], p["b_pw2"], g_ffn, w1, w2)


def _conv_params(w_pw1, b_pw1, w_dw, b_dw, ln_g, ln_b, w_pw2, b_pw2):
    return dict(w_pw1=w_pw1.astype(BF16), b_pw1=b_pw1.reshape(1, -1),
                w_dw=jnp.pad(w_dw, ((0, CONV_HALO - CONV_WIDTH), (0, 0))), b_dw=b_dw.reshape(1, -1),
                ln_g=ln_g.reshape(1, -1), ln_b=ln_b.reshape(1, -1),
                w_pw2=w_pw2.astype(BF16), b_pw2=b_pw2.reshape(1, -1))


def kernel(x, norm_mix, norm_ffn, mla_w_dq, mla_q_norm, mla_w_uq, mla_w_dkv, mla_kv_norm, mla_w_ukv, mla_q_gain, mla_k_gain, mla_w_o, mlstm_w_in, mlstm_w_if, mlstm_b_if, mlstm_head_norm, mlstm_w_o, gla_w_in, gla_w_a1, gla_w_a2, gla_b_a, gla_head_norm, gla_w_o, conv_w_pw1, conv_b_pw1, conv_w_dw, conv_b_dw, conv_ln_g, conv_ln_b, conv_w_pw2, conv_b_pw2, ffn_w1, ffn_w2):
    bsz, seq, d = x.shape
    depth = norm_mix.shape[0]
    assert d == D_MODEL and seq % ATT_TILE == 0 and seq % TOK_TILE == 0 and seq % REC_TILE == 0
    h = x.reshape(bsz * seq, d)
    for i in range(depth):
        kind, j = i % 4, i // 4
        g_mix = norm_mix[i].reshape(1, d)
        g_ffn = norm_ffn[i].reshape(1, d)
        w1 = ffn_w1[i].astype(BF16)
        w2 = ffn_w2[i].astype(BF16)
        if kind == 0:
            p = _mla_params(seq, mla_w_dq[j], mla_q_norm[j], mla_w_uq[j], mla_w_dkv[j], mla_kv_norm[j],
                            mla_w_ukv[j], mla_q_gain[j], mla_k_gain[j], mla_w_o[j])
            h = _mla_layer(h, bsz, seq, g_mix, p, g_ffn, w1, w2)
        elif kind == 1:
            p = _mlstm_params(mlstm_w_in[j], mlstm_w_if[j], mlstm_b_if[j], mlstm_head_norm[j], mlstm_w_o[j])
            h = _mlstm_layer(h, bsz, seq, g_mix, p, g_ffn, w1, w2)
        elif kind == 2:
            p = _gla_params(gla_w_in[j], gla_w_a1[j], gla_w_a2[j], gla_b_a[j], gla_head_norm[j], gla_w_o[j])
            h = _gla_layer(h, bsz, seq, g_mix, p, g_ffn, w1, w2)
        else:
            p = _conv_params(conv_w_pw1[j], conv_b_pw1[j], conv_w_dw[j], conv_b_dw[j], conv_ln_g[j],
                             conv_ln_b[j], conv_w_pw2[j], conv_b_pw2[j])
            h = _conv_layer(h, bsz, seq, g_mix, p, g_ffn, w1, w2)
    return h.reshape(bsz, seq, d)
```

```python
import functools
import math

import numpy as np
import jax
import jax.numpy as jnp
from jax import lax
from jax.experimental import pallas as pl
from jax.experimental.pallas import tpu as pltpu

F32 = jnp.float32
BF16 = jnp.bfloat16

D_MODEL = 1024
D_FF = 4 * D_MODEL
EPS = 1e-6
CHUNK = 64

MLA_HEADS = 16
MLA_NOPE = 64
MLA_ROPE = 32
MLA_QK = MLA_NOPE + MLA_ROPE
MLA_V = 64
MLA_Q_RANK = 384
MLA_KV_RANK = 256
ROPE_BASE = 10000.0
MLA_VROWS = MLA_V + 16

ML_HEADS = 4
ML_QK = D_MODEL // 8
ML_V = D_MODEL // 4
GATE_CAP = 15.0

GLA_HEADS = 4
GLA_K = D_MODEL // 8
GLA_V = D_MODEL // 4
GLA_GATE_RANK = 16
GLA_TAU = 16.0

CONV_WIDTH = 31

LANES = 128
SUBLANES = 8
VMEM_LIMIT = 56 * 1024 * 1024

TOK_TILE = 512
FF_TILE = 1024
ATT_TILE = 512
REC_TILE = 256
CONV_HALO = 32
CONV_ROWS = 32
NEG_BIG = -1e30


def _cparams(*sem):
    return pltpu.CompilerParams(dimension_semantics=sem, vmem_limit_bytes=VMEM_LIMIT)


def _const_spec(shape):
    nd = len(shape)
    return pl.BlockSpec(shape, lambda *_: (0,) * nd, pipeline_mode=pl.Buffered(1))


def _dot(a, b):
    return jnp.dot(a, b, preferred_element_type=F32)


def _dot_nt(a, b):
    return lax.dot_general(a, b, (((1,), (1,)), ((), ())), preferred_element_type=F32)


def _dot_tn(a, b):
    return lax.dot_general(a, b, (((0,), (0,)), ((), ())), preferred_element_type=F32)


def _split2(a):
    hi = a.astype(BF16)
    lo = (a - hi.astype(F32)).astype(BF16)
    return hi, lo


def _split3(a):
    hi = a.astype(BF16)
    r = a - hi.astype(F32)
    mid = r.astype(BF16)
    lo = (r - mid.astype(F32)).astype(BF16)
    return hi, mid, lo


def _rms_rows(x, g):
    return x * lax.rsqrt(jnp.mean(x * x, axis=-1, keepdims=True) + EPS) * g


def _log_sigmoid(z):
    return jnp.minimum(z, 0.0) - jnp.log1p(jnp.exp(-jnp.abs(z)))


def _sigmoid(z):
    return 1.0 / (1.0 + jnp.exp(-z))


def _mlp_kernel(*refs, has_bias, a_transposed):
    if has_bias:
        x_ref, a_ref, wo_ref, bo_ref, g_ref, w1_ref, w2_ref, o_ref = refs
    else:
        x_ref, a_ref, wo_ref, g_ref, w1_ref, w2_ref, o_ref = refs
    if a_transposed:
        x1 = x_ref[...] + _dot_tn(a_ref[...], wo_ref[...])
    else:
        x1 = x_ref[...] + _dot(a_ref[...], wo_ref[...])
    if has_bias:
        x1 = x1 + bo_ref[...]
    hn = _rms_rows(x1, g_ref[...]).astype(BF16)
    acc = x1
    for c in range(D_FF // FF_TILE):
        h = _dot(hn, w1_ref[:, c * FF_TILE:(c + 1) * FF_TILE])
        h = jnp.maximum(h, 0.0)
        acc = acc + _dot((h * h).astype(BF16), w2_ref[c * FF_TILE:(c + 1) * FF_TILE, :])
    o_ref[...] = acc


def _mlp_call(x, a, w_o, b_o, g, w1, w2, a_transposed=False):
    t, d = x.shape
    din = w_o.shape[0]
    has_bias = b_o is not None
    row = lambda i: (i, 0)
    a_spec = pl.BlockSpec((din, TOK_TILE), lambda i: (0, i)) if a_transposed else pl.BlockSpec((TOK_TILE, din), row)
    in_specs = [pl.BlockSpec((TOK_TILE, d), row), a_spec, _const_spec((din, d))]
    args = [x, a, w_o]
    if has_bias:
        in_specs.append(_const_spec((1, d)))
        args.append(b_o)
    in_specs += [_const_spec((1, d)), _const_spec((d, D_FF)), _const_spec((D_FF, d))]
    args += [g, w1, w2]
    return pl.pallas_call(
        functools.partial(_mlp_kernel, has_bias=has_bias, a_transposed=a_transposed),
        grid=(t // TOK_TILE,),
        in_specs=in_specs,
        out_specs=pl.BlockSpec((TOK_TILE, d), row),
        out_shape=jax.ShapeDtypeStruct((t, d), F32),
        compiler_params=_cparams("parallel"),
        name="mlp",
    )(*args)


def _mla_proj_kernel(x_ref, g_ref, wdq_ref, qn_ref, wuqt_ref, wdkv_ref, kvn_ref, wuk_ref, wuvt_ref, hsum_ref,
                     vone_ref, gkn_ref, qc_ref, qsa_ref, qsb_ref, kc_ref, ksa_ref, ksb_ref,
                     qt_ref, k_ref, vt_ref):
    half = MLA_ROPE // 2
    hn = _rms_rows(x_ref[...], g_ref[...]).astype(BF16)
    cq = _rms_rows(_dot(hn, wdq_ref[...]), qn_ref[...]).astype(BF16)
    dkv = _dot(hn, wdkv_ref[...])
    ckv = _rms_rows(dkv[:, :MLA_KV_RANK], kvn_ref[...]).astype(BF16)

    vt_ref[0] = (_dot_nt(wuvt_ref[...], ckv) + vone_ref[...]).astype(BF16)

    qt = _dot_nt(wuqt_ref[...], cq)
    qc, qsa, qsb = qc_ref[...], qsa_ref[...], qsb_ref[...]
    for h in range(MLA_HEADS):
        t = qt[h * LANES:(h + 1) * LANES]
        r = lax.rsqrt(jnp.sum(t * t, axis=0, keepdims=True) * (1.0 / MLA_QK) + EPS)
        up = jnp.concatenate([t[half:], t[:half]], axis=0)
        dn = jnp.concatenate([t[LANES - half:], t[:LANES - half]], axis=0)
        qt_ref[h * LANES:(h + 1) * LANES, :] = ((t * qc + up * qsa + dn * qsb) * r).astype(BF16)

    kr = dkv[:, MLA_KV_RANK:]
    kn = _dot(ckv, wuk_ref[...])
    ssq = _dot((kn * kn).astype(BF16), hsum_ref[...]) + jnp.sum(kr * kr, axis=-1, keepdims=True)
    rk = lax.rsqrt(ssq * (1.0 / MLA_QK) + EPS)
    krr = (kr * kc_ref[...] + pltpu.roll(kr, LANES - half, 1) * ksa_ref[...] + pltpu.roll(kr, half, 1) * ksb_ref[...])
    gkn = gkn_ref[...]
    for h in range(MLA_HEADS):
        sl = slice(h * LANES, (h + 1) * LANES)
        k_ref[:, sl] = ((kn[:, sl] * gkn + krr) * rk[:, h:h + 1]).astype(BF16)


def _attn_kernel(qt_ref, k_ref, vt_ref, o_ref, s_ref):
    i = pl.program_id(2)
    tq = tk = ATT_TILE
    vrows = vt_ref.shape[1] // 2
    krow = lax.broadcasted_iota(jnp.int32, (tk, tq), 0)
    qcol = lax.broadcasted_iota(jnp.int32, (tk, tq), 1)
    diag_mask = (krow // CHUNK) <= (qcol // CHUNK)

    def produce(h, kb):
        start = pl.multiple_of(kb * tk, tk)
        st = _dot(k_ref[pl.ds(start, tk), h * LANES:(h + 1) * LANES], qt_ref[h * LANES:(h + 1) * LANES, :])
        s_ref[h] = st
        return jnp.max(st, axis=0, keepdims=True)

    def consume(h, kb, m, acc, mblk, masked):
        st = s_ref[h]
        if masked:
            st = jnp.where(diag_mask, st, NEG_BIG)
            mblk = jnp.max(st, axis=0, keepdims=True)
        m_new = jnp.maximum(m, mblk)
        pt = jnp.exp2(st - m_new).astype(BF16)
        acc = jnp.exp2(m - m_new) * acc + _dot(vt_ref[kb, h * vrows:(h + 1) * vrows, :], pt)
        return m_new, acc

    def body(j, carry):
        m0, a0, m1, a1, mb0 = carry
        mb1 = produce(1, j)
        m0, a0 = consume(0, j, m0, a0, mb0, False)
        mb0 = produce(0, j + 1)
        m1, a1 = consume(1, j, m1, a1, mb1, False)
        return m0, a0, m1, a1, mb0

    m_init = jnp.full((1, tq), NEG_BIG, F32)
    a_init = jnp.zeros((vrows, tq), F32)
    m0, a0, m1, a1, _ = lax.fori_loop(0, i, body, (m_init, a_init, m_init, a_init, produce(0, 0)))
    produce(1, i)
    for h, (m, acc) in enumerate(((m0, a0), (m1, a1))):
        _, acc = consume(h, i, m, acc, None, True)
        o_ref[h * MLA_V:(h + 1) * MLA_V, :] = (acc[:MLA_V] / acc[MLA_V:MLA_V + 1]).astype(BF16)


def _mla_layer(x, bsz, seq, g_mix, p, g_ffn, w1, w2):
    t, d = x.shape
    assert TOK_TILE == ATT_TILE
    hp = MLA_HEADS * LANES
    hv = MLA_HEADS * MLA_VROWS
    nt = seq // TOK_TILE
    row = lambda i: (i, 0)
    ttab = lambda i: (i % nt, 0)
    ftab = lambda i: (0, i % nt)
    qt, k, vt = pl.pallas_call(
        _mla_proj_kernel,
        grid=(t // TOK_TILE,),
        in_specs=[pl.BlockSpec((TOK_TILE, d), row), _const_spec((1, d)),
                  _const_spec((d, MLA_Q_RANK)), _const_spec((1, MLA_Q_RANK)), _const_spec((hp, MLA_Q_RANK)),
                  _const_spec((d, MLA_KV_RANK + LANES)), _const_spec((1, MLA_KV_RANK)),
                  _const_spec((MLA_KV_RANK, hp)), _const_spec((hv, MLA_KV_RANK)), _const_spec((hp, LANES)),
                  _const_spec((hv, TOK_TILE)), _const_spec((1, LANES)),
                  pl.BlockSpec((LANES, TOK_TILE), ftab), pl.BlockSpec((LANES, TOK_TILE), ftab),
                  pl.BlockSpec((LANES, TOK_TILE), ftab),
                  pl.BlockSpec((TOK_TILE, LANES), ttab), pl.BlockSpec((TOK_TILE, LANES), ttab),
                  pl.BlockSpec((TOK_TILE, LANES), ttab)],
        out_specs=[pl.BlockSpec((hp, TOK_TILE), lambda i: (0, i)), pl.BlockSpec((TOK_TILE, hp), row),
                   pl.BlockSpec((1, hv, TOK_TILE), lambda i: (i, 0, 0))],
        out_shape=[jax.ShapeDtypeStruct((hp, t), BF16), jax.ShapeDtypeStruct((t, hp), BF16),
                   jax.ShapeDtypeStruct((t // TOK_TILE, hv, TOK_TILE), BF16)],
        compiler_params=_cparams("parallel"),
        name="mla_proj",
    )(x, g_mix, p["w_dq"], p["q_norm"], p["w_uq_t"], p["w_dkv"], p["kv_norm"], p["w_uk"], p["w_uv_t"], p["hsum"],
      p["vone"], p["gkn"], p["qc"], p["qsa"], p["qsb"], p["kc"], p["ksa"], p["ksb"])

    nq = seq // ATT_TILE
    ot = pl.pallas_call(
        _attn_kernel,
        grid=(bsz, MLA_HEADS // 2, nq),
        in_specs=[pl.BlockSpec((2 * LANES, ATT_TILE), lambda b, h, i: (h, b * nq + i)),
                  pl.BlockSpec((seq, 2 * LANES), lambda b, h, i: (b, h)),
                  pl.BlockSpec((nq, 2 * MLA_VROWS, ATT_TILE), lambda b, h, i: (b, h, 0))],
        out_specs=pl.BlockSpec((2 * MLA_V, ATT_TILE), lambda b, h, i: (h, b * nq + i)),
        out_shape=jax.ShapeDtypeStruct((MLA_HEADS * MLA_V, t), BF16),
        scratch_shapes=[pltpu.VMEM((2, ATT_TILE, ATT_TILE), F32)],
        compiler_params=_cparams("parallel", "parallel", "arbitrary"),
        name="mla_attn",
    )(qt, k, vt)
    return _mlp_call(x, ot, p["w_o"], None, g_ffn, w1, w2, a_transposed=True)


def _mla_params(seq, w_dq, q_norm, w_uq, w_dkv, kv_norm, w_ukv, q_gain, k_gain, w_o):
    pad_head = LANES - MLA_QK
    half = MLA_ROPE // 2
    w_uq_p = jnp.pad(w_uq.reshape(MLA_Q_RANK, MLA_HEADS, MLA_QK), ((0, 0), (0, 0), (0, pad_head)))
    w_ukv_r = w_ukv.reshape(MLA_KV_RANK, MLA_HEADS, MLA_NOPE + MLA_V)
    w_uk_p = jnp.pad(w_ukv_r[:, :, :MLA_NOPE], ((0, 0), (0, 0), (0, LANES - MLA_NOPE)))
    w_uv_p = jnp.pad(w_ukv_r[:, :, MLA_NOPE:], ((0, 0), (0, 0), (0, MLA_VROWS - MLA_V)))
    w_rope = jnp.pad(w_dkv[:, MLA_KV_RANK:], ((0, 0), (MLA_NOPE, pad_head)))
    w_dkv_p = jnp.concatenate([w_dkv[:, :MLA_KV_RANK], w_rope], axis=1)
    lane = np.arange(MLA_HEADS * LANES)
    hsum = (lane[:, None] // LANES == np.arange(LANES)[None, :]).astype(np.float32)
    vone = np.zeros((MLA_HEADS * MLA_VROWS, TOK_TILE), np.float32)
    vone[np.arange(MLA_HEADS) * MLA_VROWS + MLA_V] = 1.0
    gq = q_gain * ((MLA_QK ** -0.5) * math.log2(math.e))
    inv_freq = ROPE_BASE ** (-jnp.arange(half, dtype=F32) / half)
    ang = jnp.arange(seq).astype(F32)[:, None] * inv_freq[None, :]
    cos, sin = jnp.cos(ang), jnp.sin(ang)
    g1 = lambda g: g[MLA_NOPE:MLA_NOPE + half][None, :]
    g2 = lambda g: g[MLA_NOPE + half:MLA_QK][None, :]
    z = lambda n: jnp.zeros((seq, n), F32)

    def tables(g, nope_gain):
        nope = jnp.broadcast_to(g[None, :MLA_NOPE], (seq, MLA_NOPE)) if nope_gain else z(MLA_NOPE)
        c = jnp.concatenate([nope, g1(g) * cos, g2(g) * cos, z(pad_head)], axis=1)
        sa = jnp.concatenate([z(MLA_NOPE), -g2(g) * sin, z(half + pad_head)], axis=1)
        sb = jnp.concatenate([z(MLA_NOPE + half), g1(g) * sin, z(pad_head)], axis=1)
        return c, sa, sb

    qc, qsa, qsb = tables(gq, True)
    kc, ksa, ksb = tables(k_gain, False)
    return dict(w_dq=w_dq.astype(BF16), q_norm=q_norm.reshape(1, -1),
                w_uq_t=w_uq_p.reshape(MLA_Q_RANK, -1).T.astype(BF16), w_dkv=w_dkv_p.astype(BF16),
                kv_norm=kv_norm.reshape(1, -1), w_uk=w_uk_p.reshape(MLA_KV_RANK, -1).astype(BF16),
                w_uv_t=w_uv_p.reshape(MLA_KV_RANK, -1).T.astype(BF16),
                hsum=jnp.asarray(hsum, BF16), vone=jnp.asarray(vone, BF16),
                gkn=jnp.pad(k_gain[:MLA_NOPE], (0, LANES - MLA_NOPE)).reshape(1, LANES),
                qc=qc.T, qsa=qsa.T, qsb=qsb.T, kc=kc, ksa=ksa, ksb=ksb, w_o=w_o.astype(BF16))


def _mlstm_proj_kernel(x_ref, g_ref, win_ref, wc_hi_ref, wc_lo_ref, wr_hi_ref, wr_lo_ref, bc_ref, br_ref,
                       q_ref, k_ref, v_ref, o_ref, gc_ref, gr_ref):
    hn = _rms_rows(x_ref[...], g_ref[...])
    hi, lo = _split2(hn)
    y = _dot(hi, win_ref[...])
    hq = ML_HEADS * ML_QK
    q_ref[...] = y[:, :hq].astype(BF16)
    k_ref[...] = (y[:, hq:2 * hq] * (ML_QK ** -0.5)).astype(BF16)
    v_ref[...] = y[:, 2 * hq:2 * hq + ML_HEADS * ML_V].astype(BF16)
    o_ref[...] = y[:, 2 * hq + ML_HEADS * ML_V:].astype(BF16)
    gc_ref[...] = (_dot(hi, wc_hi_ref[...]) + _dot(lo, wc_hi_ref[...]) + _dot(hi, wc_lo_ref[...])) + bc_ref[...]
    gr_ref[...] = (_dot_nt(wr_hi_ref[...], hi) + _dot_nt(wr_hi_ref[...], lo) + _dot_nt(wr_lo_ref[...], hi)) + br_ref[...]


def _cap(g):
    return GATE_CAP * jnp.tanh(g * (1.0 / GATE_CAP))


def _mlstm_rec_kernel(q_ref, k_ref, v_ref, op_ref, gc_ref, gr_ref, hnorm_ref, tril_ref, triu_ref,
                      o_ref, c_ref, m_ref):
    L = REC_TILE

    @pl.when(pl.program_id(1) == 0)
    def _():
        c_ref[...] = jnp.zeros_like(c_ref)
        m_ref[...] = jnp.zeros_like(m_ref)

    gc = _cap(gc_ref[...])
    gr = _cap(gr_ref[...])
    lf_c = _log_sigmoid(gc)
    lf_r = _log_sigmoid(gr)
    c1, c2, c3 = _split3(lf_c)
    r1, r2, r3 = _split3(lf_r)
    tril, triu = tril_ref[...], triu_ref[...]
    bcum_c = _dot(tril, c1) + _dot(tril, c2) + _dot(tril, c3)
    bcum_r = _dot(r1, triu) + _dot(r2, triu) + _dot(r3, triu)
    row = lax.broadcasted_iota(jnp.int32, (L, L), 0)
    col = lax.broadcasted_iota(jnp.int32, (L, L), 1)
    causal = col <= row
    ones = jnp.ones((L, LANES), BF16)

    for h in range(ML_HEADS):
        fh = ML_HEADS + h
        bt = bcum_c[:, fh:fh + 1]
        bs = bcum_r[fh:fh + 1, :]
        li_r = gr[h:h + 1, :]
        li_c = gc[:, h:h + 1]
        m_prev = m_ref[h:h + 1, 0:1]
        q = q_ref[:, h * ML_QK:(h + 1) * ML_QK]
        k = k_ref[:, h * ML_QK:(h + 1) * ML_QK]
        vaug = jnp.concatenate([v_ref[:, h * ML_V:(h + 1) * ML_V], ones], axis=1)
        c_st = c_ref[h]

        dmat = jnp.where(causal, bt - bs + li_r, NEG_BIG)
        inter = bt + m_prev
        m_t = jnp.maximum(inter, jnp.max(dmat, axis=-1, keepdims=True))
        w_intra = jnp.exp(dmat - m_t)
        w_inter = jnp.exp(inter - m_t)
        qk = (_dot_nt(q, k) * w_intra).astype(BF16)
        nd = _dot(qk, vaug) + w_inter * _dot(q, c_st.astype(BF16))
        den = jnp.maximum(jnp.abs(nd[:, ML_V:]), jnp.exp(-m_t))
        hc = nd[:, :ML_V] / jnp.concatenate([den, den], axis=1)

        b_last = bs[:, L - 1:L]
        m_new = jnp.maximum(b_last + m_prev, jnp.max(b_last - bs + li_r, axis=-1, keepdims=True))
        ws = jnp.exp(b_last - bt + li_c - m_new)
        wc = jnp.exp(b_last + m_prev - m_new)
        kw = (k.astype(F32) * ws).astype(BF16)
        c_ref[h] = wc * c_st + _dot_tn(kw, vaug)
        m_ref[h:h + 1, :] = jnp.broadcast_to(m_new, (1, LANES))

        sl = slice(h * ML_V, (h + 1) * ML_V)
        hs = _rms_rows(hc, hnorm_ref[:, sl])
        o_ref[:, sl] = (_sigmoid(op_ref[:, sl].astype(F32)) * hs).astype(BF16)


def _tri_consts(n):
    r = np.arange(n)
    tril = (r[None, :] <= r[:, None]).astype(np.float32)
    return jnp.asarray(tril, BF16), jnp.asarray(tril.T, BF16)


def _mlstm_layer(x, bsz, seq, g_mix, p, g_ffn, w1, w2):
    t, d = x.shape
    hq, hv = ML_HEADS * ML_QK, ML_HEADS * ML_V
    row = lambda i: (i, 0)
    q, k, v, opre, gcol, grow = pl.pallas_call(
        _mlstm_proj_kernel,
        grid=(t // TOK_TILE,),
        in_specs=[pl.BlockSpec((TOK_TILE, d), row), _const_spec((1, d)), _const_spec((d, 2 * hq + 2 * hv)),
                  _const_spec((d, LANES)), _const_spec((d, LANES)),
                  _const_spec((SUBLANES, d)), _const_spec((SUBLANES, d)),
                  _const_spec((1, LANES)), _const_spec((SUBLANES, 1))],
        out_specs=[pl.BlockSpec((TOK_TILE, hq), row), pl.BlockSpec((TOK_TILE, hq), row),
                   pl.BlockSpec((TOK_TILE, hv), row), pl.BlockSpec((TOK_TILE, hv), row),
                   pl.BlockSpec((TOK_TILE, LANES), row), pl.BlockSpec((SUBLANES, TOK_TILE), lambda i: (0, i))],
        out_shape=[jax.ShapeDtypeStruct((t, hq), BF16), jax.ShapeDtypeStruct((t, hq), BF16),
                   jax.ShapeDtypeStruct((t, hv), BF16), jax.ShapeDtypeStruct((t, hv), BF16),
                   jax.ShapeDtypeStruct((t, LANES), F32), jax.ShapeDtypeStruct((SUBLANES, t), F32)],
        compiler_params=_cparams("parallel"),
        name="mlstm_proj",
    )(x, g_mix, p["w_in"], p["wc_hi"], p["wc_lo"], p["wr_hi"], p["wr_lo"], p["b_col"], p["b_row"])

    nc = seq // REC_TILE
    tril, triu = _tri_consts(REC_TILE)
    blk = lambda b, c: (b * nc + c, 0)
    a = pl.pallas_call(
        _mlstm_rec_kernel,
        grid=(bsz, nc),
        in_specs=[pl.BlockSpec((REC_TILE, hq), blk), pl.BlockSpec((REC_TILE, hq), blk),
                  pl.BlockSpec((REC_TILE, hv), blk), pl.BlockSpec((REC_TILE, hv), blk),
                  pl.BlockSpec((REC_TILE, LANES), blk), pl.BlockSpec((SUBLANES, REC_TILE), lambda b, c: (0, b * nc + c)),
                  _const_spec((1, hv)), _const_spec((REC_TILE, REC_TILE)), _const_spec((REC_TILE, REC_TILE))],
        out_specs=pl.BlockSpec((REC_TILE, hv), blk),
        out_shape=jax.ShapeDtypeStruct((t, hv), BF16),
        scratch_shapes=[pltpu.VMEM((ML_HEADS, ML_QK, ML_V + LANES), F32), pltpu.VMEM((SUBLANES, LANES), F32)],
        compiler_params=_cparams("parallel", "arbitrary"),
        name="mlstm_rec",
    )(q, k, v, opre, gcol, grow, p["head_norm"], tril, triu)
    return _mlp_call(x, a, p["w_o"], None, g_ffn, w1, w2)


def _mlstm_params(w_in, w_if, b_if, head_norm, w_o):
    ng = 2 * ML_HEADS
    w_col = jnp.pad(w_if, ((0, 0), (0, LANES - ng)))
    wc_hi = w_col.astype(BF16)
    wc_lo = (w_col - wc_hi.astype(F32)).astype(BF16)
    w_row = w_if.T
    wr_hi = w_row.astype(BF16)
    wr_lo = (w_row - wr_hi.astype(F32)).astype(BF16)
    return dict(w_in=w_in.astype(BF16), wc_hi=wc_hi, wc_lo=wc_lo, wr_hi=wr_hi, wr_lo=wr_lo,
                b_col=jnp.pad(b_if, (0, LANES - ng)).reshape(1, LANES), b_row=b_if.reshape(ng, 1),
                head_norm=head_norm.reshape(1, -1), w_o=w_o.astype(BF16))


def _gla_proj_kernel(x_ref, g_ref, win_ref, wa1_ref, wa2_ref, ba_ref, q_ref, k_ref, v_ref, r_ref, la_ref):
    hn = _rms_rows(x_ref[...], g_ref[...]).astype(BF16)
    y = _dot(hn, win_ref[...])
    hk, hv = GLA_HEADS * GLA_K, GLA_HEADS * GLA_V
    q_ref[...] = (y[:, :hk] * (GLA_K ** -0.5)).astype(BF16)
    k_ref[...] = y[:, hk:2 * hk].astype(BF16)
    v_ref[...] = y[:, 2 * hk:2 * hk + hv].astype(BF16)
    r_ref[...] = y[:, 2 * hk + hv:].astype(BF16)
    z = _dot(_dot(hn, wa1_ref[...]).astype(BF16), wa2_ref[...]) + ba_ref[...]
    la_ref[...] = _log_sigmoid(z) * (1.0 / GLA_TAU)


def _gla_levels():
    return [2 ** j for j in range(1, int(math.log2(REC_TILE)) + 1)]


def _gla_consts():
    n = REC_TILE
    t = np.arange(n)[:, None]
    u = np.arange(n)[None, :]
    mats = []
    for p in _gla_levels():
        first_upper = (t // p) * p + p // 2
        upper = (t % p) >= p // 2
        m_up = (u > first_upper) & (u <= t)
        m_lo = (u > t) & (u <= first_upper)
        mats.append(np.where(upper, m_up, m_lo))
    mats.append(u <= t)
    mats.append(u > t)
    return jnp.asarray(np.stack(mats).astype(np.float32), BF16)


def _gla_rec_kernel(q_ref, k_ref, v_ref, r_ref, la_ref, hnorm_ref, w_ref, o_ref, c_ref):
    L = REC_TILE
    levels = _gla_levels()
    nl = len(levels)

    @pl.when(pl.program_id(1) == 0)
    def _():
        c_ref[...] = jnp.zeros_like(c_ref)

    la_hi, la_lo = _split2(la_ref[...])

    def decay(j):
        w = w_ref[j]
        return jnp.exp(_dot(w, la_hi) + _dot(w, la_lo))

    row = lax.broadcasted_iota(jnp.int32, (L, L), 0)
    col = lax.broadcasted_iota(jnp.int32, (L, L), 1)
    rowk = lax.broadcasted_iota(jnp.int32, (L, GLA_K), 0)
    e_cum = decay(nl)
    e_rev = decay(nl + 1)
    e_lvl = [decay(j) for j in range(nl)]

    for h in range(GLA_HEADS):
        ks = slice(h * GLA_K, (h + 1) * GLA_K)
        vs = slice(h * GLA_V, (h + 1) * GLA_V)
        q = q_ref[:, ks].astype(F32)
        k = k_ref[:, ks].astype(F32)
        v = v_ref[:, vs]
        att = jnp.where(row == col, _dot_nt(q_ref[:, ks], k_ref[:, ks]), 0.0)
        for j, p in enumerate(levels):
            upper = (rowk & (p - 1)) >= (p // 2)
            e = e_lvl[j][:, ks]
            qf = jnp.where(upper, q * e, 0.0).astype(BF16)
            kf = jnp.where(upper, 0.0, k * e).astype(BF16)
            a = _dot_nt(qf, kf)
            if p < L:
                shift = int(math.log2(p))
                a = jnp.where((row >> shift) == (col >> shift), a, 0.0)
            att = att + a
        c_st = c_ref[h]
        o = _dot(att.astype(BF16), v) + _dot_nt((q * e_cum[:, ks]).astype(BF16), c_st.astype(BF16))
        c_ref[h] = e_cum[L - 1:L, ks] * c_st + _dot_tn(v, (k * e_rev[:, ks]).astype(BF16))
        o = _rms_rows(o, hnorm_ref[:, vs])
        rr = r_ref[:, vs].astype(F32)
        o_ref[:, vs] = (o * rr * _sigmoid(rr)).astype(BF16)


def _gla_layer(x, bsz, seq, g_mix, p, g_ffn, w1, w2):
    t, d = x.shape
    hk, hv = GLA_HEADS * GLA_K, GLA_HEADS * GLA_V
    row = lambda i: (i, 0)
    q, k, v, r, la = pl.pallas_call(
        _gla_proj_kernel,
        grid=(t // TOK_TILE,),
        in_specs=[pl.BlockSpec((TOK_TILE, d), row), _const_spec((1, d)), _const_spec((d, 2 * hk + 2 * hv)),
                  _const_spec((d, LANES)), _const_spec((LANES, hk)), _const_spec((1, hk))],
        out_specs=[pl.BlockSpec((TOK_TILE, hk), row), pl.BlockSpec((TOK_TILE, hk), row),
                   pl.BlockSpec((TOK_TILE, hv), row), pl.BlockSpec((TOK_TILE, hv), row),
                   pl.BlockSpec((TOK_TILE, hk), row)],
        out_shape=[jax.ShapeDtypeStruct((t, hk), BF16), jax.ShapeDtypeStruct((t, hk), BF16),
                   jax.ShapeDtypeStruct((t, hv), BF16), jax.ShapeDtypeStruct((t, hv), BF16),
                   jax.ShapeDtypeStruct((t, hk), F32)],
        compiler_params=_cparams("parallel"),
        name="gla_proj",
    )(x, g_mix, p["w_in"], p["w_a1"], p["w_a2"], p["b_a"])

    nc = seq // REC_TILE
    wmats = _gla_consts()
    blk = lambda b, c: (b * nc + c, 0)
    a = pl.pallas_call(
        _gla_rec_kernel,
        grid=(bsz, nc),
        in_specs=[pl.BlockSpec((REC_TILE, hk), blk), pl.BlockSpec((REC_TILE, hk), blk),
                  pl.BlockSpec((REC_TILE, hv), blk), pl.BlockSpec((REC_TILE, hv), blk),
                  pl.BlockSpec((REC_TILE, hk), blk), _const_spec((1, hv)), _const_spec(wmats.shape)],
        out_specs=pl.BlockSpec((REC_TILE, hv), blk),
        out_shape=jax.ShapeDtypeStruct((t, hv), BF16),
        scratch_shapes=[pltpu.VMEM((GLA_HEADS, GLA_V, GLA_K), F32)],
        compiler_params=_cparams("parallel", "arbitrary"),
        name="gla_rec",
    )(q, k, v, r, la, p["head_norm"], wmats)
    return _mlp_call(x, a, p["w_o"], None, g_ffn, w1, w2)


def _gla_params(w_in, w_a1, w_a2, b_a, head_norm, w_o):
    return dict(w_in=w_in.astype(BF16),
                w_a1=jnp.pad(w_a1, ((0, 0), (0, LANES - GLA_GATE_RANK))).astype(BF16),
                w_a2=jnp.pad(w_a2, ((0, LANES - GLA_GATE_RANK), (0, 0))).astype(BF16),
                b_a=b_a.reshape(1, -1), head_norm=head_norm.reshape(1, -1), w_o=w_o.astype(BF16))


def _conv_kernel(x_ref, g_ref, w1_ref, b1_ref, wdw_ref, bdw_ref, lng_ref, lnb_ref, o_ref, u_ref):
    tm = TOK_TILE
    d = D_MODEL

    nbuf = CONV_HALO + tm

    @pl.when(pl.program_id(1) == 0)
    def _():
        u_ref[0, 0:CONV_HALO, :] = jnp.zeros((CONV_HALO, d), F32)

    @pl.when(pl.program_id(1) != 0)
    def _():
        u_ref[0, 0:CONV_HALO, :] = u_ref[0, tm:tm + CONV_HALO, :]

    hn = _rms_rows(x_ref[...], g_ref[...]).astype(BF16)
    y = _dot(hn, w1_ref[...]) + b1_ref[...]
    u_ref[0, CONV_HALO:, :] = y[:, :d] * _sigmoid(y[:, d:])
    full = u_ref[0]
    for s in range(1, SUBLANES):
        u_ref[s] = pltpu.roll(full, nbuf - s, 0)

    lead = CONV_HALO - (CONV_WIDTH - 1)

    def rows(c, carry):
        r0 = pl.multiple_of(c * CONV_ROWS, CONV_ROWS)
        acc = jnp.zeros((CONV_ROWS, d), F32) + bdw_ref[...]
        for j in range(CONV_WIDTH):
            off = lead + j
            start = pl.multiple_of(r0 + (off // SUBLANES) * SUBLANES, SUBLANES)
            acc = acc + u_ref[off % SUBLANES, pl.ds(start, CONV_ROWS), :] * wdw_ref[j:j + 1, :]
        mu = jnp.mean(acc, axis=-1, keepdims=True)
        cen = acc - mu
        var = jnp.mean(cen * cen, axis=-1, keepdims=True)
        z = cen * lax.rsqrt(var + EPS) * lng_ref[...] + lnb_ref[...]
        o_ref[pl.ds(r0, CONV_ROWS), :] = (z * _sigmoid(z)).astype(BF16)
        return carry

    lax.fori_loop(0, tm // CONV_ROWS, rows, 0)


def _conv_layer(x, bsz, seq, g_mix, p, g_ffn, w1, w2):
    t, d = x.shape
    nt = seq // TOK_TILE
    blk = lambda b, i: (b * nt + i, 0)
    c = pl.pallas_call(
        _conv_kernel,
        grid=(bsz, nt),
        in_specs=[pl.BlockSpec((TOK_TILE, d), blk), _const_spec((1, d)), _const_spec((d, 2 * d)),
                  _const_spec((1, 2 * d)), _const_spec((CONV_HALO, d)), _const_spec((1, d)),
                  _const_spec((1, d)), _const_spec((1, d))],
        out_specs=pl.BlockSpec((TOK_TILE, d), blk),
        out_shape=jax.ShapeDtypeStruct((t, d), BF16),
        scratch_shapes=[pltpu.VMEM((SUBLANES, CONV_HALO + TOK_TILE, d), F32)],
        compiler_params=_cparams("parallel", "arbitrary"),
        name="conv",
    )(x, g_mix, p["w_pw1"], p["b_pw1"], p["w_dw"], p["b_dw"], p["ln_g"], p["ln_b"])
    return _mlp_call(x, c, p["w_pw2"], p["b_pw2"], g_ffn, w1, w2)


def _conv_params(w_pw1, b_pw1, w_dw, b_dw, ln_g, ln_b, w_pw2, b_pw2):
    return dict(w_pw1=w_pw1.astype(BF16), b_pw1=b_pw1.reshape(1, -1),
                w_dw=jnp.pad(w_dw, ((0, CONV_HALO - CONV_WIDTH), (0, 0))), b_dw=b_dw.reshape(1, -1),
                ln_g=ln_g.reshape(1, -1), ln_b=ln_b.reshape(1, -1),
                w_pw2=w_pw2.astype(BF16), b_pw2=b_pw2.reshape(1, -1))


def kernel(x, norm_mix, norm_ffn, mla_w_dq, mla_q_norm, mla_w_uq, mla_w_dkv, mla_kv_norm, mla_w_ukv, mla_q_gain, mla_k_gain, mla_w_o, mlstm_w_in, mlstm_w_if, mlstm_b_if, mlstm_head_norm, mlstm_w_o, gla_w_in, gla_w_a1, gla_w_a2, gla_b_a, gla_head_norm, gla_w_o, conv_w_pw1, conv_b_pw1, conv_w_dw, conv_b_dw, conv_ln_g, conv_ln_b, conv_w_pw2, conv_b_pw2, ffn_w1, ffn_w2):
    bsz, seq, d = x.shape
    depth = norm_mix.shape[0]
    assert d == D_MODEL and seq % ATT_TILE == 0 and seq % TOK_TILE == 0 and seq % REC_TILE == 0
    h = x.reshape(bsz * seq, d)
    for i in range(depth):
        kind, j = i % 4, i // 4
        g_mix = norm_mix[i].reshape(1, d)
        g_ffn = norm_ffn[i].reshape(1, d)
        w1 = ffn_w1[i].astype(BF16)
        w2 = ffn_w2[i].astype(BF16)
        if kind == 0:
            p = _mla_params(seq, mla_w_dq[j], mla_q_norm[j], mla_w_uq[j], mla_w_dkv[j], mla_kv_norm[j],
                            mla_w_ukv[j], mla_q_gain[j], mla_k_gain[j], mla_w_o[j])
            h = _mla_layer(h, bsz, seq, g_mix, p, g_ffn, w1, w2)
        elif kind == 1:
            p = _mlstm_params(mlstm_w_in[j], mlstm_w_if[j], mlstm_b_if[j], mlstm_head_norm[j], mlstm_w_o[j])
            h = _mlstm_layer(h, bsz, seq, g_mix, p, g_ffn, w1, w2)
        elif kind == 2:
            p = _gla_params(gla_w_in[j], gla_w_a1[j], gla_w_a2[j], gla_b_a[j], gla_head_norm[j], gla_w_o[j])
            h = _gla_layer(h, bsz, seq, g_mix, p, g_ffn, w1, w2)
        else:
            p = _conv_params(conv_w_pw1[j], conv_b_pw1[j], conv_w_dw[j], conv_b_dw[j], conv_ln_g[j],
                             conv_ln_b[j], conv_w_pw2[j], conv_b_pw2[j])
            h = _conv_layer(h, bsz, seq, g_mix, p, g_ffn, w1, w2)
    return h.reshape(bsz, seq, d)
```

```python
import functools
import math

import numpy as np
import jax
import jax.numpy as jnp
from jax import lax
from jax.experimental import pallas as pl
from jax.experimental.pallas import tpu as pltpu

F32 = jnp.float32
BF16 = jnp.bfloat16

D_MODEL = 1024
D_FF = 4 * D_MODEL
EPS = 1e-6
CHUNK = 64

MLA_HEADS = 16
MLA_NOPE = 64
MLA_ROPE = 32
MLA_QK = MLA_NOPE + MLA_ROPE
MLA_V = 64
MLA_Q_RANK = 384
MLA_KV_RANK = 256
ROPE_BASE = 10000.0
MLA_VROWS = MLA_V + 16

ML_HEADS = 4
ML_QK = D_MODEL // 8
ML_V = D_MODEL // 4
GATE_CAP = 15.0

GLA_HEADS = 4
GLA_K = D_MODEL // 8
GLA_V = D_MODEL // 4
GLA_GATE_RANK = 16
GLA_TAU = 16.0

CONV_WIDTH = 31

LANES = 128
SUBLANES = 8
VMEM_LIMIT = 56 * 1024 * 1024

TOK_TILE = 512
FF_TILE = 1024
ATT_TILE = 512
ATT_Q_TILE = 1024
REC_TILE = 256
CONV_HALO = 32
CONV_ROWS = 64
CONV_NORM_ROWS = 32
NEG_BIG = -1e30


def _cparams(*sem):
    return pltpu.CompilerParams(dimension_semantics=sem, vmem_limit_bytes=VMEM_LIMIT)


def _const_spec(shape):
    nd = len(shape)
    return pl.BlockSpec(shape, lambda *_: (0,) * nd, pipeline_mode=pl.Buffered(1))


def _dot(a, b):
    return jnp.dot(a, b, preferred_element_type=F32)


def _dot_nt(a, b):
    return lax.dot_general(a, b, (((1,), (1,)), ((), ())), preferred_element_type=F32)


def _dot_tn(a, b):
    return lax.dot_general(a, b, (((0,), (0,)), ((), ())), preferred_element_type=F32)


def _split2(a):
    hi = a.astype(BF16)
    lo = (a - hi.astype(F32)).astype(BF16)
    return hi, lo


def _split3(a):
    hi = a.astype(BF16)
    r = a - hi.astype(F32)
    mid = r.astype(BF16)
    lo = (r - mid.astype(F32)).astype(BF16)
    return hi, mid, lo


def _rms_rows(x, g):
    return x * lax.rsqrt(jnp.mean(x * x, axis=-1, keepdims=True) + EPS) * g


def _log_sigmoid(z):
    return jnp.minimum(z, 0.0) - jnp.log1p(jnp.exp(-jnp.abs(z)))


def _sigmoid(z):
    return 1.0 / (1.0 + jnp.exp(-z))


def _mlp_kernel(*refs, has_bias, a_transposed):
    if has_bias:
        x_ref, a_ref, wo_ref, bo_ref, g_ref, w1_ref, w2_ref, o_ref = refs
    else:
        x_ref, a_ref, wo_ref, g_ref, w1_ref, w2_ref, o_ref = refs
    if a_transposed:
        x1 = x_ref[...] + _dot_tn(a_ref[...], wo_ref[...])
    else:
        x1 = x_ref[...] + _dot(a_ref[...], wo_ref[...])
    if has_bias:
        x1 = x1 + bo_ref[...]
    hn = _rms_rows(x1, g_ref[...]).astype(BF16)
    acc = x1
    for c in range(D_FF // FF_TILE):
        h = _dot(hn, w1_ref[:, c * FF_TILE:(c + 1) * FF_TILE])
        h = jnp.maximum(h, 0.0)
        acc = acc + _dot((h * h).astype(BF16), w2_ref[c * FF_TILE:(c + 1) * FF_TILE, :])
    o_ref[...] = acc


def _mlp_call(x, a, w_o, b_o, g, w1, w2, a_transposed=False):
    t, d = x.shape
    din = w_o.shape[0]
    has_bias = b_o is not None
    row = lambda i: (i, 0)
    a_spec = pl.BlockSpec((din, TOK_TILE), lambda i: (0, i)) if a_transposed else pl.BlockSpec((TOK_TILE, din), row)
    in_specs = [pl.BlockSpec((TOK_TILE, d), row), a_spec, _const_spec((din, d))]
    args = [x, a, w_o]
    if has_bias:
        in_specs.append(_const_spec((1, d)))
        args.append(b_o)
    in_specs += [_const_spec((1, d)), _const_spec((d, D_FF)), _const_spec((D_FF, d))]
    args += [g, w1, w2]
    return pl.pallas_call(
        functools.partial(_mlp_kernel, has_bias=has_bias, a_transposed=a_transposed),
        grid=(t // TOK_TILE,),
        in_specs=in_specs,
        out_specs=pl.BlockSpec((TOK_TILE, d), row),
        out_shape=jax.ShapeDtypeStruct((t, d), F32),
        compiler_params=_cparams("parallel"),
        name="mlp",
    )(*args)


def _mla_proj_kernel(x_ref, g_ref, wdq_ref, qn_ref, wuqt_ref, wdkv_ref, kvn_ref, wuk_ref, wuvt_ref, hsum_ref,
                     vone_ref, gkn_ref, qc_ref, qsa_ref, qsb_ref, kc_ref, ksa_ref, ksb_ref,
                     qt_ref, k_ref, vt_ref):
    half = MLA_ROPE // 2
    hn = _rms_rows(x_ref[...], g_ref[...]).astype(BF16)
    cq = _rms_rows(_dot(hn, wdq_ref[...]), qn_ref[...]).astype(BF16)
    dkv = _dot(hn, wdkv_ref[...])
    ckv = _rms_rows(dkv[:, :MLA_KV_RANK], kvn_ref[...]).astype(BF16)

    vt_ref[0] = (_dot_nt(wuvt_ref[...], ckv) + vone_ref[...]).astype(BF16)

    qt = _dot_nt(wuqt_ref[...], cq)
    qc, qsa, qsb = qc_ref[...], qsa_ref[...], qsb_ref[...]
    for h in range(MLA_HEADS):
        t = qt[h * LANES:(h + 1) * LANES]
        r = lax.rsqrt(jnp.sum(t * t, axis=0, keepdims=True) * (1.0 / MLA_QK) + EPS)
        up = jnp.concatenate([t[half:], t[:half]], axis=0)
        dn = jnp.concatenate([t[LANES - half:], t[:LANES - half]], axis=0)
        qt_ref[h * LANES:(h + 1) * LANES, :] = ((t * qc + up * qsa + dn * qsb) * r).astype(BF16)

    kr = dkv[:, MLA_KV_RANK:]
    kn = _dot(ckv, wuk_ref[...])
    ssq = _dot((kn * kn).astype(BF16), hsum_ref[...]) + jnp.sum(kr * kr, axis=-1, keepdims=True)
    rk = lax.rsqrt(ssq * (1.0 / MLA_QK) + EPS)
    krr = (kr * kc_ref[...] + pltpu.roll(kr, LANES - half, 1) * ksa_ref[...] + pltpu.roll(kr, half, 1) * ksb_ref[...])
    gkn = gkn_ref[...]
    for h in range(MLA_HEADS):
        sl = slice(h * LANES, (h + 1) * LANES)
        k_ref[:, sl] = ((kn[:, sl] * gkn + krr) * rk[:, h:h + 1]).astype(BF16)


def _attn_kernel(qt_ref, k_ref, vt_ref, o_ref, s_ref):
    i = pl.program_id(2)
    tq, tk = ATT_Q_TILE, ATT_TILE
    kpq = tq // tk
    assert kpq == 2
    vrows = vt_ref.shape[1] // 2
    krow = lax.broadcasted_iota(jnp.int32, (tk, tq), 0)
    qcol = lax.broadcasted_iota(jnp.int32, (tk, tq), 1)
    diag_mask = (krow // CHUNK) <= (qcol // CHUNK)
    late = slice(tk, tq)

    def produce(h, kb, qcols=slice(None)):
        start = pl.multiple_of(kb * tk, tk)
        st = _dot(k_ref[pl.ds(start, tk), h * LANES:(h + 1) * LANES], qt_ref[h * LANES:(h + 1) * LANES, qcols])
        s_ref[h, :, qcols] = st
        return jnp.max(st, axis=0, keepdims=True)

    def consume(h, kb, m, acc, mblk, mask=None, qcols=slice(None)):
        st = s_ref[h, :, qcols]
        if mask is not None:
            st = jnp.where(mask, st, NEG_BIG)
            mblk = jnp.max(st, axis=0, keepdims=True)
        m_new = jnp.maximum(m, mblk)
        pt = jnp.exp2(st - m_new).astype(BF16)
        acc = jnp.exp2(m - m_new) * acc + _dot(vt_ref[kb, h * vrows:(h + 1) * vrows, :], pt)
        return m_new, acc

    def body(j, carry):
        m0, a0, m1, a1, mb0 = carry
        mb1 = produce(1, j)
        m0, a0 = consume(0, j, m0, a0, mb0)
        mb0 = produce(0, j + 1)
        m1, a1 = consume(1, j, m1, a1, mb1)
        return m0, a0, m1, a1, mb0

    m_init = jnp.full((1, tq), NEG_BIG, F32)
    a_init = jnp.zeros((vrows, tq), F32)
    first = kpq * i
    m0, a0, m1, a1, _ = lax.fori_loop(0, i, lambda t, c: body(kpq * t + 1, body(kpq * t, c)),
                                      (m_init, a_init, m_init, a_init, produce(0, 0)))
    produce(1, first)
    m0, a0 = consume(0, first, m0, a0, None, diag_mask)
    produce(0, first + 1, late)
    m1, a1 = consume(1, first, m1, a1, None, diag_mask)
    produce(1, first + 1, late)
    for h, (m, acc) in enumerate(((m0, a0), (m1, a1))):
        _, acc_l = consume(h, first + 1, m[:, late], acc[:, late], None, diag_mask[:, :tk], late)
        acc = jnp.concatenate([acc[:, :tk], acc_l], axis=1)
        o_ref[h * MLA_V:(h + 1) * MLA_V, :] = (acc[:MLA_V] / acc[MLA_V:MLA_V + 1]).astype(BF16)


def _mla_layer(x, bsz, seq, g_mix, p, g_ffn, w1, w2):
    t, d = x.shape
    assert TOK_TILE == ATT_TILE
    hp = MLA_HEADS * LANES
    hv = MLA_HEADS * MLA_VROWS
    nt = seq // TOK_TILE
    row = lambda i: (i, 0)
    ttab = lambda i: (i % nt, 0)
    ftab = lambda i: (0, i % nt)
    qt, k, vt = pl.pallas_call(
        _mla_proj_kernel,
        grid=(t // TOK_TILE,),
        in_specs=[pl.BlockSpec((TOK_TILE, d), row), _const_spec((1, d)),
                  _const_spec((d, MLA_Q_RANK)), _const_spec((1, MLA_Q_RANK)), _const_spec((hp, MLA_Q_RANK)),
                  _const_spec((d, MLA_KV_RANK + LANES)), _const_spec((1, MLA_KV_RANK)),
                  _const_spec((MLA_KV_RANK, hp)), _const_spec((hv, MLA_KV_RANK)), _const_spec((hp, LANES)),
                  _const_spec((hv, TOK_TILE)), _const_spec((1, LANES)),
                  pl.BlockSpec((LANES, TOK_TILE), ftab), pl.BlockSpec((LANES, TOK_TILE), ftab),
                  pl.BlockSpec((LANES, TOK_TILE), ftab),
                  pl.BlockSpec((TOK_TILE, LANES), ttab), pl.BlockSpec((TOK_TILE, LANES), ttab),
                  pl.BlockSpec((TOK_TILE, LANES), ttab)],
        out_specs=[pl.BlockSpec((hp, TOK_TILE), lambda i: (0, i)), pl.BlockSpec((TOK_TILE, hp), row),
                   pl.BlockSpec((1, hv, TOK_TILE), lambda i: (i, 0, 0))],
        out_shape=[jax.ShapeDtypeStruct((hp, t), BF16), jax.ShapeDtypeStruct((t, hp), BF16),
                   jax.ShapeDtypeStruct((t // TOK_TILE, hv, TOK_TILE), BF16)],
        compiler_params=_cparams("parallel"),
        name="mla_proj",
    )(x, g_mix, p["w_dq"], p["q_norm"], p["w_uq_t"], p["w_dkv"], p["kv_norm"], p["w_uk"], p["w_uv_t"], p["hsum"],
      p["vone"], p["gkn"], p["qc"], p["qsa"], p["qsb"], p["kc"], p["ksa"], p["ksb"])

    nq = seq // ATT_Q_TILE
    ot = pl.pallas_call(
        _attn_kernel,
        grid=(bsz, MLA_HEADS // 2, nq),
        in_specs=[pl.BlockSpec((2 * LANES, ATT_Q_TILE), lambda b, h, i: (h, b * nq + i)),
                  pl.BlockSpec((seq, 2 * LANES), lambda b, h, i: (b, h)),
                  pl.BlockSpec((seq // ATT_TILE, 2 * MLA_VROWS, ATT_TILE), lambda b, h, i: (b, h, 0))],
        out_specs=pl.BlockSpec((2 * MLA_V, ATT_Q_TILE), lambda b, h, i: (h, b * nq + i)),
        out_shape=jax.ShapeDtypeStruct((MLA_HEADS * MLA_V, t), BF16),
        scratch_shapes=[pltpu.VMEM((2, ATT_TILE, ATT_Q_TILE), F32)],
        compiler_params=_cparams("parallel", "parallel", "arbitrary"),
        name="mla_attn",
    )(qt, k, vt)
    return _mlp_call(x, ot, p["w_o"], None, g_ffn, w1, w2, a_transposed=True)


def _mla_params(seq, w_dq, q_norm, w_uq, w_dkv, kv_norm, w_ukv, q_gain, k_gain, w_o):
    pad_head = LANES - MLA_QK
    half = MLA_ROPE // 2
    w_uq_p = jnp.pad(w_uq.reshape(MLA_Q_RANK, MLA_HEADS, MLA_QK), ((0, 0), (0, 0), (0, pad_head)))
    w_ukv_r = w_ukv.reshape(MLA_KV_RANK, MLA_HEADS, MLA_NOPE + MLA_V)
    w_uk_p = jnp.pad(w_ukv_r[:, :, :MLA_NOPE], ((0, 0), (0, 0), (0, LANES - MLA_NOPE)))
    w_uv_p = jnp.pad(w_ukv_r[:, :, MLA_NOPE:], ((0, 0), (0, 0), (0, MLA_VROWS - MLA_V)))
    w_rope = jnp.pad(w_dkv[:, MLA_KV_RANK:], ((0, 0), (MLA_NOPE, pad_head)))
    w_dkv_p = jnp.concatenate([w_dkv[:, :MLA_KV_RANK], w_rope], axis=1)
    lane = np.arange(MLA_HEADS * LANES)
    hsum = (lane[:, None] // LANES == np.arange(LANES)[None, :]).astype(np.float32)
    vone = np.zeros((MLA_HEADS * MLA_VROWS, TOK_TILE), np.float32)
    vone[np.arange(MLA_HEADS) * MLA_VROWS + MLA_V] = 1.0
    gq = q_gain * ((MLA_QK ** -0.5) * math.log2(math.e))
    inv_freq = ROPE_BASE ** (-jnp.arange(half, dtype=F32) / half)
    ang = jnp.arange(seq).astype(F32)[:, None] * inv_freq[None, :]
    cos, sin = jnp.cos(ang), jnp.sin(ang)
    g1 = lambda g: g[MLA_NOPE:MLA_NOPE + half][None, :]
    g2 = lambda g: g[MLA_NOPE + half:MLA_QK][None, :]
    z = lambda n: jnp.zeros((seq, n), F32)

    def tables(g, nope_gain):
        nope = jnp.broadcast_to(g[None, :MLA_NOPE], (seq, MLA_NOPE)) if nope_gain else z(MLA_NOPE)
        c = jnp.concatenate([nope, g1(g) * cos, g2(g) * cos, z(pad_head)], axis=1)
        sa = jnp.concatenate([z(MLA_NOPE), -g2(g) * sin, z(half + pad_head)], axis=1)
        sb = jnp.concatenate([z(MLA_NOPE + half), g1(g) * sin, z(pad_head)], axis=1)
        return c, sa, sb

    qc, qsa, qsb = tables(gq, True)
    kc, ksa, ksb = tables(k_gain, False)
    return dict(w_dq=w_dq.astype(BF16), q_norm=q_norm.reshape(1, -1),
                w_uq_t=w_uq_p.reshape(MLA_Q_RANK, -1).T.astype(BF16), w_dkv=w_dkv_p.astype(BF16),
                kv_norm=kv_norm.reshape(1, -1), w_uk=w_uk_p.reshape(MLA_KV_RANK, -1).astype(BF16),
                w_uv_t=w_uv_p.reshape(MLA_KV_RANK, -1).T.astype(BF16),
                hsum=jnp.asarray(hsum, BF16), vone=jnp.asarray(vone, BF16),
                gkn=jnp.pad(k_gain[:MLA_NOPE], (0, LANES - MLA_NOPE)).reshape(1, LANES),
                qc=qc.T, qsa=qsa.T, qsb=qsb.T, kc=kc, ksa=ksa, ksb=ksb, w_o=w_o.astype(BF16))


def _mlstm_proj_kernel(x_ref, g_ref, win_ref, wc_hi_ref, wc_lo_ref, wr_hi_ref, wr_lo_ref, bc_ref, br_ref,
                       q_ref, k_ref, v_ref, o_ref, gc_ref, gr_ref):
    hn = _rms_rows(x_ref[...], g_ref[...])
    hi, lo = _split2(hn)
    y = _dot(hi, win_ref[...])
    hq = ML_HEADS * ML_QK
    q_ref[...] = y[:, :hq].astype(BF16)
    k_ref[...] = (y[:, hq:2 * hq] * (ML_QK ** -0.5)).astype(BF16)
    v_ref[...] = y[:, 2 * hq:2 * hq + ML_HEADS * ML_V].astype(BF16)
    o_ref[...] = y[:, 2 * hq + ML_HEADS * ML_V:].astype(BF16)
    gc_ref[...] = (_dot(hi, wc_hi_ref[...]) + _dot(lo, wc_hi_ref[...]) + _dot(hi, wc_lo_ref[...])) + bc_ref[...]
    gr_ref[...] = (_dot_nt(wr_hi_ref[...], hi) + _dot_nt(wr_hi_ref[...], lo) + _dot_nt(wr_lo_ref[...], hi)) + br_ref[...]


def _cap(g):
    return GATE_CAP * jnp.tanh(g * (1.0 / GATE_CAP))


def _mlstm_rec_kernel(q_ref, k_ref, v_ref, op_ref, gc_ref, gr_ref, hnorm_ref, tril_ref, triu_ref,
                      o_ref, c_ref, m_ref):
    L = REC_TILE

    @pl.when(pl.program_id(1) == 0)
    def _():
        c_ref[...] = jnp.zeros_like(c_ref)
        m_ref[...] = jnp.zeros_like(m_ref)

    gc = _cap(gc_ref[...])
    gr = _cap(gr_ref[...])
    lf_c = _log_sigmoid(gc)
    lf_r = _log_sigmoid(gr)
    c1, c2, c3 = _split3(lf_c)
    r1, r2, r3 = _split3(lf_r)
    tril, triu = tril_ref[...], triu_ref[...]
    bcum_c = _dot(tril, c1) + _dot(tril, c2) + _dot(tril, c3)
    bcum_r = _dot(r1, triu) + _dot(r2, triu) + _dot(r3, triu)
    row = lax.broadcasted_iota(jnp.int32, (L, L), 0)
    col = lax.broadcasted_iota(jnp.int32, (L, L), 1)
    causal = col <= row
    ones = jnp.ones((L, LANES), BF16)

    for h in range(ML_HEADS):
        fh = ML_HEADS + h
        bt = bcum_c[:, fh:fh + 1]
        bs = bcum_r[fh:fh + 1, :]
        li_r = gr[h:h + 1, :]
        li_c = gc[:, h:h + 1]
        m_prev = m_ref[h:h + 1, 0:1]
        q = q_ref[:, h * ML_QK:(h + 1) * ML_QK]
        k = k_ref[:, h * ML_QK:(h + 1) * ML_QK]
        vaug = jnp.concatenate([v_ref[:, h * ML_V:(h + 1) * ML_V], ones], axis=1)
        c_st = c_ref[h]

        dmat = jnp.where(causal, bt - bs + li_r, NEG_BIG)
        inter = bt + m_prev
        m_t = jnp.maximum(inter, jnp.max(dmat, axis=-1, keepdims=True))
        w_intra = jnp.exp(dmat - m_t)
        w_inter = jnp.exp(inter - m_t)
        qk = (_dot_nt(q, k) * w_intra).astype(BF16)
        nd = _dot(qk, vaug) + w_inter * _dot(q, c_st.astype(BF16))
        den = jnp.maximum(jnp.abs(nd[:, ML_V:]), jnp.exp(-m_t))
        hc = nd[:, :ML_V] / jnp.concatenate([den, den], axis=1)

        b_last = bs[:, L - 1:L]
        m_new = jnp.maximum(b_last + m_prev, jnp.max(b_last - bs + li_r, axis=-1, keepdims=True))
        ws = jnp.exp(b_last - bt + li_c - m_new)
        wc = jnp.exp(b_last + m_prev - m_new)
        kw = (k.astype(F32) * ws).astype(BF16)
        c_ref[h] = wc * c_st + _dot_tn(kw, vaug)
        m_ref[h:h + 1, :] = jnp.broadcast_to(m_new, (1, LANES))

        sl = slice(h * ML_V, (h + 1) * ML_V)
        hs = _rms_rows(hc, hnorm_ref[:, sl])
        o_ref[:, sl] = (_sigmoid(op_ref[:, sl].astype(F32)) * hs).astype(BF16)


def _tri_consts(n):
    r = np.arange(n)
    tril = (r[None, :] <= r[:, None]).astype(np.float32)
    return jnp.asarray(tril, BF16), jnp.asarray(tril.T, BF16)


def _mlstm_layer(x, bsz, seq, g_mix, p, g_ffn, w1, w2):
    t, d = x.shape
    hq, hv = ML_HEADS * ML_QK, ML_HEADS * ML_V
    row = lambda i: (i, 0)
    q, k, v, opre, gcol, grow = pl.pallas_call(
        _mlstm_proj_kernel,
        grid=(t // TOK_TILE,),
        in_specs=[pl.BlockSpec((TOK_TILE, d), row), _const_spec((1, d)), _const_spec((d, 2 * hq + 2 * hv)),
                  _const_spec((d, LANES)), _const_spec((d, LANES)),
                  _const_spec((SUBLANES, d)), _const_spec((SUBLANES, d)),
                  _const_spec((1, LANES)), _const_spec((SUBLANES, 1))],
        out_specs=[pl.BlockSpec((TOK_TILE, hq), row), pl.BlockSpec((TOK_TILE, hq), row),
                   pl.BlockSpec((TOK_TILE, hv), row), pl.BlockSpec((TOK_TILE, hv), row),
                   pl.BlockSpec((TOK_TILE, LANES), row), pl.BlockSpec((SUBLANES, TOK_TILE), lambda i: (0, i))],
        out_shape=[jax.ShapeDtypeStruct((t, hq), BF16), jax.ShapeDtypeStruct((t, hq), BF16),
                   jax.ShapeDtypeStruct((t, hv), BF16), jax.ShapeDtypeStruct((t, hv), BF16),
                   jax.ShapeDtypeStruct((t, LANES), F32), jax.ShapeDtypeStruct((SUBLANES, t), F32)],
        compiler_params=_cparams("parallel"),
        name="mlstm_proj",
    )(x, g_mix, p["w_in"], p["wc_hi"], p["wc_lo"], p["wr_hi"], p["wr_lo"], p["b_col"], p["b_row"])

    nc = seq // REC_TILE
    tril, triu = _tri_consts(REC_TILE)
    blk = lambda b, c: (b * nc + c, 0)
    a = pl.pallas_call(
        _mlstm_rec_kernel,
        grid=(bsz, nc),
        in_specs=[pl.BlockSpec((REC_TILE, hq), blk), pl.BlockSpec((REC_TILE, hq), blk),
                  pl.BlockSpec((REC_TILE, hv), blk), pl.BlockSpec((REC_TILE, hv), blk),
                  pl.BlockSpec((REC_TILE, LANES), blk), pl.BlockSpec((SUBLANES, REC_TILE), lambda b, c: (0, b * nc + c)),
                  _const_spec((1, hv)), _const_spec((REC_TILE, REC_TILE)), _const_spec((REC_TILE, REC_TILE))],
        out_specs=pl.BlockSpec((REC_TILE, hv), blk),
        out_shape=jax.ShapeDtypeStruct((t, hv), BF16),
        scratch_shapes=[pltpu.VMEM((ML_HEADS, ML_QK, ML_V + LANES), F32), pltpu.VMEM((SUBLANES, LANES), F32)],
        compiler_params=_cparams("parallel", "arbitrary"),
        name="mlstm_rec",
    )(q, k, v, opre, gcol, grow, p["head_norm"], tril, triu)
    return _mlp_call(x, a, p["w_o"], None, g_ffn, w1, w2)


def _mlstm_params(w_in, w_if, b_if, head_norm, w_o):
    ng = 2 * ML_HEADS
    w_col = jnp.pad(w_if, ((0, 0), (0, LANES - ng)))
    wc_hi = w_col.astype(BF16)
    wc_lo = (w_col - wc_hi.astype(F32)).astype(BF16)
    w_row = w_if.T
    wr_hi = w_row.astype(BF16)
    wr_lo = (w_row - wr_hi.astype(F32)).astype(BF16)
    return dict(w_in=w_in.astype(BF16), wc_hi=wc_hi, wc_lo=wc_lo, wr_hi=wr_hi, wr_lo=wr_lo,
                b_col=jnp.pad(b_if, (0, LANES - ng)).reshape(1, LANES), b_row=b_if.reshape(ng, 1),
                head_norm=head_norm.reshape(1, -1), w_o=w_o.astype(BF16))


def _gla_proj_kernel(x_ref, g_ref, win_ref, wa1_ref, wa2_ref, ba_ref, q_ref, k_ref, v_ref, r_ref, la_ref):
    hn = _rms_rows(x_ref[...], g_ref[...]).astype(BF16)
    y = _dot(hn, win_ref[...])
    hk, hv = GLA_HEADS * GLA_K, GLA_HEADS * GLA_V
    q_ref[...] = (y[:, :hk] * (GLA_K ** -0.5)).astype(BF16)
    k_ref[...] = y[:, hk:2 * hk].astype(BF16)
    v_ref[...] = y[:, 2 * hk:2 * hk + hv].astype(BF16)
    r_ref[...] = y[:, 2 * hk + hv:].astype(BF16)
    z = _dot(_dot(hn, wa1_ref[...]).astype(BF16), wa2_ref[...]) + ba_ref[...]
    la_ref[...] = _log_sigmoid(z) * (1.0 / GLA_TAU)


def _gla_levels():
    return [2 ** j for j in range(1, int(math.log2(REC_TILE)) + 1)]


def _gla_consts():
    n = REC_TILE
    t = np.arange(n)[:, None]
    u = np.arange(n)[None, :]
    mats = []
    for p in _gla_levels():
        first_upper = (t // p) * p + p // 2
        upper = (t % p) >= p // 2
        m_up = (u > first_upper) & (u <= t)
        m_lo = (u > t) & (u <= first_upper)
        mats.append(np.where(upper, m_up, m_lo))
    mats.append(u <= t)
    mats.append(u > t)
    return jnp.asarray(np.stack(mats).astype(np.float32), BF16)


def _gla_rec_kernel(q_ref, k_ref, v_ref, r_ref, la_ref, hnorm_ref, w_ref, o_ref, c_ref):
    L = REC_TILE
    levels = _gla_levels()
    nl = len(levels)

    @pl.when(pl.program_id(1) == 0)
    def _():
        c_ref[...] = jnp.zeros_like(c_ref)

    la_hi, la_lo = _split2(la_ref[...])

    def decay(j):
        w = w_ref[j]
        return jnp.exp(_dot(w, la_hi) + _dot(w, la_lo))

    row = lax.broadcasted_iota(jnp.int32, (L, L), 0)
    col = lax.broadcasted_iota(jnp.int32, (L, L), 1)
    rowk = lax.broadcasted_iota(jnp.int32, (L, GLA_K), 0)
    e_cum = decay(nl)
    e_rev = decay(nl + 1)
    e_lvl = [decay(j) for j in range(nl)]

    for h in range(GLA_HEADS):
        ks = slice(h * GLA_K, (h + 1) * GLA_K)
        vs = slice(h * GLA_V, (h + 1) * GLA_V)
        q = q_ref[:, ks].astype(F32)
        k = k_ref[:, ks].astype(F32)
        v = v_ref[:, vs]
        att = jnp.where(row == col, _dot_nt(q_ref[:, ks], k_ref[:, ks]), 0.0)
        for j, p in enumerate(levels):
            upper = (rowk & (p - 1)) >= (p // 2)
            e = e_lvl[j][:, ks]
            qf = jnp.where(upper, q * e, 0.0).astype(BF16)
            kf = jnp.where(upper, 0.0, k * e).astype(BF16)
            a = _dot_nt(qf, kf)
            if p < L:
                shift = int(math.log2(p))
                a = jnp.where((row >> shift) == (col >> shift), a, 0.0)
            att = att + a
        c_st = c_ref[h]
        o = _dot(att.astype(BF16), v) + _dot_nt((q * e_cum[:, ks]).astype(BF16), c_st.astype(BF16))
        c_ref[h] = e_cum[L - 1:L, ks] * c_st + _dot_tn(v, (k * e_rev[:, ks]).astype(BF16))
        o = _rms_rows(o, hnorm_ref[:, vs])
        rr = r_ref[:, vs].astype(F32)
        o_ref[:, vs] = (o * rr * _sigmoid(rr)).astype(BF16)


def _gla_layer(x, bsz, seq, g_mix, p, g_ffn, w1, w2):
    t, d = x.shape
    hk, hv = GLA_HEADS * GLA_K, GLA_HEADS * GLA_V
    row = lambda i: (i, 0)
    q, k, v, r, la = pl.pallas_call(
        _gla_proj_kernel,
        grid=(t // TOK_TILE,),
        in_specs=[pl.BlockSpec((TOK_TILE, d), row), _const_spec((1, d)), _const_spec((d, 2 * hk + 2 * hv)),
                  _const_spec((d, LANES)), _const_spec((LANES, hk)), _const_spec((1, hk))],
        out_specs=[pl.BlockSpec((TOK_TILE, hk), row), pl.BlockSpec((TOK_TILE, hk), row),
                   pl.BlockSpec((TOK_TILE, hv), row), pl.BlockSpec((TOK_TILE, hv), row),
                   pl.BlockSpec((TOK_TILE, hk), row)],
        out_shape=[jax.ShapeDtypeStruct((t, hk), BF16), jax.ShapeDtypeStruct((t, hk), BF16),
                   jax.ShapeDtypeStruct((t, hv), BF16), jax.ShapeDtypeStruct((t, hv), BF16),
                   jax.ShapeDtypeStruct((t, hk), F32)],
        compiler_params=_cparams("parallel"),
        name="gla_proj",
    )(x, g_mix, p["w_in"], p["w_a1"], p["w_a2"], p["b_a"])

    nc = seq // REC_TILE
    wmats = _gla_consts()
    blk = lambda b, c: (b * nc + c, 0)
    a = pl.pallas_call(
        _gla_rec_kernel,
        grid=(bsz, nc),
        in_specs=[pl.BlockSpec((REC_TILE, hk), blk), pl.BlockSpec((REC_TILE, hk), blk),
                  pl.BlockSpec((REC_TILE, hv), blk), pl.BlockSpec((REC_TILE, hv), blk),
                  pl.BlockSpec((REC_TILE, hk), blk), _const_spec((1, hv)), _const_spec(wmats.shape)],
        out_specs=pl.BlockSpec((REC_TILE, hv), blk),
        out_shape=jax.ShapeDtypeStruct((t, hv), BF16),
        scratch_shapes=[pltpu.VMEM((GLA_HEADS, GLA_V, GLA_K), F32)],
        compiler_params=_cparams("parallel", "arbitrary"),
        name="gla_rec",
    )(q, k, v, r, la, p["head_norm"], wmats)
    return _mlp_call(x, a, p["w_o"], None, g_ffn, w1, w2)


def _gla_params(w_in, w_a1, w_a2, b_a, head_norm, w_o):
    return dict(w_in=w_in.astype(BF16),
                w_a1=jnp.pad(w_a1, ((0, 0), (0, LANES - GLA_GATE_RANK))).astype(BF16),
                w_a2=jnp.pad(w_a2, ((0, LANES - GLA_GATE_RANK), (0, 0))).astype(BF16),
                b_a=b_a.reshape(1, -1), head_norm=head_norm.reshape(1, -1), w_o=w_o.astype(BF16))


def _conv_kernel(x_ref, g_ref, w1_ref, b1_ref, wdw_ref, lng_ref, lnb_ref, o_ref, u_ref, c_ref):
    tm = TOK_TILE
    d = D_MODEL

    nbuf = CONV_HALO + tm

    @pl.when(pl.program_id(1) == 0)
    def _():
        u_ref[0, :, 0:CONV_HALO, :] = jnp.zeros((d // LANES, CONV_HALO, LANES), F32)

    @pl.when(pl.program_id(1) != 0)
    def _():
        u_ref[0, :, 0:CONV_HALO, :] = u_ref[0, :, tm:tm + CONV_HALO, :]

    hn = _rms_rows(x_ref[...], g_ref[...]).astype(BF16)
    y = _dot(hn, w1_ref[...]) + b1_ref[...]
    u = y[:, :d] * _sigmoid(y[:, d:])
    lead = CONV_HALO - (CONV_WIDTH - 1)
    groups = CONV_ROWS // SUBLANES

    for cb in range(d // LANES):
        cols = slice(cb * LANES, (cb + 1) * LANES)
        u_ref[0, cb, CONV_HALO:, :] = u[:, cols]
        full = u_ref[0, cb]
        for s in range(1, SUBLANES):
            u_ref[s, cb] = pltpu.roll(full, nbuf - s, 0)

        def taps(c, carry):
            r0 = pl.multiple_of(c * CONV_ROWS, CONV_ROWS)
            acc = jnp.broadcast_to(wdw_ref[CONV_WIDTH, :, cols][None], (groups, SUBLANES, LANES))
            for j in range(CONV_WIDTH):
                off = lead + j
                start = pl.multiple_of(r0 + (off // SUBLANES) * SUBLANES, SUBLANES)
                uj = u_ref[off % SUBLANES, cb, pl.ds(start, CONV_ROWS), :].reshape(groups, SUBLANES, LANES)
                acc = acc + uj * wdw_ref[j, :, cols][None]
            c_ref[pl.ds(r0, CONV_ROWS), cols] = acc.reshape(CONV_ROWS, LANES)
            return carry

        lax.fori_loop(0, tm // CONV_ROWS, taps, 0)

    def norm(c, carry):
        r0 = pl.multiple_of(c * CONV_NORM_ROWS, CONV_NORM_ROWS)
        acc = c_ref[pl.ds(r0, CONV_NORM_ROWS), :]
        mu = jnp.mean(acc, axis=-1, keepdims=True)
        cen = acc - mu
        var = jnp.mean(cen * cen, axis=-1, keepdims=True)
        z = cen * lax.rsqrt(var + EPS) * lng_ref[...] + lnb_ref[...]
        o_ref[pl.ds(r0, CONV_NORM_ROWS), :] = (z * _sigmoid(z)).astype(BF16)
        return carry

    lax.fori_loop(0, tm // CONV_NORM_ROWS, norm, 0, unroll=4)


def _conv_layer(x, bsz, seq, g_mix, p, g_ffn, w1, w2):
    t, d = x.shape
    nt = seq // TOK_TILE
    blk = lambda b, i: (b * nt + i, 0)
    c = pl.pallas_call(
        _conv_kernel,
        grid=(bsz, nt),
        in_specs=[pl.BlockSpec((TOK_TILE, d), blk), _const_spec((1, d)), _const_spec((d, 2 * d)),
                  _const_spec((1, 2 * d)), _const_spec((CONV_WIDTH + 1, SUBLANES, d)),
                  _const_spec((1, d)), _const_spec((1, d))],
        out_specs=pl.BlockSpec((TOK_TILE, d), blk),
        out_shape=jax.ShapeDtypeStruct((t, d), BF16),
        scratch_shapes=[pltpu.VMEM((SUBLANES, d // LANES, CONV_HALO + TOK_TILE, LANES), F32),
                        pltpu.VMEM((TOK_TILE, d), F32)],
        compiler_params=_cparams("parallel", "arbitrary"),
        name="conv",
    )(x, g_mix, p["w_pw1"], p["b_pw1"], p["w_dw"], p["ln_g"], p["ln_b"])
    return _mlp_call(x, c, p["w_pw2"], p["b_pw2"], g_ffn, w1, w2)


def _conv_params(w_pw1, b_pw1, w_dw, b_dw, ln_g, ln_b, w_pw2, b_pw2):
    return dict(w_pw1=w_pw1.astype(BF16), b_pw1=b_pw1.reshape(1, -1),
                w_dw=jnp.broadcast_to(jnp.concatenate([w_dw, b_dw[None]], axis=0)[:, None, :],
                                      (CONV_WIDTH + 1, SUBLANES, w_dw.shape[1])),
                ln_g=ln_g.reshape(1, -1), ln_b=ln_b.reshape(1, -1),
                w_pw2=w_pw2.astype(BF16), b_pw2=b_pw2.reshape(1, -1))


def kernel(x, norm_mix, norm_ffn, mla_w_dq, mla_q_norm, mla_w_uq, mla_w_dkv, mla_kv_norm, mla_w_ukv, mla_q_gain, mla_k_gain, mla_w_o, mlstm_w_in, mlstm_w_if, mlstm_b_if, mlstm_head_norm, mlstm_w_o, gla_w_in, gla_w_a1, gla_w_a2, gla_b_a, gla_head_norm, gla_w_o, conv_w_pw1, conv_b_pw1, conv_w_dw, conv_b_dw, conv_ln_g, conv_ln_b, conv_w_pw2, conv_b_pw2, ffn_w1, ffn_w2):
    bsz, seq, d = x.shape
    depth = norm_mix.shape[0]
    assert d == D_MODEL and seq % ATT_Q_TILE == 0 and seq % TOK_TILE == 0 and seq % REC_TILE == 0
    h = x.reshape(bsz * seq, d)
    for i in range(depth):
        kind, j = i % 4, i // 4
        g_mix = norm_mix[i].reshape(1, d)
        g_ffn = norm_ffn[i].reshape(1, d)
        w1 = ffn_w1[i].astype(BF16)
        w2 = ffn_w2[i].astype(BF16)
        if kind == 0:
            p = _mla_params(seq, mla_w_dq[j], mla_q_norm[j], mla_w_uq[j], mla_w_dkv[j], mla_kv_norm[j],
                            mla_w_ukv[j], mla_q_gain[j], mla_k_gain[j], mla_w_o[j])
            h = _mla_layer(h, bsz, seq, g_mix, p, g_ffn, w1, w2)
        elif kind == 1:
            p = _mlstm_params(mlstm_w_in[j], mlstm_w_if[j], mlstm_b_if[j], mlstm_head_norm[j], mlstm_w_o[j])
            h = _mlstm_layer(h, bsz, seq, g_mix, p, g_ffn, w1, w2)
        elif kind == 2:
            p = _gla_params(gla_w_in[j], gla_w_a1[j], gla_w_a2[j], gla_b_a[j], gla_head_norm[j], gla_w_o[j])
            h = _gla_layer(h, bsz, seq, g_mix, p, g_ffn, w1, w2)
        else:
            p = _conv_params(conv_w_pw1[j], conv_b_pw1[j], conv_w_dw[j], conv_b_dw[j], conv_ln_g[j],
                             conv_ln_b[j], conv_w_pw2[j], conv_b_pw2[j])
            h = _conv_layer(h, bsz, seq, g_mix, p, g_ffn, w1, w2)
    return h.reshape(bsz, seq, d)
```

```python
import functools
import math

import numpy as np
import jax
import jax.numpy as jnp
from jax import lax
from jax.experimental import pallas as pl
from jax.experimental.pallas import tpu as pltpu

F32 = jnp.float32
BF16 = jnp.bfloat16

D_MODEL = 1024
D_FF = 4 * D_MODEL
EPS = 1e-6
CHUNK = 64

MLA_HEADS = 16
MLA_NOPE = 64
MLA_ROPE = 32
MLA_QK = MLA_NOPE + MLA_ROPE
MLA_V = 64
MLA_Q_RANK = 384
MLA_KV_RANK = 256
ROPE_BASE = 10000.0
MLA_VROWS = MLA_V + 16

ML_HEADS = 4
ML_QK = D_MODEL // 8
ML_V = D_MODEL // 4
GATE_CAP = 15.0

GLA_HEADS = 4
GLA_K = D_MODEL // 8
GLA_V = D_MODEL // 4
GLA_GATE_RANK = 16
GLA_TAU = 16.0

CONV_WIDTH = 31

LANES = 128
SUBLANES = 8
VMEM_LIMIT = 56 * 1024 * 1024

TOK_TILE = 512
FF_TILE = 1024
ATT_TILE = 512
ATT_Q_TILE = 1024
ATT_HEADS = 4
REC_TILE = 256
CONV_HALO = 32
CONV_ROWS = 64
CONV_NORM_ROWS = 32
NEG_BIG = -1e30


def _cparams(*sem):
    return pltpu.CompilerParams(dimension_semantics=sem, vmem_limit_bytes=VMEM_LIMIT)


def _const_spec(shape):
    nd = len(shape)
    return pl.BlockSpec(shape, lambda *_: (0,) * nd, pipeline_mode=pl.Buffered(1))


def _dot(a, b):
    return jnp.dot(a, b, preferred_element_type=F32)


def _dot_nt(a, b):
    return lax.dot_general(a, b, (((1,), (1,)), ((), ())), preferred_element_type=F32)


def _dot_tn(a, b):
    return lax.dot_general(a, b, (((0,), (0,)), ((), ())), preferred_element_type=F32)


def _split2(a):
    hi = a.astype(BF16)
    lo = (a - hi.astype(F32)).astype(BF16)
    return hi, lo


def _split3(a):
    hi = a.astype(BF16)
    r = a - hi.astype(F32)
    mid = r.astype(BF16)
    lo = (r - mid.astype(F32)).astype(BF16)
    return hi, mid, lo


def _rms_rows(x, g):
    return x * lax.rsqrt(jnp.mean(x * x, axis=-1, keepdims=True) + EPS) * g


def _log_sigmoid(z):
    return jnp.minimum(z, 0.0) - jnp.log1p(jnp.exp(-jnp.abs(z)))


def _sigmoid(z):
    return 1.0 / (1.0 + jnp.exp(-z))


def _mlp_kernel(*refs, has_bias, a_transposed):
    if has_bias:
        x_ref, a_ref, wo_ref, bo_ref, g_ref, w1_ref, w2_ref, o_ref = refs
    else:
        x_ref, a_ref, wo_ref, g_ref, w1_ref, w2_ref, o_ref = refs
    if a_transposed:
        x1 = x_ref[...] + _dot_tn(a_ref[...], wo_ref[...])
    else:
        x1 = x_ref[...] + _dot(a_ref[...], wo_ref[...])
    if has_bias:
        x1 = x1 + bo_ref[...]
    hn = _rms_rows(x1, g_ref[...]).astype(BF16)
    acc = x1
    for c in range(D_FF // FF_TILE):
        h = _dot(hn, w1_ref[:, c * FF_TILE:(c + 1) * FF_TILE])
        h = jnp.maximum(h, 0.0)
        acc = acc + _dot((h * h).astype(BF16), w2_ref[c * FF_TILE:(c + 1) * FF_TILE, :])
    o_ref[...] = acc


def _mlp_call(x, a, w_o, b_o, g, w1, w2, a_transposed=False):
    t, d = x.shape
    din = w_o.shape[0]
    has_bias = b_o is not None
    row = lambda i: (i, 0)
    a_spec = pl.BlockSpec((din, TOK_TILE), lambda i: (0, i)) if a_transposed else pl.BlockSpec((TOK_TILE, din), row)
    in_specs = [pl.BlockSpec((TOK_TILE, d), row), a_spec, _const_spec((din, d))]
    args = [x, a, w_o]
    if has_bias:
        in_specs.append(_const_spec((1, d)))
        args.append(b_o)
    in_specs += [_const_spec((1, d)), _const_spec((d, D_FF)), _const_spec((D_FF, d))]
    args += [g, w1, w2]
    return pl.pallas_call(
        functools.partial(_mlp_kernel, has_bias=has_bias, a_transposed=a_transposed),
        grid=(t // TOK_TILE,),
        in_specs=in_specs,
        out_specs=pl.BlockSpec((TOK_TILE, d), row),
        out_shape=jax.ShapeDtypeStruct((t, d), F32),
        compiler_params=_cparams("parallel"),
        name="mlp",
    )(*args)


def _mla_proj_kernel(x_ref, g_ref, wdq_ref, qn_ref, wuqt_ref, wdkv_ref, kvn_ref, wukt_ref, wuvt_ref,
                     vone_ref, qc_ref, qsa_ref, qsb_ref, kc_ref, ksa_ref, ksb_ref,
                     qt_ref, k_ref, vt_ref):
    half = MLA_ROPE // 2
    hn = _rms_rows(x_ref[...], g_ref[...]).astype(BF16)
    cq = _rms_rows(_dot(hn, wdq_ref[...]), qn_ref[...]).astype(BF16)
    dkv = _dot(hn, wdkv_ref[...])
    ckv = _rms_rows(dkv[:, :MLA_KV_RANK], kvn_ref[...]).astype(BF16)

    vt_ref[0] = (_dot_nt(wuvt_ref[...], ckv) + vone_ref[...]).astype(BF16)

    n0, a0, b0, e0 = 0, MLA_NOPE, MLA_NOPE + half, MLA_QK

    def ssq(t):
        return jnp.sum(t * t, axis=0, keepdims=True)

    def rope(a, b, c, sa, sb):
        return a * c[a0:b0] + b * sa[a0:b0], b * c[b0:e0] + a * sb[b0:e0]

    def inv_rms(s):
        return lax.rsqrt(s * (1.0 / MLA_QK) + EPS)

    tm = x_ref.shape[0]
    pad_rows = jnp.zeros((LANES - MLA_QK, tm), F32)

    qt = _dot_nt(wuqt_ref[...], cq)
    qc, qsa, qsb = qc_ref[...], qsa_ref[...], qsb_ref[...]
    for h in range(MLA_HEADS):
        t = qt[h * MLA_QK:(h + 1) * MLA_QK]
        n, a, b = t[n0:a0], t[a0:b0], t[b0:e0]
        r = inv_rms(ssq(n) + ssq(a) + ssq(b))
        ra, rb = rope(a, b, qc, qsa, qsb)
        qt_ref[h * LANES:h * LANES + MLA_QK, :] = jnp.concatenate(
            [n * qc[n0:a0] * r, ra * r, rb * r], axis=0).astype(BF16)
        qt_ref[h * LANES + MLA_QK:(h + 1) * LANES, :] = pad_rows.astype(BF16)

    krt = dkv[:, MLA_KV_RANK:].T
    kt = _dot_nt(wukt_ref[...], ckv)
    kc, ksa, ksb = kc_ref[...], ksa_ref[...], ksb_ref[...]
    ka, kb = krt[a0:b0], krt[b0:e0]
    ssq_rope = ssq(ka) + ssq(kb)
    ra, rb = rope(ka, kb, kc, ksa, ksb)
    for h in range(MLA_HEADS):
        n = kt[h * MLA_NOPE:(h + 1) * MLA_NOPE]
        r = inv_rms(ssq(n) + ssq_rope)
        kh = jnp.concatenate([n * kc[n0:a0] * r, ra * r, rb * r, pad_rows], axis=0)
        k_ref[:, h * LANES:(h + 1) * LANES] = kh.T.astype(BF16)


def _attn_kernel(qt_ref, k_ref, vt_ref, o_ref, s_ref):
    i = pl.program_id(2)
    tq, tk = ATT_Q_TILE, ATT_TILE
    nh = ATT_HEADS
    kpq = tq // tk
    assert kpq == 2
    vrows = vt_ref.shape[1] // nh
    krow = lax.broadcasted_iota(jnp.int32, (tk, tq), 0)
    qcol = lax.broadcasted_iota(jnp.int32, (tk, tq), 1)
    diag_mask = (krow // CHUNK) <= (qcol // CHUNK)
    late = slice(tk, tq)

    def produce(h, kb, qcols=slice(None)):
        start = pl.multiple_of(kb * tk, tk)
        st = _dot(k_ref[pl.ds(start, tk), h * LANES:(h + 1) * LANES], qt_ref[h * LANES:(h + 1) * LANES, qcols])
        s_ref[h, :, qcols] = st
        return jnp.max(st, axis=0, keepdims=True)

    def consume(h, kb, m, acc, mblk, mask=None, qcols=slice(None)):
        st = s_ref[h, :, qcols]
        if mask is not None:
            st = jnp.where(mask, st, NEG_BIG)
            mblk = jnp.max(st, axis=0, keepdims=True)
        m_new = jnp.maximum(m, mblk)
        pt = jnp.exp2(st - m_new).astype(BF16)
        acc = jnp.exp2(m - m_new) * acc + _dot(vt_ref[kb, h * vrows:(h + 1) * vrows, :], pt)
        return m_new, acc

    def body(j, carry):
        ms, accs, mb = carry
        ms, accs = list(ms), list(accs)
        for h in range(nh):
            mb_next = produce((h + 1) % nh, j + (h + 1) // nh)
            ms[h], accs[h] = consume(h, j, ms[h], accs[h], mb)
            mb = mb_next
        return tuple(ms), tuple(accs), mb

    m_init = jnp.full((1, tq), NEG_BIG, F32)
    a_init = jnp.zeros((vrows, tq), F32)
    first = kpq * i
    ms, accs, _ = lax.fori_loop(0, i, lambda t, c: body(kpq * t + 1, body(kpq * t, c)),
                                ((m_init,) * nh, (a_init,) * nh, produce(0, 0)))
    ms, accs = list(ms), list(accs)
    for h in range(nh):
        if h + 1 < nh:
            produce(h + 1, first)
        else:
            produce(0, first + 1, late)
        ms[h], accs[h] = consume(h, first, ms[h], accs[h], None, diag_mask)
    for h in range(nh):
        if h + 1 < nh:
            produce(h + 1, first + 1, late)
        _, acc_l = consume(h, first + 1, ms[h][:, late], accs[h][:, late], None, diag_mask[:, :tk], late)
        acc = jnp.concatenate([accs[h][:, :tk], acc_l], axis=1)
        o_ref[h * MLA_V:(h + 1) * MLA_V, :] = (acc[:MLA_V] / acc[MLA_V:MLA_V + 1]).astype(BF16)


def _mla_layer(x, bsz, seq, g_mix, p, g_ffn, w1, w2):
    t, d = x.shape
    assert TOK_TILE == ATT_TILE
    hp = MLA_HEADS * LANES
    hv = MLA_HEADS * MLA_VROWS
    nt = seq // TOK_TILE
    row = lambda i: (i, 0)
    ttab = lambda i: (i % nt, 0)
    ftab = lambda i: (0, i % nt)
    qt, k, vt = pl.pallas_call(
        _mla_proj_kernel,
        grid=(t // TOK_TILE,),
        in_specs=[pl.BlockSpec((TOK_TILE, d), row), _const_spec((1, d)),
                  _const_spec((d, MLA_Q_RANK)), _const_spec((1, MLA_Q_RANK)),
                  _const_spec((MLA_HEADS * MLA_QK, MLA_Q_RANK)),
                  _const_spec((d, MLA_KV_RANK + LANES)), _const_spec((1, MLA_KV_RANK)),
                  _const_spec((MLA_HEADS * MLA_NOPE, MLA_KV_RANK)), _const_spec((hv, MLA_KV_RANK)),
                  _const_spec((hv, TOK_TILE))] + [pl.BlockSpec((LANES, TOK_TILE), ftab)] * 6,
        out_specs=[pl.BlockSpec((hp, TOK_TILE), lambda i: (0, i)), pl.BlockSpec((TOK_TILE, hp), row),
                   pl.BlockSpec((1, hv, TOK_TILE), lambda i: (i, 0, 0))],
        out_shape=[jax.ShapeDtypeStruct((hp, t), BF16), jax.ShapeDtypeStruct((t, hp), BF16),
                   jax.ShapeDtypeStruct((t // TOK_TILE, hv, TOK_TILE), BF16)],
        compiler_params=_cparams("parallel"),
        name="mla_proj",
    )(x, g_mix, p["w_dq"], p["q_norm"], p["w_uq_t"], p["w_dkv"], p["kv_norm"], p["w_uk_t"], p["w_uv_t"],
      p["vone"], p["qc"], p["qsa"], p["qsb"], p["kc"], p["ksa"], p["ksb"])

    nq = seq // ATT_Q_TILE
    ot = pl.pallas_call(
        _attn_kernel,
        grid=(bsz, MLA_HEADS // ATT_HEADS, nq),
        in_specs=[pl.BlockSpec((ATT_HEADS * LANES, ATT_Q_TILE), lambda b, h, i: (h, b * nq + i)),
                  pl.BlockSpec((seq, ATT_HEADS * LANES), lambda b, h, i: (b, h)),
                  pl.BlockSpec((seq // ATT_TILE, ATT_HEADS * MLA_VROWS, ATT_TILE), lambda b, h, i: (b, h, 0))],
        out_specs=pl.BlockSpec((ATT_HEADS * MLA_V, ATT_Q_TILE), lambda b, h, i: (h, b * nq + i)),
        out_shape=jax.ShapeDtypeStruct((MLA_HEADS * MLA_V, t), BF16),
        scratch_shapes=[pltpu.VMEM((ATT_HEADS, ATT_TILE, ATT_Q_TILE), F32)],
        compiler_params=_cparams("parallel", "parallel", "arbitrary"),
        name="mla_attn",
    )(qt, k, vt)
    return _mlp_call(x, ot, p["w_o"], None, g_ffn, w1, w2, a_transposed=True)


def _mla_params(seq, w_dq, q_norm, w_uq, w_dkv, kv_norm, w_ukv, q_gain, k_gain, w_o):
    pad_head = LANES - MLA_QK
    half = MLA_ROPE // 2
    w_ukv_r = w_ukv.reshape(MLA_KV_RANK, MLA_HEADS, MLA_NOPE + MLA_V)
    w_uv_p = jnp.pad(w_ukv_r[:, :, MLA_NOPE:], ((0, 0), (0, 0), (0, MLA_VROWS - MLA_V)))
    w_rope = jnp.pad(w_dkv[:, MLA_KV_RANK:], ((0, 0), (MLA_NOPE, pad_head)))
    w_dkv_p = jnp.concatenate([w_dkv[:, :MLA_KV_RANK], w_rope], axis=1)
    vone = np.zeros((MLA_HEADS * MLA_VROWS, TOK_TILE), np.float32)
    vone[np.arange(MLA_HEADS) * MLA_VROWS + MLA_V] = 1.0
    gq = q_gain * ((MLA_QK ** -0.5) * math.log2(math.e))
    inv_freq = ROPE_BASE ** (-jnp.arange(half, dtype=F32) / half)
    ang = jnp.arange(seq).astype(F32)[:, None] * inv_freq[None, :]
    cos, sin = jnp.cos(ang), jnp.sin(ang)
    g1 = lambda g: g[MLA_NOPE:MLA_NOPE + half][None, :]
    g2 = lambda g: g[MLA_NOPE + half:MLA_QK][None, :]
    z = lambda n: jnp.zeros((seq, n), F32)

    def tables(g):
        nope = jnp.broadcast_to(g[None, :MLA_NOPE], (seq, MLA_NOPE))
        c = jnp.concatenate([nope, g1(g) * cos, g2(g) * cos, z(pad_head)], axis=1)
        sa = jnp.concatenate([z(MLA_NOPE), -g2(g) * sin, z(half + pad_head)], axis=1)
        sb = jnp.concatenate([z(MLA_NOPE + half), g1(g) * sin, z(pad_head)], axis=1)
        return c.T, sa.T, sb.T

    qc, qsa, qsb = tables(gq)
    kc, ksa, ksb = tables(k_gain)
    return dict(w_dq=w_dq.astype(BF16), q_norm=q_norm.reshape(1, -1),
                w_uq_t=w_uq.T.astype(BF16), w_dkv=w_dkv_p.astype(BF16), kv_norm=kv_norm.reshape(1, -1),
                w_uk_t=w_ukv_r[:, :, :MLA_NOPE].reshape(MLA_KV_RANK, -1).T.astype(BF16),
                w_uv_t=w_uv_p.reshape(MLA_KV_RANK, -1).T.astype(BF16), vone=jnp.asarray(vone, BF16),
                qc=qc, qsa=qsa, qsb=qsb, kc=kc, ksa=ksa, ksb=ksb, w_o=w_o.astype(BF16))


def _mlstm_proj_kernel(x_ref, g_ref, wt_ref, wk_ref, wc_hi_ref, wc_lo_ref, wr_hi_ref, wr_lo_ref, bc_ref, br_ref,
                       qt_ref, k_ref, vt_ref, ot_ref, gc_ref, gr_ref):
    hn = _rms_rows(x_ref[...], g_ref[...])
    hi, lo = _split2(hn)
    hq, hv = ML_HEADS * ML_QK, ML_HEADS * ML_V
    yt = _dot_nt(wt_ref[...], hi)
    qt_ref[...] = yt[:hq].astype(BF16)
    vt_ref[...] = yt[hq:hq + hv].astype(BF16)
    ot_ref[...] = yt[hq + hv:].astype(BF16)
    k_ref[...] = (_dot(hi, wk_ref[...]) * (ML_QK ** -0.5)).astype(BF16)
    gc_ref[...] = (_dot(hi, wc_hi_ref[...]) + _dot(lo, wc_hi_ref[...]) + _dot(hi, wc_lo_ref[...])) + bc_ref[...]
    gr_ref[...] = (_dot_nt(wr_hi_ref[...], hi) + _dot_nt(wr_hi_ref[...], lo) + _dot_nt(wr_lo_ref[...], hi)) + br_ref[...]


def _cap(g):
    return GATE_CAP * jnp.tanh(g * (1.0 / GATE_CAP))


def _mlstm_rec_kernel(qt_ref, k_ref, vt_ref, opt_ref, gc_ref, gr_ref, hnorm_ref, tril_ref, triu_ref, sel_ref,
                      o_ref, c_ref, m_ref):
    L = REC_TILE

    @pl.when(pl.program_id(1) == 0)
    def _():
        c_ref[...] = jnp.zeros_like(c_ref)
        m_ref[...] = jnp.zeros_like(m_ref)

    gc = _cap(gc_ref[...])
    gr = _cap(gr_ref[...])
    lf_c = _log_sigmoid(gc)
    lf_r = _log_sigmoid(gr)
    c1, c2, c3 = _split3(lf_c)
    r1, r2, r3 = _split3(lf_r)
    tril, triu = tril_ref[...], triu_ref[...]
    bcum_c = _dot(tril, c1) + _dot(tril, c2) + _dot(tril, c3)
    bcum_r = _dot(r1, triu) + _dot(r2, triu) + _dot(r3, triu)
    lane = lax.broadcasted_iota(jnp.int32, (L, LANES), 1)
    x1, x2, x3 = _split3(jnp.where(lane < ML_HEADS, gc, bcum_c))
    sel = sel_ref[...]
    cb_all = _dot(x1, sel) + _dot(x2, sel) + _dot(x3, sel)
    src = lax.broadcasted_iota(jnp.int32, (L, L), 0)
    tgt = lax.broadcasted_iota(jnp.int32, (L, L), 1)
    causal = src <= tgt
    ones = jnp.ones((LANES, L), BF16)

    for h in range(ML_HEADS):
        fh = ML_HEADS + h
        cb = cb_all[:, h * L:(h + 1) * L]
        bt = bcum_r[fh:fh + 1, :]
        li_r = gr[h:h + 1, :]
        m_prev = m_ref[h:h + 1, 0:1]
        qt = qt_ref[h * ML_QK:(h + 1) * ML_QK, :]
        k = k_ref[:, h * ML_QK:(h + 1) * ML_QK]
        vaug = jnp.concatenate([vt_ref[h * ML_V:(h + 1) * ML_V, :], ones], axis=0)
        c_st = c_ref[h]

        dmat = jnp.where(causal, cb + bt, NEG_BIG)
        inter = bt + m_prev
        m_t = jnp.maximum(inter, jnp.max(dmat, axis=0, keepdims=True))
        w_intra = jnp.exp(dmat - m_t)
        w_inter = jnp.exp(inter - m_t)
        pt = (_dot(k, qt) * w_intra).astype(BF16)
        nd = _dot(vaug, pt) + w_inter * _dot(c_st.astype(BF16), qt)
        den = jnp.maximum(jnp.abs(nd[ML_V:ML_V + 1]), jnp.exp(-m_t))
        hc = nd[:ML_V] / den

        b_last = bt[:, L - 1:L]
        m_new = jnp.maximum(b_last + m_prev, jnp.max(b_last - bt + li_r, axis=-1, keepdims=True))
        ws = jnp.exp(cb[:, :ML_QK] + (b_last - m_new))
        wc = jnp.exp(b_last + m_prev - m_new)
        kw = (k.astype(F32) * ws).astype(BF16)
        c_ref[h] = wc * c_st + _dot(vaug, kw)
        m_ref[h:h + 1, :] = jnp.broadcast_to(m_new, (1, LANES))

        rows = slice(h * ML_V, (h + 1) * ML_V)
        hs = hc * lax.rsqrt(jnp.mean(hc * hc, axis=0, keepdims=True) + EPS) * hnorm_ref[rows, :]
        o_ref[rows, :] = (_sigmoid(opt_ref[rows, :].astype(F32)) * hs).astype(BF16)


def _mlstm_consts(n):
    r = np.arange(n)
    tril = (r[None, :] <= r[:, None]).astype(np.float32)
    sel = np.zeros((LANES, ML_HEADS * n), np.float32)
    for h in range(ML_HEADS):
        sel[h, h * n:(h + 1) * n] = 1.0
        sel[ML_HEADS + h, h * n:(h + 1) * n] = -1.0
    return jnp.asarray(tril, BF16), jnp.asarray(tril.T, BF16), jnp.asarray(sel, BF16)


def _mlstm_layer(x, bsz, seq, g_mix, p, g_ffn, w1, w2):
    t, d = x.shape
    hq, hv = ML_HEADS * ML_QK, ML_HEADS * ML_V
    row = lambda i: (i, 0)
    col = lambda i: (0, i)
    qt, k, vt, opt, gcol, grow = pl.pallas_call(
        _mlstm_proj_kernel,
        grid=(t // TOK_TILE,),
        in_specs=[pl.BlockSpec((TOK_TILE, d), row), _const_spec((1, d)), _const_spec((hq + 2 * hv, d)),
                  _const_spec((d, hq)), _const_spec((d, LANES)), _const_spec((d, LANES)),
                  _const_spec((SUBLANES, d)), _const_spec((SUBLANES, d)),
                  _const_spec((1, LANES)), _const_spec((SUBLANES, 1))],
        out_specs=[pl.BlockSpec((hq, TOK_TILE), col), pl.BlockSpec((TOK_TILE, hq), row),
                   pl.BlockSpec((hv, TOK_TILE), col), pl.BlockSpec((hv, TOK_TILE), col),
                   pl.BlockSpec((TOK_TILE, LANES), row), pl.BlockSpec((SUBLANES, TOK_TILE), col)],
        out_shape=[jax.ShapeDtypeStruct((hq, t), BF16), jax.ShapeDtypeStruct((t, hq), BF16),
                   jax.ShapeDtypeStruct((hv, t), BF16), jax.ShapeDtypeStruct((hv, t), BF16),
                   jax.ShapeDtypeStruct((t, LANES), F32), jax.ShapeDtypeStruct((SUBLANES, t), F32)],
        compiler_params=_cparams("parallel"),
        name="mlstm_proj",
    )(x, g_mix, p["w_t"], p["w_k"], p["wc_hi"], p["wc_lo"], p["wr_hi"], p["wr_lo"], p["b_col"], p["b_row"])

    nc = seq // REC_TILE
    tril, triu, sel = _mlstm_consts(REC_TILE)
    hnorm = jnp.broadcast_to(p["head_norm"].reshape(hv, 1), (hv, REC_TILE))
    rblk = lambda b, c: (b * nc + c, 0)
    cblk = lambda b, c: (0, b * nc + c)
    at = pl.pallas_call(
        _mlstm_rec_kernel,
        grid=(bsz, nc),
        in_specs=[pl.BlockSpec((hq, REC_TILE), cblk), pl.BlockSpec((REC_TILE, hq), rblk),
                  pl.BlockSpec((hv, REC_TILE), cblk), pl.BlockSpec((hv, REC_TILE), cblk),
                  pl.BlockSpec((REC_TILE, LANES), rblk), pl.BlockSpec((SUBLANES, REC_TILE), cblk),
                  _const_spec((hv, REC_TILE)), _const_spec((REC_TILE, REC_TILE)), _const_spec((REC_TILE, REC_TILE)),
                  _const_spec((LANES, ML_HEADS * REC_TILE))],
        out_specs=pl.BlockSpec((hv, REC_TILE), cblk),
        out_shape=jax.ShapeDtypeStruct((hv, t), BF16),
        scratch_shapes=[pltpu.VMEM((ML_HEADS, ML_V + LANES, ML_QK), F32), pltpu.VMEM((SUBLANES, LANES), F32)],
        compiler_params=_cparams("parallel", "arbitrary"),
        name="mlstm_rec",
    )(qt, k, vt, opt, gcol, grow, hnorm, tril, triu, sel)
    return _mlp_call(x, at, p["w_o"], None, g_ffn, w1, w2, a_transposed=True)


def _mlstm_params(w_in, w_if, b_if, head_norm, w_o):
    ng = 2 * ML_HEADS
    hq = ML_HEADS * ML_QK
    w_col = jnp.pad(w_if, ((0, 0), (0, LANES - ng)))
    wc_hi = w_col.astype(BF16)
    wc_lo = (w_col - wc_hi.astype(F32)).astype(BF16)
    w_row = w_if.T
    wr_hi = w_row.astype(BF16)
    wr_lo = (w_row - wr_hi.astype(F32)).astype(BF16)
    w_t = jnp.concatenate([w_in[:, :hq], w_in[:, 2 * hq:]], axis=1).T
    return dict(w_t=w_t.astype(BF16), w_k=w_in[:, hq:2 * hq].astype(BF16),
                wc_hi=wc_hi, wc_lo=wc_lo, wr_hi=wr_hi, wr_lo=wr_lo,
                b_col=jnp.pad(b_if, (0, LANES - ng)).reshape(1, LANES), b_row=b_if.reshape(ng, 1),
                head_norm=head_norm, w_o=w_o.astype(BF16))


def _gla_proj_kernel(x_ref, g_ref, win_ref, wa1_ref, wa2_ref, ba_ref, q_ref, k_ref, v_ref, r_ref, la_ref):
    hn = _rms_rows(x_ref[...], g_ref[...]).astype(BF16)
    y = _dot(hn, win_ref[...])
    hk, hv = GLA_HEADS * GLA_K, GLA_HEADS * GLA_V
    q_ref[...] = (y[:, :hk] * (GLA_K ** -0.5)).astype(BF16)
    k_ref[...] = y[:, hk:2 * hk].astype(BF16)
    v_ref[...] = y[:, 2 * hk:2 * hk + hv].astype(BF16)
    r_ref[...] = y[:, 2 * hk + hv:].astype(BF16)
    z = _dot(_dot(hn, wa1_ref[...]).astype(BF16), wa2_ref[...]) + ba_ref[...]
    la_ref[...] = _log_sigmoid(z) * (1.0 / GLA_TAU)


def _gla_levels():
    return [2 ** j for j in range(1, int(math.log2(REC_TILE)) + 1)]


def _gla_consts():
    n = REC_TILE
    t = np.arange(n)[:, None]
    u = np.arange(n)[None, :]
    mats = []
    for p in _gla_levels():
        first_upper = (t // p) * p + p // 2
        upper = (t % p) >= p // 2
        m_up = (u > first_upper) & (u <= t)
        m_lo = (u > t) & (u <= first_upper)
        mats.append(np.where(upper, m_up, m_lo))
    mats.append(u <= t)
    mats.append(u > t)
    return jnp.asarray(np.stack(mats).astype(np.float32), BF16)


def _gla_rec_kernel(q_ref, k_ref, v_ref, r_ref, la_ref, hnorm_ref, w_ref, o_ref, c_ref):
    L = REC_TILE
    levels = _gla_levels()
    nl = len(levels)

    @pl.when(pl.program_id(1) == 0)
    def _():
        c_ref[...] = jnp.zeros_like(c_ref)

    la_hi, la_lo = _split2(la_ref[...])

    def decay(j):
        w = w_ref[j]
        return jnp.exp(_dot(w, la_hi) + _dot(w, la_lo))

    row = lax.broadcasted_iota(jnp.int32, (L, L), 0)
    col = lax.broadcasted_iota(jnp.int32, (L, L), 1)
    rowk = lax.broadcasted_iota(jnp.int32, (L, GLA_K), 0)
    e_cum = decay(nl)
    e_rev = decay(nl + 1)
    e_lvl = [decay(j) for j in range(nl)]

    for h in range(GLA_HEADS):
        ks = slice(h * GLA_K, (h + 1) * GLA_K)
        vs = slice(h * GLA_V, (h + 1) * GLA_V)
        q = q_ref[:, ks].astype(F32)
        k = k_ref[:, ks].astype(F32)
        v = v_ref[:, vs]
        att = jnp.where(row == col, _dot_nt(q_ref[:, ks], k_ref[:, ks]), 0.0)
        for j, p in enumerate(levels):
            upper = (rowk & (p - 1)) >= (p // 2)
            e = e_lvl[j][:, ks]
            qf = jnp.where(upper, q * e, 0.0).astype(BF16)
            kf = jnp.where(upper, 0.0, k * e).astype(BF16)
            a = _dot_nt(qf, kf)
            if p < L:
                shift = int(math.log2(p))
                a = jnp.where((row >> shift) == (col >> shift), a, 0.0)
            att = att + a
        c_st = c_ref[h]
        o = _dot(att.astype(BF16), v) + _dot_nt((q * e_cum[:, ks]).astype(BF16), c_st.astype(BF16))
        c_ref[h] = e_cum[L - 1:L, ks] * c_st + _dot_tn(v, (k * e_rev[:, ks]).astype(BF16))
        o = _rms_rows(o, hnorm_ref[:, vs])
        rr = r_ref[:, vs].astype(F32)
        o_ref[:, vs] = (o * rr * _sigmoid(rr)).astype(BF16)


def _gla_layer(x, bsz, seq, g_mix, p, g_ffn, w1, w2):
    t, d = x.shape
    hk, hv = GLA_HEADS * GLA_K, GLA_HEADS * GLA_V
    row = lambda i: (i, 0)
    q, k, v, r, la = pl.pallas_call(
        _gla_proj_kernel,
        grid=(t // TOK_TILE,),
        in_specs=[pl.BlockSpec((TOK_TILE, d), row), _const_spec((1, d)), _const_spec((d, 2 * hk + 2 * hv)),
                  _const_spec((d, LANES)), _const_spec((LANES, hk)), _const_spec((1, hk))],
        out_specs=[pl.BlockSpec((TOK_TILE, hk), row), pl.BlockSpec((TOK_TILE, hk), row),
                   pl.BlockSpec((TOK_TILE, hv), row), pl.BlockSpec((TOK_TILE, hv), row),
                   pl.BlockSpec((TOK_TILE, hk), row)],
        out_shape=[jax.ShapeDtypeStruct((t, hk), BF16), jax.ShapeDtypeStruct((t, hk), BF16),
                   jax.ShapeDtypeStruct((t, hv), BF16), jax.ShapeDtypeStruct((t, hv), BF16),
                   jax.ShapeDtypeStruct((t, hk), F32)],
        compiler_params=_cparams("parallel"),
        name="gla_proj",
    )(x, g_mix, p["w_in"], p["w_a1"], p["w_a2"], p["b_a"])

    nc = seq // REC_TILE
    wmats = _gla_consts()
    blk = lambda b, c: (b * nc + c, 0)
    a = pl.pallas_call(
        _gla_rec_kernel,
        grid=(bsz, nc),
        in_specs=[pl.BlockSpec((REC_TILE, hk), blk), pl.BlockSpec((REC_TILE, hk), blk),
                  pl.BlockSpec((REC_TILE, hv), blk), pl.BlockSpec((REC_TILE, hv), blk),
                  pl.BlockSpec((REC_TILE, hk), blk), _const_spec((1, hv)), _const_spec(wmats.shape)],
        out_specs=pl.BlockSpec((REC_TILE, hv), blk),
        out_shape=jax.ShapeDtypeStruct((t, hv), BF16),
        scratch_shapes=[pltpu.VMEM((GLA_HEADS, GLA_V, GLA_K), F32)],
        compiler_params=_cparams("parallel", "arbitrary"),
        name="gla_rec",
    )(q, k, v, r, la, p["head_norm"], wmats)
    return _mlp_call(x, a, p["w_o"], None, g_ffn, w1, w2)


def _gla_params(w_in, w_a1, w_a2, b_a, head_norm, w_o):
    return dict(w_in=w_in.astype(BF16),
                w_a1=jnp.pad(w_a1, ((0, 0), (0, LANES - GLA_GATE_RANK))).astype(BF16),
                w_a2=jnp.pad(w_a2, ((0, LANES - GLA_GATE_RANK), (0, 0))).astype(BF16),
                b_a=b_a.reshape(1, -1), head_norm=head_norm.reshape(1, -1), w_o=w_o.astype(BF16))


def _conv_kernel(x_ref, g_ref, w1_ref, b1_ref, wdw_ref, lng_ref, lnb_ref, o_ref, u_ref, c_ref):
    tm = TOK_TILE
    d = D_MODEL

    nbuf = CONV_HALO + tm

    @pl.when(pl.program_id(1) == 0)
    def _():
        u_ref[0, :, 0:CONV_HALO, :] = jnp.zeros((d // LANES, CONV_HALO, LANES), F32)

    @pl.when(pl.program_id(1) != 0)
    def _():
        u_ref[0, :, 0:CONV_HALO, :] = u_ref[0, :, tm:tm + CONV_HALO, :]

    hn = _rms_rows(x_ref[...], g_ref[...]).astype(BF16)
    y = _dot(hn, w1_ref[...]) + b1_ref[...]
    u = y[:, :d] * _sigmoid(y[:, d:])
    lead = CONV_HALO - (CONV_WIDTH - 1)
    groups = CONV_ROWS // SUBLANES

    for cb in range(d // LANES):
        cols = slice(cb * LANES, (cb + 1) * LANES)
        u_ref[0, cb, CONV_HALO:, :] = u[:, cols]
        full = u_ref[0, cb]
        for s in range(1, SUBLANES):
            u_ref[s, cb] = pltpu.roll(full, nbuf - s, 0)

        def taps(c, carry):
            r0 = pl.multiple_of(c * CONV_ROWS, CONV_ROWS)
            acc = jnp.broadcast_to(wdw_ref[CONV_WIDTH, :, cols][None], (groups, SUBLANES, LANES))
            for j in range(CONV_WIDTH):
                off = lead + j
                start = pl.multiple_of(r0 + (off // SUBLANES) * SUBLANES, SUBLANES)
                uj = u_ref[off % SUBLANES, cb, pl.ds(start, CONV_ROWS), :].reshape(groups, SUBLANES, LANES)
                acc = acc + uj * wdw_ref[j, :, cols][None]
            c_ref[pl.ds(r0, CONV_ROWS), cols] = acc.reshape(CONV_ROWS, LANES)
            return carry

        lax.fori_loop(0, tm // CONV_ROWS, taps, 0)

    def norm(c, carry):
        r0 = pl.multiple_of(c * CONV_NORM_ROWS, CONV_NORM_ROWS)
        acc = c_ref[pl.ds(r0, CONV_NORM_ROWS), :]
        mu = jnp.mean(acc, axis=-1, keepdims=True)
        cen = acc - mu
        var = jnp.mean(cen * cen, axis=-1, keepdims=True)
        z = cen * lax.rsqrt(var + EPS) * lng_ref[...] + lnb_ref[...]
        o_ref[pl.ds(r0, CONV_NORM_ROWS), :] = (z * _sigmoid(z)).astype(BF16)
        return carry

    lax.fori_loop(0, tm // CONV_NORM_ROWS, norm, 0, unroll=4)


def _conv_layer(x, bsz, seq, g_mix, p, g_ffn, w1, w2):
    t, d = x.shape
    nt = seq // TOK_TILE
    blk = lambda b, i: (b * nt + i, 0)
    c = pl.pallas_call(
        _conv_kernel,
        grid=(bsz, nt),
        in_specs=[pl.BlockSpec((TOK_TILE, d), blk), _const_spec((1, d)), _const_spec((d, 2 * d)),
                  _const_spec((1, 2 * d)), _const_spec((CONV_WIDTH + 1, SUBLANES, d)),
                  _const_spec((1, d)), _const_spec((1, d))],
        out_specs=pl.BlockSpec((TOK_TILE, d), blk),
        out_shape=jax.ShapeDtypeStruct((t, d), BF16),
        scratch_shapes=[pltpu.VMEM((SUBLANES, d // LANES, CONV_HALO + TOK_TILE, LANES), F32),
                        pltpu.VMEM((TOK_TILE, d), F32)],
        compiler_params=_cparams("parallel", "arbitrary"),
        name="conv",
    )(x, g_mix, p["w_pw1"], p["b_pw1"], p["w_dw"], p["ln_g"], p["ln_b"])
    return _mlp_call(x, c, p["w_pw2"], p["b_pw2"], g_ffn, w1, w2)


def _conv_params(w_pw1, b_pw1, w_dw, b_dw, ln_g, ln_b, w_pw2, b_pw2):
    return dict(w_pw1=w_pw1.astype(BF16), b_pw1=b_pw1.reshape(1, -1),
                w_dw=jnp.broadcast_to(jnp.concatenate([w_dw, b_dw[None]], axis=0)[:, None, :],
                                      (CONV_WIDTH + 1, SUBLANES, w_dw.shape[1])),
                ln_g=ln_g.reshape(1, -1), ln_b=ln_b.reshape(1, -1),
                w_pw2=w_pw2.astype(BF16), b_pw2=b_pw2.reshape(1, -1))


def kernel(x, norm_mix, norm_ffn, mla_w_dq, mla_q_norm, mla_w_uq, mla_w_dkv, mla_kv_norm, mla_w_ukv, mla_q_gain, mla_k_gain, mla_w_o, mlstm_w_in, mlstm_w_if, mlstm_b_if, mlstm_head_norm, mlstm_w_o, gla_w_in, gla_w_a1, gla_w_a2, gla_b_a, gla_head_norm, gla_w_o, conv_w_pw1, conv_b_pw1, conv_w_dw, conv_b_dw, conv_ln_g, conv_ln_b, conv_w_pw2, conv_b_pw2, ffn_w1, ffn_w2):
    bsz, seq, d = x.shape
    depth = norm_mix.shape[0]
    assert d == D_MODEL and seq % ATT_Q_TILE == 0 and seq % TOK_TILE == 0 and seq % REC_TILE == 0
    h = x.reshape(bsz * seq, d)
    for i in range(depth):
        kind, j = i % 4, i // 4
        g_mix = norm_mix[i].reshape(1, d)
        g_ffn = norm_ffn[i].reshape(1, d)
        w1 = ffn_w1[i].astype(BF16)
        w2 = ffn_w2[i].astype(BF16)
        if kind == 0:
            p = _mla_params(seq, mla_w_dq[j], mla_q_norm[j], mla_w_uq[j], mla_w_dkv[j], mla_kv_norm[j],
                            mla_w_ukv[j], mla_q_gain[j], mla_k_gain[j], mla_w_o[j])
            h = _mla_layer(h, bsz, seq, g_mix, p, g_ffn, w1, w2)
        elif kind == 1:
            p = _mlstm_params(mlstm_w_in[j], mlstm_w_if[j], mlstm_b_if[j], mlstm_head_norm[j], mlstm_w_o[j])
            h = _mlstm_layer(h, bsz, seq, g_mix, p, g_ffn, w1, w2)
        elif kind == 2:
            p = _gla_params(gla_w_in[j], gla_w_a1[j], gla_w_a2[j], gla_b_a[j], gla_head_norm[j], gla_w_o[j])
            h = _gla_layer(h, bsz, seq, g_mix, p, g_ffn, w1, w2)
        else:
            p = _conv_params(conv_w_pw1[j], conv_b_pw1[j], conv_w_dw[j], conv_b_dw[j], conv_ln_g[j],
                             conv_ln_b[j], conv_w_pw2[j], conv_b_pw2[j])
            h = _conv_layer(h, bsz, seq, g_mix, p, g_ffn, w1, w2)
    return h.reshape(bsz, seq, d)
```

```python
import functools
import math

import numpy as np
import jax
import jax.numpy as jnp
from jax import lax
from jax.experimental import pallas as pl
from jax.experimental.pallas import tpu as pltpu

F32 = jnp.float32
BF16 = jnp.bfloat16

D_MODEL = 1024
D_FF = 4 * D_MODEL
EPS = 1e-6
CHUNK = 64

MLA_HEADS = 16
MLA_NOPE = 64
MLA_ROPE = 32
MLA_QK = MLA_NOPE + MLA_ROPE
MLA_V = 64
MLA_Q_RANK = 384
MLA_KV_RANK = 256
ROPE_BASE = 10000.0
MLA_VROWS = MLA_V + 16

ML_HEADS = 4
ML_QK = D_MODEL // 8
ML_V = D_MODEL // 4
GATE_CAP = 15.0

GLA_HEADS = 4
GLA_K = D_MODEL // 8
GLA_V = D_MODEL // 4
GLA_GATE_RANK = 16
GLA_TAU = 16.0

CONV_WIDTH = 31

LANES = 128
SUBLANES = 8
VMEM_LIMIT = 56 * 1024 * 1024

TOK_TILE = 512
FF_TILE = 1024
ATT_TILE = 512
ATT_Q_TILE = 1024
ATT_HEADS = 4
REC_TILE = 256
CONV_HALO = 32
CONV_ROWS = 64
CONV_NORM_ROWS = 32
NEG_BIG = -1e30


def _cparams(*sem):
    return pltpu.CompilerParams(dimension_semantics=sem, vmem_limit_bytes=VMEM_LIMIT)


def _const_spec(shape):
    nd = len(shape)
    return pl.BlockSpec(shape, lambda *_: (0,) * nd, pipeline_mode=pl.Buffered(1))


def _dot(a, b):
    return jnp.dot(a, b, preferred_element_type=F32)


def _dot_nt(a, b):
    return lax.dot_general(a, b, (((1,), (1,)), ((), ())), preferred_element_type=F32)


def _dot_tn(a, b):
    return lax.dot_general(a, b, (((0,), (0,)), ((), ())), preferred_element_type=F32)


def _split2(a):
    hi = a.astype(BF16)
    lo = (a - hi.astype(F32)).astype(BF16)
    return hi, lo


def _split3(a):
    hi = a.astype(BF16)
    r = a - hi.astype(F32)
    mid = r.astype(BF16)
    lo = (r - mid.astype(F32)).astype(BF16)
    return hi, mid, lo


def _rms_rows(x, g):
    return x * lax.rsqrt(jnp.mean(x * x, axis=-1, keepdims=True) + EPS) * g


def _log_sigmoid(z):
    return jnp.minimum(z, 0.0) - jnp.log1p(jnp.exp(-jnp.abs(z)))


def _sigmoid(z):
    return 1.0 / (1.0 + jnp.exp(-z))


def _mlp_kernel(*refs, has_bias, a_transposed):
    if has_bias:
        x_ref, a_ref, wo_ref, bo_ref, g_ref, w1_ref, w2_ref, o_ref = refs
    else:
        x_ref, a_ref, wo_ref, g_ref, w1_ref, w2_ref, o_ref = refs
    if a_transposed:
        x1 = x_ref[...] + _dot_tn(a_ref[...], wo_ref[...])
    else:
        x1 = x_ref[...] + _dot(a_ref[...], wo_ref[...])
    if has_bias:
        x1 = x1 + bo_ref[...]
    hn = _rms_rows(x1, g_ref[...]).astype(BF16)
    acc = x1
    for c in range(D_FF // FF_TILE):
        h = _dot(hn, w1_ref[:, c * FF_TILE:(c + 1) * FF_TILE])
        h = jnp.maximum(h, 0.0)
        acc = acc + _dot((h * h).astype(BF16), w2_ref[c * FF_TILE:(c + 1) * FF_TILE, :])
    o_ref[...] = acc


def _mlp_call(x, a, w_o, b_o, g, w1, w2, a_transposed=False):
    t, d = x.shape
    din = w_o.shape[0]
    has_bias = b_o is not None
    row = lambda i: (i, 0)
    a_spec = pl.BlockSpec((din, TOK_TILE), lambda i: (0, i)) if a_transposed else pl.BlockSpec((TOK_TILE, din), row)
    in_specs = [pl.BlockSpec((TOK_TILE, d), row), a_spec, _const_spec((din, d))]
    args = [x, a, w_o]
    if has_bias:
        in_specs.append(_const_spec((1, d)))
        args.append(b_o)
    in_specs += [_const_spec((1, d)), _const_spec((d, D_FF)), _const_spec((D_FF, d))]
    args += [g, w1, w2]
    return pl.pallas_call(
        functools.partial(_mlp_kernel, has_bias=has_bias, a_transposed=a_transposed),
        grid=(t // TOK_TILE,),
        in_specs=in_specs,
        out_specs=pl.BlockSpec((TOK_TILE, d), row),
        out_shape=jax.ShapeDtypeStruct((t, d), F32),
        compiler_params=_cparams("parallel"),
        name="mlp",
    )(*args)


def _mla_proj_kernel(x_ref, g_ref, wdq_ref, qn_ref, wuqt_ref, wdkv_ref, kvn_ref, wukt_ref, wuvt_ref,
                     vone_ref, qc_ref, qsa_ref, qsb_ref, kc_ref, ksa_ref, ksb_ref,
                     qt_ref, k_ref, vt_ref):
    half = MLA_ROPE // 2
    hn = _rms_rows(x_ref[...], g_ref[...]).astype(BF16)
    cq = _rms_rows(_dot(hn, wdq_ref[...]), qn_ref[...]).astype(BF16)
    dkv = _dot(hn, wdkv_ref[...])
    ckv = _rms_rows(dkv[:, :MLA_KV_RANK], kvn_ref[...]).astype(BF16)

    vt_ref[0] = (_dot_nt(wuvt_ref[...], ckv) + vone_ref[...]).astype(BF16)

    n0, a0, b0, e0 = 0, MLA_NOPE, MLA_NOPE + half, MLA_QK

    def ssq(t):
        return jnp.sum(t * t, axis=0, keepdims=True)

    def rope(a, b, c, sa, sb):
        return a * c[a0:b0] + b * sa[a0:b0], b * c[b0:e0] + a * sb[b0:e0]

    def inv_rms(s):
        return lax.rsqrt(s * (1.0 / MLA_QK) + EPS)

    tm = x_ref.shape[0]
    pad_rows = jnp.zeros((LANES - MLA_QK, tm), F32)

    qt = _dot_nt(wuqt_ref[...], cq)
    qc, qsa, qsb = qc_ref[...], qsa_ref[...], qsb_ref[...]
    for h in range(MLA_HEADS):
        t = qt[h * MLA_QK:(h + 1) * MLA_QK]
        n, a, b = t[n0:a0], t[a0:b0], t[b0:e0]
        r = inv_rms(ssq(n) + ssq(a) + ssq(b))
        ra, rb = rope(a, b, qc, qsa, qsb)
        qt_ref[h * LANES:h * LANES + MLA_QK, :] = jnp.concatenate(
            [n * qc[n0:a0] * r, ra * r, rb * r], axis=0).astype(BF16)
        qt_ref[h * LANES + MLA_QK:(h + 1) * LANES, :] = pad_rows.astype(BF16)

    krt = dkv[:, MLA_KV_RANK:].T
    kt = _dot_nt(wukt_ref[...], ckv)
    kc, ksa, ksb = kc_ref[...], ksa_ref[...], ksb_ref[...]
    ka, kb = krt[a0:b0], krt[b0:e0]
    ssq_rope = ssq(ka) + ssq(kb)
    ra, rb = rope(ka, kb, kc, ksa, ksb)
    for h in range(MLA_HEADS):
        n = kt[h * MLA_NOPE:(h + 1) * MLA_NOPE]
        r = inv_rms(ssq(n) + ssq_rope)
        kh = jnp.concatenate([n * kc[n0:a0] * r, ra * r, rb * r, pad_rows], axis=0)
        k_ref[:, h * LANES:(h + 1) * LANES] = kh.T.astype(BF16)


def _attn_kernel(qt_ref, k_ref, vt_ref, o_ref, s_ref, acc_ref, m_ref, mb_ref):
    i = pl.program_id(2)
    tq, tk = ATT_Q_TILE, ATT_TILE
    nh = ATT_HEADS
    kpq = tq // tk
    assert kpq == 2
    vrows = vt_ref.shape[1] // nh
    krow = lax.broadcasted_iota(jnp.int32, (tk, tq), 0)
    qcol = lax.broadcasted_iota(jnp.int32, (tk, tq), 1)
    diag_mask = (krow // CHUNK) <= (qcol // CHUNK)
    late = slice(tk, tq)

    def produce(h, kb, qcols=slice(None)):
        start = pl.multiple_of(kb * tk, tk)
        st = _dot(k_ref[pl.ds(start, tk), h * LANES:(h + 1) * LANES], qt_ref[h * LANES:(h + 1) * LANES, qcols])
        s_ref[h, :, qcols] = st
        mb_ref[h, :, qcols] = jnp.max(st, axis=0, keepdims=True)

    def consume(h, kb, mask=None, qcols=slice(None)):
        st = s_ref[h, :, qcols]
        if mask is None:
            mblk = mb_ref[h, :, qcols]
        else:
            st = jnp.where(mask, st, NEG_BIG)
            mblk = jnp.max(st, axis=0, keepdims=True)
        m = m_ref[h, :, qcols]
        m_new = jnp.maximum(m, mblk)
        m_ref[h, :, qcols] = m_new
        pt = jnp.exp2(st - m_new).astype(BF16)
        acc_ref[h, :, qcols] = (jnp.exp2(m - m_new) * acc_ref[h, :, qcols]
                                + _dot(vt_ref[kb, h * vrows:(h + 1) * vrows, :], pt))

    def body(j, carry):
        for h in range(nh):
            produce((h + 1) % nh, j + (h + 1) // nh)
            consume(h, j)
        return carry

    m_ref[...] = jnp.full(m_ref.shape, NEG_BIG, F32)
    acc_ref[...] = jnp.zeros(acc_ref.shape, F32)
    first = kpq * i
    produce(0, 0)
    lax.fori_loop(0, i, lambda t, c: body(kpq * t + 1, body(kpq * t, c)), 0)
    for h in range(nh):
        if h + 1 < nh:
            produce(h + 1, first)
        else:
            produce(0, first + 1, late)
        consume(h, first, diag_mask)
    for h in range(nh):
        if h + 1 < nh:
            produce(h + 1, first + 1, late)
        consume(h, first + 1, diag_mask[:, :tk], late)
        acc = acc_ref[h]
        o_ref[h * MLA_V:(h + 1) * MLA_V, :] = (acc[:MLA_V] / acc[MLA_V:MLA_V + 1]).astype(BF16)


def _mla_layer(x, bsz, seq, g_mix, p, g_ffn, w1, w2):
    t, d = x.shape
    assert TOK_TILE == ATT_TILE
    hp = MLA_HEADS * LANES
    hv = MLA_HEADS * MLA_VROWS
    nt = seq // TOK_TILE
    row = lambda i: (i, 0)
    ttab = lambda i: (i % nt, 0)
    ftab = lambda i: (0, i % nt)
    qt, k, vt = pl.pallas_call(
        _mla_proj_kernel,
        grid=(t // TOK_TILE,),
        in_specs=[pl.BlockSpec((TOK_TILE, d), row), _const_spec((1, d)),
                  _const_spec((d, MLA_Q_RANK)), _const_spec((1, MLA_Q_RANK)),
                  _const_spec((MLA_HEADS * MLA_QK, MLA_Q_RANK)),
                  _const_spec((d, MLA_KV_RANK + LANES)), _const_spec((1, MLA_KV_RANK)),
                  _const_spec((MLA_HEADS * MLA_NOPE, MLA_KV_RANK)), _const_spec((hv, MLA_KV_RANK)),
                  _const_spec((hv, TOK_TILE))] + [pl.BlockSpec((LANES, TOK_TILE), ftab)] * 6,
        out_specs=[pl.BlockSpec((hp, TOK_TILE), lambda i: (0, i)), pl.BlockSpec((TOK_TILE, hp), row),
                   pl.BlockSpec((1, hv, TOK_TILE), lambda i: (i, 0, 0))],
        out_shape=[jax.ShapeDtypeStruct((hp, t), BF16), jax.ShapeDtypeStruct((t, hp), BF16),
                   jax.ShapeDtypeStruct((t // TOK_TILE, hv, TOK_TILE), BF16)],
        compiler_params=_cparams("parallel"),
        name="mla_proj",
    )(x, g_mix, p["w_dq"], p["q_norm"], p["w_uq_t"], p["w_dkv"], p["kv_norm"], p["w_uk_t"], p["w_uv_t"],
      p["vone"], p["qc"], p["qsa"], p["qsb"], p["kc"], p["ksa"], p["ksb"])

    nq = seq // ATT_Q_TILE
    ot = pl.pallas_call(
        _attn_kernel,
        grid=(bsz, MLA_HEADS // ATT_HEADS, nq),
        in_specs=[pl.BlockSpec((ATT_HEADS * LANES, ATT_Q_TILE), lambda b, h, i: (h, b * nq + i)),
                  pl.BlockSpec((seq, ATT_HEADS * LANES), lambda b, h, i: (b, h)),
                  pl.BlockSpec((seq // ATT_TILE, ATT_HEADS * MLA_VROWS, ATT_TILE), lambda b, h, i: (b, h, 0))],
        out_specs=pl.BlockSpec((ATT_HEADS * MLA_V, ATT_Q_TILE), lambda b, h, i: (h, b * nq + i)),
        out_shape=jax.ShapeDtypeStruct((MLA_HEADS * MLA_V, t), BF16),
        scratch_shapes=[pltpu.VMEM((ATT_HEADS, ATT_TILE, ATT_Q_TILE), F32),
                        pltpu.VMEM((ATT_HEADS, MLA_VROWS, ATT_Q_TILE), F32),
                        pltpu.VMEM((ATT_HEADS, 1, ATT_Q_TILE), F32), pltpu.VMEM((ATT_HEADS, 1, ATT_Q_TILE), F32)],
        compiler_params=_cparams("parallel", "parallel", "arbitrary"),
        name="mla_attn",
    )(qt, k, vt)
    return _mlp_call(x, ot, p["w_o"], None, g_ffn, w1, w2, a_transposed=True)


def _mla_params(seq, w_dq, q_norm, w_uq, w_dkv, kv_norm, w_ukv, q_gain, k_gain, w_o):
    pad_head = LANES - MLA_QK
    half = MLA_ROPE // 2
    w_ukv_r = w_ukv.reshape(MLA_KV_RANK, MLA_HEADS, MLA_NOPE + MLA_V)
    w_uv_p = jnp.pad(w_ukv_r[:, :, MLA_NOPE:], ((0, 0), (0, 0), (0, MLA_VROWS - MLA_V)))
    w_rope = jnp.pad(w_dkv[:, MLA_KV_RANK:], ((0, 0), (MLA_NOPE, pad_head)))
    w_dkv_p = jnp.concatenate([w_dkv[:, :MLA_KV_RANK], w_rope], axis=1)
    vone = np.zeros((MLA_HEADS * MLA_VROWS, TOK_TILE), np.float32)
    vone[np.arange(MLA_HEADS) * MLA_VROWS + MLA_V] = 1.0
    gq = q_gain * ((MLA_QK ** -0.5) * math.log2(math.e))
    inv_freq = ROPE_BASE ** (-jnp.arange(half, dtype=F32) / half)
    ang = jnp.arange(seq).astype(F32)[:, None] * inv_freq[None, :]
    cos, sin = jnp.cos(ang), jnp.sin(ang)
    g1 = lambda g: g[MLA_NOPE:MLA_NOPE + half][None, :]
    g2 = lambda g: g[MLA_NOPE + half:MLA_QK][None, :]
    z = lambda n: jnp.zeros((seq, n), F32)

    def tables(g):
        nope = jnp.broadcast_to(g[None, :MLA_NOPE], (seq, MLA_NOPE))
        c = jnp.concatenate([nope, g1(g) * cos, g2(g) * cos, z(pad_head)], axis=1)
        sa = jnp.concatenate([z(MLA_NOPE), -g2(g) * sin, z(half + pad_head)], axis=1)
        sb = jnp.concatenate([z(MLA_NOPE + half), g1(g) * sin, z(pad_head)], axis=1)
        return c.T, sa.T, sb.T

    qc, qsa, qsb = tables(gq)
    kc, ksa, ksb = tables(k_gain)
    return dict(w_dq=w_dq.astype(BF16), q_norm=q_norm.reshape(1, -1),
                w_uq_t=w_uq.T.astype(BF16), w_dkv=w_dkv_p.astype(BF16), kv_norm=kv_norm.reshape(1, -1),
                w_uk_t=w_ukv_r[:, :, :MLA_NOPE].reshape(MLA_KV_RANK, -1).T.astype(BF16),
                w_uv_t=w_uv_p.reshape(MLA_KV_RANK, -1).T.astype(BF16), vone=jnp.asarray(vone, BF16),
                qc=qc, qsa=qsa, qsb=qsb, kc=kc, ksa=ksa, ksb=ksb, w_o=w_o.astype(BF16))


def _mlstm_proj_kernel(x_ref, g_ref, wt_ref, wk_ref, wr_hi_ref, wr_lo_ref, br_ref,
                       qt_ref, k_ref, vt_ref, ot_ref, gr_ref):
    hn = _rms_rows(x_ref[...], g_ref[...])
    hi, lo = _split2(hn)
    hq, hv = ML_HEADS * ML_QK, ML_HEADS * ML_V
    yt = _dot_nt(wt_ref[...], hi)
    qt_ref[...] = yt[:hq].astype(BF16)
    vt_ref[...] = yt[hq:hq + hv].astype(BF16)
    ot_ref[...] = yt[hq + hv:].astype(BF16)
    k_ref[...] = (_dot(hi, wk_ref[...]) * (ML_QK ** -0.5)).astype(BF16)
    gr_ref[...] = (_dot_nt(wr_hi_ref[...], hi) + _dot_nt(wr_hi_ref[...], lo) + _dot_nt(wr_lo_ref[...], hi)) + br_ref[...]


def _cap(g):
    return GATE_CAP * jnp.tanh(g * (1.0 / GATE_CAP))


def _mlstm_rec_kernel(qt_ref, k_ref, vt_ref, opt_ref, gr_ref, hnorm_ref, triu_ref, sel_ref,
                      o_ref, c_ref, m_ref):
    L = REC_TILE

    @pl.when(pl.program_id(1) == 0)
    def _():
        c_ref[...] = jnp.zeros_like(c_ref)
        m_ref[...] = jnp.zeros_like(m_ref)

    gr = _cap(gr_ref[...])
    r1, r2, r3 = _split3(_log_sigmoid(gr))
    triu = triu_ref[...]
    bcum_r = _dot(r1, triu) + _dot(r2, triu) + _dot(r3, triu)
    grow = lax.broadcasted_iota(jnp.int32, (SUBLANES, L), 0)
    xr = jnp.concatenate([jnp.where(grow < ML_HEADS, gr, bcum_r), jnp.zeros((LANES - SUBLANES, L), F32)], axis=0)
    x1, x2, x3 = _split3(xr)
    sel = sel_ref[...]
    cb_all = _dot_tn(x1, sel) + _dot_tn(x2, sel) + _dot_tn(x3, sel)
    src = lax.broadcasted_iota(jnp.int32, (L, L), 0)
    tgt = lax.broadcasted_iota(jnp.int32, (L, L), 1)
    causal = src <= tgt
    ones = jnp.ones((LANES, L), BF16)

    for h in range(ML_HEADS):
        fh = ML_HEADS + h
        cb = cb_all[:, h * L:(h + 1) * L]
        bt = bcum_r[fh:fh + 1, :]
        li_r = gr[h:h + 1, :]
        m_prev = m_ref[h:h + 1, 0:1]
        qt = qt_ref[h * ML_QK:(h + 1) * ML_QK, :]
        k = k_ref[:, h * ML_QK:(h + 1) * ML_QK]
        vaug = jnp.concatenate([vt_ref[h * ML_V:(h + 1) * ML_V, :], ones], axis=0)
        c_st = c_ref[h]

        dmat = jnp.where(causal, cb + bt, NEG_BIG)
        inter = bt + m_prev
        m_t = jnp.maximum(inter, jnp.max(dmat, axis=0, keepdims=True))
        w_intra = jnp.exp(dmat - m_t)
        w_inter = jnp.exp(inter - m_t)
        pt = (_dot(k, qt) * w_intra).astype(BF16)
        nd = _dot(vaug, pt) + w_inter * _dot(c_st.astype(BF16), qt)
        den = jnp.maximum(jnp.abs(nd[ML_V:ML_V + 1]), jnp.exp(-m_t))
        hc = nd[:ML_V] / den

        b_last = bt[:, L - 1:L]
        m_new = jnp.maximum(b_last + m_prev, jnp.max(b_last - bt + li_r, axis=-1, keepdims=True))
        ws = jnp.exp(cb[:, :ML_QK] + (b_last - m_new))
        wc = jnp.exp(b_last + m_prev - m_new)
        kw = (k.astype(F32) * ws).astype(BF16)
        c_ref[h] = wc * c_st + _dot(vaug, kw)
        m_ref[h:h + 1, :] = jnp.broadcast_to(m_new, (1, LANES))

        rows = slice(h * ML_V, (h + 1) * ML_V)
        hs = hc * lax.rsqrt(jnp.mean(hc * hc, axis=0, keepdims=True) + EPS) * hnorm_ref[rows, :]
        o_ref[rows, :] = (_sigmoid(opt_ref[rows, :].astype(F32)) * hs).astype(BF16)


def _mlstm_consts(n):
    r = np.arange(n)
    triu = (r[:, None] <= r[None, :]).astype(np.float32)
    sel = np.zeros((LANES, ML_HEADS * n), np.float32)
    for h in range(ML_HEADS):
        sel[h, h * n:(h + 1) * n] = 1.0
        sel[ML_HEADS + h, h * n:(h + 1) * n] = -1.0
    return jnp.asarray(triu, BF16), jnp.asarray(sel, BF16)


def _mlstm_layer(x, bsz, seq, g_mix, p, g_ffn, w1, w2):
    t, d = x.shape
    hq, hv = ML_HEADS * ML_QK, ML_HEADS * ML_V
    row = lambda i: (i, 0)
    col = lambda i: (0, i)
    qt, k, vt, opt, grow = pl.pallas_call(
        _mlstm_proj_kernel,
        grid=(t // TOK_TILE,),
        in_specs=[pl.BlockSpec((TOK_TILE, d), row), _const_spec((1, d)), _const_spec((hq + 2 * hv, d)),
                  _const_spec((d, hq)), _const_spec((SUBLANES, d)), _const_spec((SUBLANES, d)),
                  _const_spec((SUBLANES, 1))],
        out_specs=[pl.BlockSpec((hq, TOK_TILE), col), pl.BlockSpec((TOK_TILE, hq), row),
                   pl.BlockSpec((hv, TOK_TILE), col), pl.BlockSpec((hv, TOK_TILE), col),
                   pl.BlockSpec((SUBLANES, TOK_TILE), col)],
        out_shape=[jax.ShapeDtypeStruct((hq, t), BF16), jax.ShapeDtypeStruct((t, hq), BF16),
                   jax.ShapeDtypeStruct((hv, t), BF16), jax.ShapeDtypeStruct((hv, t), BF16),
                   jax.ShapeDtypeStruct((SUBLANES, t), F32)],
        compiler_params=_cparams("parallel"),
        name="mlstm_proj",
    )(x, g_mix, p["w_t"], p["w_k"], p["wr_hi"], p["wr_lo"], p["b_row"])

    nc = seq // REC_TILE
    triu, sel = _mlstm_consts(REC_TILE)
    hnorm = jnp.broadcast_to(p["head_norm"].reshape(hv, 1), (hv, REC_TILE))
    rblk = lambda b, c: (b * nc + c, 0)
    cblk = lambda b, c: (0, b * nc + c)
    at = pl.pallas_call(
        _mlstm_rec_kernel,
        grid=(bsz, nc),
        in_specs=[pl.BlockSpec((hq, REC_TILE), cblk), pl.BlockSpec((REC_TILE, hq), rblk),
                  pl.BlockSpec((hv, REC_TILE), cblk), pl.BlockSpec((hv, REC_TILE), cblk),
                  pl.BlockSpec((SUBLANES, REC_TILE), cblk),
                  _const_spec((hv, REC_TILE)), _const_spec((REC_TILE, REC_TILE)),
                  _const_spec((LANES, ML_HEADS * REC_TILE))],
        out_specs=pl.BlockSpec((hv, REC_TILE), cblk),
        out_shape=jax.ShapeDtypeStruct((hv, t), BF16),
        scratch_shapes=[pltpu.VMEM((ML_HEADS, ML_V + LANES, ML_QK), F32), pltpu.VMEM((SUBLANES, LANES), F32)],
        compiler_params=_cparams("parallel", "arbitrary"),
        name="mlstm_rec",
    )(qt, k, vt, opt, grow, hnorm, triu, sel)
    return _mlp_call(x, at, p["w_o"], None, g_ffn, w1, w2, a_transposed=True)


def _mlstm_params(w_in, w_if, b_if, head_norm, w_o):
    ng = 2 * ML_HEADS
    hq = ML_HEADS * ML_QK
    w_row = w_if.T
    wr_hi = w_row.astype(BF16)
    wr_lo = (w_row - wr_hi.astype(F32)).astype(BF16)
    w_t = jnp.concatenate([w_in[:, :hq], w_in[:, 2 * hq:]], axis=1).T
    return dict(w_t=w_t.astype(BF16), w_k=w_in[:, hq:2 * hq].astype(BF16),
                wr_hi=wr_hi, wr_lo=wr_lo, b_row=b_if.reshape(ng, 1),
                head_norm=head_norm, w_o=w_o.astype(BF16))


def _gla_proj_kernel(x_ref, g_ref, win_ref, wa1_ref, wa2_ref, ba_ref, q_ref, k_ref, v_ref, r_ref, la_ref):
    hn = _rms_rows(x_ref[...], g_ref[...]).astype(BF16)
    y = _dot(hn, win_ref[...])
    hk, hv = GLA_HEADS * GLA_K, GLA_HEADS * GLA_V
    q_ref[...] = (y[:, :hk] * (GLA_K ** -0.5)).astype(BF16)
    k_ref[...] = y[:, hk:2 * hk].astype(BF16)
    v_ref[...] = y[:, 2 * hk:2 * hk + hv].astype(BF16)
    r_ref[...] = y[:, 2 * hk + hv:].astype(BF16)
    z = _dot(_dot(hn, wa1_ref[...]).astype(BF16), wa2_ref[...]) + ba_ref[...]
    la_ref[...] = _log_sigmoid(z) * (1.0 / GLA_TAU)


def _gla_levels():
    return [2 ** j for j in range(1, int(math.log2(REC_TILE)) + 1)]


def _gla_consts():
    n = REC_TILE
    t = np.arange(n)[:, None]
    u = np.arange(n)[None, :]
    mats = []
    for p in _gla_levels():
        first_upper = (t // p) * p + p // 2
        upper = (t % p) >= p // 2
        m_up = (u > first_upper) & (u <= t)
        m_lo = (u > t) & (u <= first_upper)
        mats.append(np.where(upper, m_up, m_lo))
    mats.append(u <= t)
    mats.append(u > t)
    return jnp.asarray(np.stack(mats).astype(np.float32), BF16)


def _gla_rec_kernel(q_ref, k_ref, v_ref, r_ref, la_ref, hnorm_ref, w_ref, o_ref, c_ref):
    L = REC_TILE
    levels = _gla_levels()
    nl = len(levels)

    @pl.when(pl.program_id(1) == 0)
    def _():
        c_ref[...] = jnp.zeros_like(c_ref)

    la_hi, la_lo = _split2(la_ref[...])

    def decay(j, two_term):
        w = w_ref[j]
        x = _dot(w, la_hi)
        if two_term:
            x = x + _dot(w, la_lo)
        return jnp.exp(x)

    row = lax.broadcasted_iota(jnp.int32, (L, L), 0)
    col = lax.broadcasted_iota(jnp.int32, (L, L), 1)
    rowk = lax.broadcasted_iota(jnp.int32, (L, GLA_K), 0)
    e_cum = decay(nl, True)
    e_rev = decay(nl + 1, True)
    e_lvl = [decay(j, False) for j in range(nl)]

    for h in range(GLA_HEADS):
        ks = slice(h * GLA_K, (h + 1) * GLA_K)
        vs = slice(h * GLA_V, (h + 1) * GLA_V)
        q = q_ref[:, ks].astype(F32)
        k = k_ref[:, ks].astype(F32)
        v = v_ref[:, vs]
        att = jnp.where(row == col, _dot_nt(q_ref[:, ks], k_ref[:, ks]), 0.0)
        for j, p in enumerate(levels):
            upper = (rowk & (p - 1)) >= (p // 2)
            e = e_lvl[j][:, ks]
            qf = jnp.where(upper, q * e, 0.0).astype(BF16)
            kf = jnp.where(upper, 0.0, k * e).astype(BF16)
            a = _dot_nt(qf, kf)
            if p < L:
                shift = int(math.log2(p))
                a = jnp.where((row >> shift) == (col >> shift), a, 0.0)
            att = att + a
        c_st = c_ref[h]
        o = _dot(att.astype(BF16), v) + _dot_nt((q * e_cum[:, ks]).astype(BF16), c_st.astype(BF16))
        c_ref[h] = e_cum[L - 1:L, ks] * c_st + _dot_tn(v, (k * e_rev[:, ks]).astype(BF16))
        o = _rms_rows(o, hnorm_ref[:, vs])
        rr = r_ref[:, vs].astype(F32)
        o_ref[:, vs] = (o * rr * _sigmoid(rr)).astype(BF16)


def _gla_layer(x, bsz, seq, g_mix, p, g_ffn, w1, w2):
    t, d = x.shape
    hk, hv = GLA_HEADS * GLA_K, GLA_HEADS * GLA_V
    row = lambda i: (i, 0)
    q, k, v, r, la = pl.pallas_call(
        _gla_proj_kernel,
        grid=(t // TOK_TILE,),
        in_specs=[pl.BlockSpec((TOK_TILE, d), row), _const_spec((1, d)), _const_spec((d, 2 * hk + 2 * hv)),
                  _const_spec((d, LANES)), _const_spec((LANES, hk)), _const_spec((1, hk))],
        out_specs=[pl.BlockSpec((TOK_TILE, hk), row), pl.BlockSpec((TOK_TILE, hk), row),
                   pl.BlockSpec((TOK_TILE, hv), row), pl.BlockSpec((TOK_TILE, hv), row),
                   pl.BlockSpec((TOK_TILE, hk), row)],
        out_shape=[jax.ShapeDtypeStruct((t, hk), BF16), jax.ShapeDtypeStruct((t, hk), BF16),
                   jax.ShapeDtypeStruct((t, hv), BF16), jax.ShapeDtypeStruct((t, hv), BF16),
                   jax.ShapeDtypeStruct((t, hk), F32)],
        compiler_params=_cparams("parallel"),
        name="gla_proj",
    )(x, g_mix, p["w_in"], p["w_a1"], p["w_a2"], p["b_a"])

    nc = seq // REC_TILE
    wmats = _gla_consts()
    blk = lambda b, c: (b * nc + c, 0)
    a = pl.pallas_call(
        _gla_rec_kernel,
        grid=(bsz, nc),
        in_specs=[pl.BlockSpec((REC_TILE, hk), blk), pl.BlockSpec((REC_TILE, hk), blk),
                  pl.BlockSpec((REC_TILE, hv), blk), pl.BlockSpec((REC_TILE, hv), blk),
                  pl.BlockSpec((REC_TILE, hk), blk), _const_spec((1, hv)), _const_spec(wmats.shape)],
        out_specs=pl.BlockSpec((REC_TILE, hv), blk),
        out_shape=jax.ShapeDtypeStruct((t, hv), BF16),
        scratch_shapes=[pltpu.VMEM((GLA_HEADS, GLA_V, GLA_K), F32)],
        compiler_params=_cparams("parallel", "arbitrary"),
        name="gla_rec",
    )(q, k, v, r, la, p["head_norm"], wmats)
    return _mlp_call(x, a, p["w_o"], None, g_ffn, w1, w2)


def _gla_params(w_in, w_a1, w_a2, b_a, head_norm, w_o):
    return dict(w_in=w_in.astype(BF16),
                w_a1=jnp.pad(w_a1, ((0, 0), (0, LANES - GLA_GATE_RANK))).astype(BF16),
                w_a2=jnp.pad(w_a2, ((0, LANES - GLA_GATE_RANK), (0, 0))).astype(BF16),
                b_a=b_a.reshape(1, -1), head_norm=head_norm.reshape(1, -1), w_o=w_o.astype(BF16))


def _conv_kernel(x_ref, g_ref, w1_ref, b1_ref, wdw_ref, lng_ref, lnb_ref, o_ref, u_ref, c_ref):
    tm = TOK_TILE
    d = D_MODEL
    nbuf = CONV_HALO + tm

    @pl.when(pl.program_id(1) == 0)
    def _():
        u_ref[0, :, 0:CONV_HALO, :] = jnp.zeros((d // LANES, CONV_HALO, LANES), F32)

    @pl.when(pl.program_id(1) != 0)
    def _():
        u_ref[0, :, 0:CONV_HALO, :] = u_ref[0, :, tm:tm + CONV_HALO, :]

    hn = _rms_rows(x_ref[...], g_ref[...]).astype(BF16)
    y = _dot(hn, w1_ref[...]) + b1_ref[...]
    u = y[:, :d] * _sigmoid(y[:, d:])
    lead = CONV_HALO - (CONV_WIDTH - 1)
    groups = CONV_ROWS // SUBLANES
    tiles = nbuf // SUBLANES
    sub = lax.broadcasted_iota(jnp.int32, (tiles, SUBLANES, LANES), 1)

    for cb in range(d // LANES):
        cols = slice(cb * LANES, (cb + 1) * LANES)
        u_ref[0, cb, CONV_HALO:, :] = u[:, cols]
        full = u_ref[0, cb].reshape(tiles, SUBLANES, LANES)
        for s in range(1, SUBLANES):
            rot = pltpu.roll(full, SUBLANES - s, 1)
            nxt = jnp.concatenate([rot[1:], rot[:1]], axis=0)
            u_ref[s, cb] = jnp.where(sub < SUBLANES - s, rot, nxt).reshape(nbuf, LANES)

        def taps(c, carry):
            r0 = pl.multiple_of(c * CONV_ROWS, CONV_ROWS)
            acc = jnp.broadcast_to(wdw_ref[CONV_WIDTH, :, cols][None], (groups, SUBLANES, LANES))
            for j in range(CONV_WIDTH):
                off = lead + j
                start = pl.multiple_of(r0 + (off // SUBLANES) * SUBLANES, SUBLANES)
                uj = u_ref[off % SUBLANES, cb, pl.ds(start, CONV_ROWS), :].reshape(groups, SUBLANES, LANES)
                acc = acc + uj * wdw_ref[j, :, cols][None]
            c_ref[pl.ds(r0, CONV_ROWS), cols] = acc.reshape(CONV_ROWS, LANES)
            return carry

        lax.fori_loop(0, tm // CONV_ROWS, taps, 0)

    def norm(c, carry):
        r0 = pl.multiple_of(c * CONV_NORM_ROWS, CONV_NORM_ROWS)
        acc = c_ref[pl.ds(r0, CONV_NORM_ROWS), :]
        mu = jnp.mean(acc, axis=-1, keepdims=True)
        cen = acc - mu
        var = jnp.mean(cen * cen, axis=-1, keepdims=True)
        z = cen * lax.rsqrt(var + EPS) * lng_ref[...] + lnb_ref[...]
        o_ref[pl.ds(r0, CONV_NORM_ROWS), :] = (z * _sigmoid(z)).astype(BF16)
        return carry

    lax.fori_loop(0, tm // CONV_NORM_ROWS, norm, 0, unroll=4)


def _conv_layer(x, bsz, seq, g_mix, p, g_ffn, w1, w2):
    t, d = x.shape
    nt = seq // TOK_TILE
    blk = lambda b, i: (b * nt + i, 0)
    c = pl.pallas_call(
        _conv_kernel,
        grid=(bsz, nt),
        in_specs=[pl.BlockSpec((TOK_TILE, d), blk), _const_spec((1, d)), _const_spec((d, 2 * d)),
                  _const_spec((1, 2 * d)), _const_spec((CONV_WIDTH + 1, SUBLANES, d)),
                  _const_spec((1, d)), _const_spec((1, d))],
        out_specs=pl.BlockSpec((TOK_TILE, d), blk),
        out_shape=jax.ShapeDtypeStruct((t, d), BF16),
        scratch_shapes=[pltpu.VMEM((SUBLANES, d // LANES, CONV_HALO + TOK_TILE, LANES), F32),
                        pltpu.VMEM((TOK_TILE, d), F32)],
        compiler_params=_cparams("parallel", "arbitrary"),
        name="conv",
    )(x, g_mix, p["w_pw1"], p["b_pw1"], p["w_dw"], p["ln_g"], p["ln_b"])
    return _mlp_call(x, c, p["w_pw2"], p["b_pw2"], g_ffn, w1, w2)


def _conv_params(w_pw1, b_pw1, w_dw, b_dw, ln_g, ln_b, w_pw2, b_pw2):
    return dict(w_pw1=w_pw1.astype(BF16), b_pw1=b_pw1.reshape(1, -1),
                w_dw=jnp.broadcast_to(jnp.concatenate([w_dw, b_dw[None]], axis=0)[:, None, :],
                                      (CONV_WIDTH + 1, SUBLANES, w_dw.shape[1])),
                ln_g=ln_g.reshape(1, -1), ln_b=ln_b.reshape(1, -1),
                w_pw2=w_pw2.astype(BF16), b_pw2=b_pw2.reshape(1, -1))


def kernel(x, norm_mix, norm_ffn, mla_w_dq, mla_q_norm, mla_w_uq, mla_w_dkv, mla_kv_norm, mla_w_ukv, mla_q_gain, mla_k_gain, mla_w_o, mlstm_w_in, mlstm_w_if, mlstm_b_if, mlstm_head_norm, mlstm_w_o, gla_w_in, gla_w_a1, gla_w_a2, gla_b_a, gla_head_norm, gla_w_o, conv_w_pw1, conv_b_pw1, conv_w_dw, conv_b_dw, conv_ln_g, conv_ln_b, conv_w_pw2, conv_b_pw2, ffn_w1, ffn_w2):
    bsz, seq, d = x.shape
    depth = norm_mix.shape[0]
    assert d == D_MODEL and seq % ATT_Q_TILE == 0 and seq % TOK_TILE == 0 and seq % REC_TILE == 0
    h = x.reshape(bsz * seq, d)
    for i in range(depth):
        kind, j = i % 4, i // 4
        g_mix = norm_mix[i].reshape(1, d)
        g_ffn = norm_ffn[i].reshape(1, d)
        w1 = ffn_w1[i].astype(BF16)
        w2 = ffn_w2[i].astype(BF16)
        if kind == 0:
            p = _mla_params(seq, mla_w_dq[j], mla_q_norm[j], mla_w_uq[j], mla_w_dkv[j], mla_kv_norm[j],
                            mla_w_ukv[j], mla_q_gain[j], mla_k_gain[j], mla_w_o[j])
            h = _mla_layer(h, bsz, seq, g_mix, p, g_ffn, w1, w2)
        elif kind == 1:
            p = _mlstm_params(mlstm_w_in[j], mlstm_w_if[j], mlstm_b_if[j], mlstm_head_norm[j], mlstm_w_o[j])
            h = _mlstm_layer(h, bsz, seq, g_mix, p, g_ffn, w1, w2)
        elif kind == 2:
            p = _gla_params(gla_w_in[j], gla_w_a1[j], gla_w_a2[j], gla_b_a[j], gla_head_norm[j], gla_w_o[j])
            h = _gla_layer(h, bsz, seq, g_mix, p, g_ffn, w1, w2)
        else:
            p = _conv_params(conv_w_pw1[j], conv_b_pw1[j], conv_w_dw[j], conv_b_dw[j], conv_ln_g[j],
                             conv_ln_b[j], conv_w_pw2[j], conv_b_pw2[j])
            h = _conv_layer(h, bsz, seq, g_mix, p, g_ffn, w1, w2)
    return h.reshape(bsz, seq, d)
```

```python
import functools
import math

import numpy as np
import jax
import jax.numpy as jnp
from jax import lax
from jax.experimental import pallas as pl
from jax.experimental.pallas import tpu as pltpu

F32 = jnp.float32
BF16 = jnp.bfloat16

D_MODEL = 1024
D_FF = 4 * D_MODEL
EPS = 1e-6
CHUNK = 64

MLA_HEADS = 16
MLA_NOPE = 64
MLA_ROPE = 32
MLA_QK = MLA_NOPE + MLA_ROPE
MLA_V = 64
MLA_Q_RANK = 384
MLA_KV_RANK = 256
ROPE_BASE = 10000.0
MLA_VROWS = MLA_V + 16

ML_HEADS = 4
ML_QK = D_MODEL // 8
ML_V = D_MODEL // 4
GATE_CAP = 15.0

GLA_HEADS = 4
GLA_K = D_MODEL // 8
GLA_V = D_MODEL // 4
GLA_GATE_RANK = 16
GLA_TAU = 16.0

CONV_WIDTH = 31

LANES = 128
SUBLANES = 8
VMEM_LIMIT = 56 * 1024 * 1024

TOK_TILE = 512
PROJ_TILE = 1024
FF_TILE = 1024
ATT_TILE = 512
ATT_Q_TILE = 1024
ATT_HEADS = 4
REC_TILE = 256
CONV_HALO = 32
CONV_ROWS = 128
CONV_NORM_ROWS = 32
NEG_BIG = -1e30


def _cparams(*sem):
    return pltpu.CompilerParams(dimension_semantics=sem, vmem_limit_bytes=VMEM_LIMIT)


def _const_spec(shape):
    nd = len(shape)
    return pl.BlockSpec(shape, lambda *_: (0,) * nd, pipeline_mode=pl.Buffered(1))


def _dot(a, b):
    return jnp.dot(a, b, preferred_element_type=F32)


def _dot_nt(a, b):
    return lax.dot_general(a, b, (((1,), (1,)), ((), ())), preferred_element_type=F32)


def _dot_tn(a, b):
    return lax.dot_general(a, b, (((0,), (0,)), ((), ())), preferred_element_type=F32)


def _split2(a):
    hi = a.astype(BF16)
    lo = (a - hi.astype(F32)).astype(BF16)
    return hi, lo


def _split3(a):
    hi = a.astype(BF16)
    r = a - hi.astype(F32)
    mid = r.astype(BF16)
    lo = (r - mid.astype(F32)).astype(BF16)
    return hi, mid, lo


def _rms_rows(x, g):
    return x * lax.rsqrt(jnp.mean(x * x, axis=-1, keepdims=True) + EPS) * g


def _log_sigmoid(z):
    return jnp.minimum(z, 0.0) - jnp.log1p(jnp.exp(-jnp.abs(z)))


def _sigmoid(z):
    return 1.0 / (1.0 + jnp.exp(-z))


def _mlp_kernel(*refs, has_bias, a_transposed):
    if has_bias:
        x_ref, a_ref, wo_ref, bo_ref, g_ref, w1_ref, w2_ref, o_ref = refs
    else:
        x_ref, a_ref, wo_ref, g_ref, w1_ref, w2_ref, o_ref = refs
    if a_transposed:
        x1 = x_ref[...] + _dot_tn(a_ref[...], wo_ref[...])
    else:
        x1 = x_ref[...] + _dot(a_ref[...], wo_ref[...])
    if has_bias:
        x1 = x1 + bo_ref[...]
    hn = _rms_rows(x1, g_ref[...]).astype(BF16)
    acc = x1
    for c in range(D_FF // FF_TILE):
        h = _dot(hn, w1_ref[:, c * FF_TILE:(c + 1) * FF_TILE])
        h = jnp.maximum(h, 0.0)
        acc = acc + _dot((h * h).astype(BF16), w2_ref[c * FF_TILE:(c + 1) * FF_TILE, :])
    o_ref[...] = acc


def _mlp_call(x, a, w_o, b_o, g, w1, w2, a_transposed=False):
    t, d = x.shape
    din = w_o.shape[0]
    has_bias = b_o is not None
    row = lambda i: (i, 0)
    a_spec = pl.BlockSpec((din, TOK_TILE), lambda i: (0, i)) if a_transposed else pl.BlockSpec((TOK_TILE, din), row)
    in_specs = [pl.BlockSpec((TOK_TILE, d), row), a_spec, _const_spec((din, d))]
    args = [x, a, w_o]
    if has_bias:
        in_specs.append(_const_spec((1, d)))
        args.append(b_o)
    in_specs += [_const_spec((1, d)), _const_spec((d, D_FF)), _const_spec((D_FF, d))]
    args += [g, w1, w2]
    return pl.pallas_call(
        functools.partial(_mlp_kernel, has_bias=has_bias, a_transposed=a_transposed),
        grid=(t // TOK_TILE,),
        in_specs=in_specs,
        out_specs=pl.BlockSpec((TOK_TILE, d), row),
        out_shape=jax.ShapeDtypeStruct((t, d), F32),
        compiler_params=_cparams("parallel"),
        name="mlp",
    )(*args)


def _mla_proj_kernel(x_ref, g_ref, wdq_ref, qn_ref, wuqt_ref, wdkv_ref, kvn_ref, wukt_ref, wuvt_ref,
                     vone_ref, qc_ref, qsa_ref, qsb_ref, kc_ref, ksa_ref, ksb_ref,
                     qt_ref, k_ref, vt_ref):
    half = MLA_ROPE // 2
    hn = _rms_rows(x_ref[...], g_ref[...]).astype(BF16)
    cq = _rms_rows(_dot(hn, wdq_ref[...]), qn_ref[...]).astype(BF16)
    dkv = _dot(hn, wdkv_ref[...])
    ckv = _rms_rows(dkv[:, :MLA_KV_RANK], kvn_ref[...]).astype(BF16)

    vt_ref[0] = (_dot_nt(wuvt_ref[...], ckv) + vone_ref[...]).astype(BF16)

    n0, a0, b0, e0 = 0, MLA_NOPE, MLA_NOPE + half, MLA_QK

    def ssq(t):
        return jnp.sum(t * t, axis=0, keepdims=True)

    def rope(a, b, c, sa, sb):
        return a * c[a0:b0] + b * sa[a0:b0], b * c[b0:e0] + a * sb[b0:e0]

    def inv_rms(s):
        return lax.rsqrt(s * (1.0 / MLA_QK) + EPS)

    tm = x_ref.shape[0]
    pad_rows = jnp.zeros((LANES - MLA_QK, tm), F32)

    qt = _dot_nt(wuqt_ref[...], cq)
    qc, qsa, qsb = qc_ref[...], qsa_ref[...], qsb_ref[...]
    for h in range(MLA_HEADS):
        t = qt[h * MLA_QK:(h + 1) * MLA_QK]
        n, a, b = t[n0:a0], t[a0:b0], t[b0:e0]
        r = inv_rms(ssq(n) + ssq(a) + ssq(b))
        ra, rb = rope(a, b, qc, qsa, qsb)
        qt_ref[h * LANES:h * LANES + MLA_QK, :] = jnp.concatenate(
            [n * qc[n0:a0] * r, ra * r, rb * r], axis=0).astype(BF16)
        qt_ref[h * LANES + MLA_QK:(h + 1) * LANES, :] = pad_rows.astype(BF16)

    krt = dkv[:, MLA_KV_RANK:].T
    kt = _dot_nt(wukt_ref[...], ckv)
    kc, ksa, ksb = kc_ref[...], ksa_ref[...], ksb_ref[...]
    ka, kb = krt[a0:b0], krt[b0:e0]
    ssq_rope = ssq(ka) + ssq(kb)
    ra, rb = rope(ka, kb, kc, ksa, ksb)
    for h in range(MLA_HEADS):
        n = kt[h * MLA_NOPE:(h + 1) * MLA_NOPE]
        r = inv_rms(ssq(n) + ssq_rope)
        kh = jnp.concatenate([n * kc[n0:a0] * r, ra * r, rb * r, pad_rows], axis=0)
        k_ref[:, h * LANES:(h + 1) * LANES] = kh.T.astype(BF16)


def _attn_kernel(qt_ref, k_ref, vt_ref, o_ref, s_ref, acc_ref, m_ref, mb_ref):
    i = pl.program_id(2)
    tq, tk = ATT_Q_TILE, ATT_TILE
    nh = ATT_HEADS
    kpq = tq // tk
    assert kpq == 2
    vrows = vt_ref.shape[1] // nh
    krow = lax.broadcasted_iota(jnp.int32, (tk, tq), 0)
    qcol = lax.broadcasted_iota(jnp.int32, (tk, tq), 1)
    diag_mask = (krow // CHUNK) <= (qcol // CHUNK)
    late = slice(tk, tq)

    def produce(h, kb, qcols=slice(None), mask=None):
        start = pl.multiple_of(kb * tk, tk)
        st = _dot(k_ref[pl.ds(start, tk), h * LANES:(h + 1) * LANES], qt_ref[h * LANES:(h + 1) * LANES, qcols])
        if mask is not None:
            st = jnp.where(mask, st, NEG_BIG)
        s_ref[h, :, qcols] = st
        mb_ref[h, :, qcols] = jnp.max(st, axis=0, keepdims=True)

    def consume(h, kb, mask=None, qcols=slice(None)):
        st = s_ref[h, :, qcols]
        if mask is None:
            mblk = mb_ref[h, :, qcols]
        else:
            st = jnp.where(mask, st, NEG_BIG)
            mblk = jnp.max(st, axis=0, keepdims=True)
        m = m_ref[h, :, qcols]
        m_new = jnp.maximum(m, mblk)
        m_ref[h, :, qcols] = m_new
        pt = jnp.exp2(st - m_new).astype(BF16)
        acc_ref[h, :, qcols] = (jnp.exp2(m - m_new) * acc_ref[h, :, qcols]
                                + _dot(vt_ref[kb, h * vrows:(h + 1) * vrows, :], pt))

    def body(j, carry):
        for h in range(nh):
            produce((h + 1) % nh, j + (h + 1) // nh)
            consume(h, j)
        return carry

    m_ref[...] = jnp.full(m_ref.shape, NEG_BIG, F32)
    acc_ref[...] = jnp.zeros(acc_ref.shape, F32)
    first = kpq * i
    produce(0, 0)
    lax.fori_loop(0, i, lambda t, c: body(kpq * t + 1, body(kpq * t, c)), 0)
    late_mask = diag_mask[:, :tk]
    for h in range(nh):
        if h + 1 < nh:
            produce(h + 1, first, mask=diag_mask)
        else:
            produce(0, first + 1, late, late_mask)
        consume(h, first, diag_mask if h == 0 else None)
    for h in range(nh):
        if h + 1 < nh:
            produce(h + 1, first + 1, late, late_mask)
        consume(h, first + 1, None, late)
        acc = acc_ref[h]
        o_ref[h * MLA_V:(h + 1) * MLA_V, :] = (acc[:MLA_V] / acc[MLA_V:MLA_V + 1]).astype(BF16)


def _mla_layer(x, bsz, seq, g_mix, p, g_ffn, w1, w2):
    t, d = x.shape
    assert TOK_TILE == ATT_TILE
    hp = MLA_HEADS * LANES
    hv = MLA_HEADS * MLA_VROWS
    nt = seq // TOK_TILE
    row = lambda i: (i, 0)
    ttab = lambda i: (i % nt, 0)
    ftab = lambda i: (0, i % nt)
    qt, k, vt = pl.pallas_call(
        _mla_proj_kernel,
        grid=(t // TOK_TILE,),
        in_specs=[pl.BlockSpec((TOK_TILE, d), row), _const_spec((1, d)),
                  _const_spec((d, MLA_Q_RANK)), _const_spec((1, MLA_Q_RANK)),
                  _const_spec((MLA_HEADS * MLA_QK, MLA_Q_RANK)),
                  _const_spec((d, MLA_KV_RANK + LANES)), _const_spec((1, MLA_KV_RANK)),
                  _const_spec((MLA_HEADS * MLA_NOPE, MLA_KV_RANK)), _const_spec((hv, MLA_KV_RANK)),
                  _const_spec((hv, TOK_TILE))] + [pl.BlockSpec((LANES, TOK_TILE), ftab)] * 6,
        out_specs=[pl.BlockSpec((hp, TOK_TILE), lambda i: (0, i)), pl.BlockSpec((TOK_TILE, hp), row),
                   pl.BlockSpec((1, hv, TOK_TILE), lambda i: (i, 0, 0))],
        out_shape=[jax.ShapeDtypeStruct((hp, t), BF16), jax.ShapeDtypeStruct((t, hp), BF16),
                   jax.ShapeDtypeStruct((t // TOK_TILE, hv, TOK_TILE), BF16)],
        compiler_params=_cparams("parallel"),
        name="mla_proj",
    )(x, g_mix, p["w_dq"], p["q_norm"], p["w_uq_t"], p["w_dkv"], p["kv_norm"], p["w_uk_t"], p["w_uv_t"],
      p["vone"], p["qc"], p["qsa"], p["qsb"], p["kc"], p["ksa"], p["ksb"])

    nq = seq // ATT_Q_TILE
    ot = pl.pallas_call(
        _attn_kernel,
        grid=(bsz, MLA_HEADS // ATT_HEADS, nq),
        in_specs=[pl.BlockSpec((ATT_HEADS * LANES, ATT_Q_TILE), lambda b, h, i: (h, b * nq + i)),
                  pl.BlockSpec((seq, ATT_HEADS * LANES), lambda b, h, i: (b, h)),
                  pl.BlockSpec((seq // ATT_TILE, ATT_HEADS * MLA_VROWS, ATT_TILE), lambda b, h, i: (b, h, 0))],
        out_specs=pl.BlockSpec((ATT_HEADS * MLA_V, ATT_Q_TILE), lambda b, h, i: (h, b * nq + i)),
        out_shape=jax.ShapeDtypeStruct((MLA_HEADS * MLA_V, t), BF16),
        scratch_shapes=[pltpu.VMEM((ATT_HEADS, ATT_TILE, ATT_Q_TILE), F32),
                        pltpu.VMEM((ATT_HEADS, MLA_VROWS, ATT_Q_TILE), F32),
                        pltpu.VMEM((ATT_HEADS, 1, ATT_Q_TILE), F32), pltpu.VMEM((ATT_HEADS, 1, ATT_Q_TILE), F32)],
        compiler_params=_cparams("parallel", "parallel", "arbitrary"),
        name="mla_attn",
    )(qt, k, vt)
    return _mlp_call(x, ot, p["w_o"], None, g_ffn, w1, w2, a_transposed=True)


def _mla_params(seq, w_dq, q_norm, w_uq, w_dkv, kv_norm, w_ukv, q_gain, k_gain, w_o):
    pad_head = LANES - MLA_QK
    half = MLA_ROPE // 2
    w_ukv_r = w_ukv.reshape(MLA_KV_RANK, MLA_HEADS, MLA_NOPE + MLA_V)
    w_uv_p = jnp.pad(w_ukv_r[:, :, MLA_NOPE:], ((0, 0), (0, 0), (0, MLA_VROWS - MLA_V)))
    w_rope = jnp.pad(w_dkv[:, MLA_KV_RANK:], ((0, 0), (MLA_NOPE, pad_head)))
    w_dkv_p = jnp.concatenate([w_dkv[:, :MLA_KV_RANK], w_rope], axis=1)
    vone = np.zeros((MLA_HEADS * MLA_VROWS, TOK_TILE), np.float32)
    vone[np.arange(MLA_HEADS) * MLA_VROWS + MLA_V] = 1.0
    gq = q_gain * ((MLA_QK ** -0.5) * math.log2(math.e))
    inv_freq = ROPE_BASE ** (-jnp.arange(half, dtype=F32) / half)
    ang = jnp.arange(seq).astype(F32)[:, None] * inv_freq[None, :]
    cos, sin = jnp.cos(ang), jnp.sin(ang)
    g1 = lambda g: g[MLA_NOPE:MLA_NOPE + half][None, :]
    g2 = lambda g: g[MLA_NOPE + half:MLA_QK][None, :]
    z = lambda n: jnp.zeros((seq, n), F32)

    def tables(g):
        nope = jnp.broadcast_to(g[None, :MLA_NOPE], (seq, MLA_NOPE))
        c = jnp.concatenate([nope, g1(g) * cos, g2(g) * cos, z(pad_head)], axis=1)
        sa = jnp.concatenate([z(MLA_NOPE), -g2(g) * sin, z(half + pad_head)], axis=1)
        sb = jnp.concatenate([z(MLA_NOPE + half), g1(g) * sin, z(pad_head)], axis=1)
        return c.T, sa.T, sb.T

    qc, qsa, qsb = tables(gq)
    kc, ksa, ksb = tables(k_gain)
    return dict(w_dq=w_dq.astype(BF16), q_norm=q_norm.reshape(1, -1),
                w_uq_t=w_uq.T.astype(BF16), w_dkv=w_dkv_p.astype(BF16), kv_norm=kv_norm.reshape(1, -1),
                w_uk_t=w_ukv_r[:, :, :MLA_NOPE].reshape(MLA_KV_RANK, -1).T.astype(BF16),
                w_uv_t=w_uv_p.reshape(MLA_KV_RANK, -1).T.astype(BF16), vone=jnp.asarray(vone, BF16),
                qc=qc, qsa=qsa, qsb=qsb, kc=kc, ksa=ksa, ksb=ksb, w_o=w_o.astype(BF16))


def _mlstm_proj_kernel(x_ref, g_ref, wt_ref, wk_ref, wr_hi_ref, wr_lo_ref, br_ref,
                       qt_ref, k_ref, vt_ref, ot_ref, gr_ref):
    hn = _rms_rows(x_ref[...], g_ref[...])
    hi, lo = _split2(hn)
    hq, hv = ML_HEADS * ML_QK, ML_HEADS * ML_V
    yt = _dot_nt(wt_ref[...], hi)
    qt_ref[...] = yt[:hq].astype(BF16)
    vt_ref[...] = yt[hq:hq + hv].astype(BF16)
    ot_ref[...] = yt[hq + hv:].astype(BF16)
    k_ref[...] = (_dot(hi, wk_ref[...]) * (ML_QK ** -0.5)).astype(BF16)
    gr_ref[...] = (_dot_nt(wr_hi_ref[...], hi) + _dot_nt(wr_hi_ref[...], lo) + _dot_nt(wr_lo_ref[...], hi)) + br_ref[...]


def _cap(g):
    return GATE_CAP * jnp.tanh(g * (1.0 / GATE_CAP))


def _mlstm_rec_kernel(qt_ref, k_ref, vt_ref, opt_ref, gr_ref, hnorm_ref, triu_ref, sel_ref,
                      o_ref, c_ref, m_ref):
    L = REC_TILE

    @pl.when(pl.program_id(1) == 0)
    def _():
        c_ref[...] = jnp.zeros_like(c_ref)
        m_ref[...] = jnp.zeros_like(m_ref)

    gr = _cap(gr_ref[...])
    r1, r2, r3 = _split3(_log_sigmoid(gr))
    triu = triu_ref[...]
    bcum_r = _dot(r1, triu) + _dot(r2, triu) + _dot(r3, triu)
    grow = lax.broadcasted_iota(jnp.int32, (SUBLANES, L), 0)
    xr = jnp.concatenate([jnp.where(grow < ML_HEADS, gr, bcum_r), jnp.zeros((LANES - SUBLANES, L), F32)], axis=0)
    x1, x2, x3 = _split3(xr)
    sel = sel_ref[...]
    cb_all = _dot_tn(x1, sel) + _dot_tn(x2, sel) + _dot_tn(x3, sel)
    src = lax.broadcasted_iota(jnp.int32, (L, L), 0)
    tgt = lax.broadcasted_iota(jnp.int32, (L, L), 1)
    causal = src <= tgt
    ones = jnp.ones((LANES, L), BF16)

    for h in range(ML_HEADS):
        fh = ML_HEADS + h
        cb = cb_all[:, h * L:(h + 1) * L]
        bt = bcum_r[fh:fh + 1, :]
        li_r = gr[h:h + 1, :]
        m_prev = m_ref[h:h + 1, 0:1]
        qt = qt_ref[h * ML_QK:(h + 1) * ML_QK, :]
        k = k_ref[:, h * ML_QK:(h + 1) * ML_QK]
        vaug = jnp.concatenate([vt_ref[h * ML_V:(h + 1) * ML_V, :], ones], axis=0)
        c_st = c_ref[h]

        dmat = jnp.where(causal, cb + bt, NEG_BIG)
        inter = bt + m_prev
        m_t = jnp.maximum(inter, jnp.max(dmat, axis=0, keepdims=True))
        w_intra = jnp.exp(dmat - m_t)
        w_inter = jnp.exp(inter - m_t)
        pt = (_dot(k, qt) * w_intra).astype(BF16)
        nd = _dot(vaug, pt) + w_inter * _dot(c_st.astype(BF16), qt)
        den = jnp.maximum(jnp.abs(nd[ML_V:ML_V + 1]), jnp.exp(-m_t))
        hc = nd[:ML_V] / den

        b_last = bt[:, L - 1:L]
        m_new = jnp.maximum(b_last + m_prev, jnp.max(b_last - bt + li_r, axis=-1, keepdims=True))
        ws = jnp.exp(cb[:, :ML_QK] + (b_last - m_new))
        wc = jnp.exp(b_last + m_prev - m_new)
        kw = (k.astype(F32) * ws).astype(BF16)
        c_ref[h] = wc * c_st + _dot(vaug, kw)
        m_ref[h:h + 1, :] = jnp.broadcast_to(m_new, (1, LANES))

        rows = slice(h * ML_V, (h + 1) * ML_V)
        hs = hc * lax.rsqrt(jnp.mean(hc * hc, axis=0, keepdims=True) + EPS) * hnorm_ref[rows, :]
        o_ref[rows, :] = (_sigmoid(opt_ref[rows, :].astype(F32)) * hs).astype(BF16)


def _mlstm_consts(n):
    r = np.arange(n)
    triu = (r[:, None] <= r[None, :]).astype(np.float32)
    sel = np.zeros((LANES, ML_HEADS * n), np.float32)
    for h in range(ML_HEADS):
        sel[h, h * n:(h + 1) * n] = 1.0
        sel[ML_HEADS + h, h * n:(h + 1) * n] = -1.0
    return jnp.asarray(triu, BF16), jnp.asarray(sel, BF16)


def _mlstm_layer(x, bsz, seq, g_mix, p, g_ffn, w1, w2):
    t, d = x.shape
    hq, hv = ML_HEADS * ML_QK, ML_HEADS * ML_V
    row = lambda i: (i, 0)
    col = lambda i: (0, i)
    qt, k, vt, opt, grow = pl.pallas_call(
        _mlstm_proj_kernel,
        grid=(t // PROJ_TILE,),
        in_specs=[pl.BlockSpec((PROJ_TILE, d), row), _const_spec((1, d)), _const_spec((hq + 2 * hv, d)),
                  _const_spec((d, hq)), _const_spec((SUBLANES, d)), _const_spec((SUBLANES, d)),
                  _const_spec((SUBLANES, 1))],
        out_specs=[pl.BlockSpec((hq, PROJ_TILE), col), pl.BlockSpec((PROJ_TILE, hq), row),
                   pl.BlockSpec((hv, PROJ_TILE), col), pl.BlockSpec((hv, PROJ_TILE), col),
                   pl.BlockSpec((SUBLANES, PROJ_TILE), col)],
        out_shape=[jax.ShapeDtypeStruct((hq, t), BF16), jax.ShapeDtypeStruct((t, hq), BF16),
                   jax.ShapeDtypeStruct((hv, t), BF16), jax.ShapeDtypeStruct((hv, t), BF16),
                   jax.ShapeDtypeStruct((SUBLANES, t), F32)],
        compiler_params=_cparams("parallel"),
        name="mlstm_proj",
    )(x, g_mix, p["w_t"], p["w_k"], p["wr_hi"], p["wr_lo"], p["b_row"])

    nc = seq // REC_TILE
    triu, sel = _mlstm_consts(REC_TILE)
    hnorm = jnp.broadcast_to(p["head_norm"].reshape(hv, 1), (hv, REC_TILE))
    rblk = lambda b, c: (b * nc + c, 0)
    cblk = lambda b, c: (0, b * nc + c)
    at = pl.pallas_call(
        _mlstm_rec_kernel,
        grid=(bsz, nc),
        in_specs=[pl.BlockSpec((hq, REC_TILE), cblk), pl.BlockSpec((REC_TILE, hq), rblk),
                  pl.BlockSpec((hv, REC_TILE), cblk), pl.BlockSpec((hv, REC_TILE), cblk),
                  pl.BlockSpec((SUBLANES, REC_TILE), cblk),
                  _const_spec((hv, REC_TILE)), _const_spec((REC_TILE, REC_TILE)),
                  _const_spec((LANES, ML_HEADS * REC_TILE))],
        out_specs=pl.BlockSpec((hv, REC_TILE), cblk),
        out_shape=jax.ShapeDtypeStruct((hv, t), BF16),
        scratch_shapes=[pltpu.VMEM((ML_HEADS, ML_V + LANES, ML_QK), F32), pltpu.VMEM((SUBLANES, LANES), F32)],
        compiler_params=_cparams("parallel", "arbitrary"),
        name="mlstm_rec",
    )(qt, k, vt, opt, grow, hnorm, triu, sel)
    return _mlp_call(x, at, p["w_o"], None, g_ffn, w1, w2, a_transposed=True)


def _mlstm_params(w_in, w_if, b_if, head_norm, w_o):
    ng = 2 * ML_HEADS
    hq = ML_HEADS * ML_QK
    w_row = w_if.T
    wr_hi = w_row.astype(BF16)
    wr_lo = (w_row - wr_hi.astype(F32)).astype(BF16)
    w_t = jnp.concatenate([w_in[:, :hq], w_in[:, 2 * hq:]], axis=1).T
    return dict(w_t=w_t.astype(BF16), w_k=w_in[:, hq:2 * hq].astype(BF16),
                wr_hi=wr_hi, wr_lo=wr_lo, b_row=b_if.reshape(ng, 1),
                head_norm=head_norm, w_o=w_o.astype(BF16))


def _gla_proj_kernel(x_ref, g_ref, win_ref, wa1_ref, wa2_ref, ba_ref, q_ref, k_ref, v_ref, r_ref, la_ref):
    hn = _rms_rows(x_ref[...], g_ref[...]).astype(BF16)
    y = _dot(hn, win_ref[...])
    hk, hv = GLA_HEADS * GLA_K, GLA_HEADS * GLA_V
    q_ref[...] = (y[:, :hk] * (GLA_K ** -0.5)).astype(BF16)
    k_ref[...] = y[:, hk:2 * hk].astype(BF16)
    v_ref[...] = y[:, 2 * hk:2 * hk + hv].astype(BF16)
    r_ref[...] = y[:, 2 * hk + hv:].astype(BF16)
    z = _dot(_dot(hn, wa1_ref[...]).astype(BF16), wa2_ref[...]) + ba_ref[...]
    la_ref[...] = _log_sigmoid(z) * (1.0 / GLA_TAU)


def _gla_levels():
    return [2 ** j for j in range(1, int(math.log2(REC_TILE)) + 1)]


def _gla_consts():
    n = REC_TILE
    t = np.arange(n)[:, None]
    u = np.arange(n)[None, :]
    mats = []
    for p in _gla_levels():
        first_upper = (t // p) * p + p // 2
        upper = (t % p) >= p // 2
        m_up = (u > first_upper) & (u <= t)
        m_lo = (u > t) & (u <= first_upper)
        mats.append(np.where(upper, m_up, m_lo))
    mats.append(u <= t)
    mats.append(u > t)
    return jnp.asarray(np.stack(mats).astype(np.float32), BF16)


def _gla_rec_kernel(q_ref, k_ref, v_ref, r_ref, la_ref, hnorm_ref, w_ref, o_ref, c_ref):
    L = REC_TILE
    levels = _gla_levels()
    nl = len(levels)

    @pl.when(pl.program_id(1) == 0)
    def _():
        c_ref[...] = jnp.zeros_like(c_ref)

    la_hi, la_lo = _split2(la_ref[...])

    def decay(j, two_term):
        w = w_ref[j]
        x = _dot(w, la_hi)
        if two_term:
            x = x + _dot(w, la_lo)
        return jnp.exp(x)

    row = lax.broadcasted_iota(jnp.int32, (L, L), 0)
    col = lax.broadcasted_iota(jnp.int32, (L, L), 1)
    rowk = lax.broadcasted_iota(jnp.int32, (L, GLA_K), 0)
    e_cum = decay(nl, True)
    e_rev = decay(nl + 1, True)
    e_lvl = [decay(j, False) for j in range(nl)]

    for h in range(GLA_HEADS):
        ks = slice(h * GLA_K, (h + 1) * GLA_K)
        vs = slice(h * GLA_V, (h + 1) * GLA_V)
        q = q_ref[:, ks].astype(F32)
        k = k_ref[:, ks].astype(F32)
        v = v_ref[:, vs]
        att = jnp.where(row == col, _dot_nt(q_ref[:, ks], k_ref[:, ks]), 0.0)
        for j, p in enumerate(levels):
            upper = (rowk & (p - 1)) >= (p // 2)
            e = e_lvl[j][:, ks]
            qf = jnp.where(upper, q * e, 0.0).astype(BF16)
            kf = jnp.where(upper, 0.0, k * e).astype(BF16)
            a = _dot_nt(qf, kf)
            if p < L:
                shift = int(math.log2(p))
                a = jnp.where((row >> shift) == (col >> shift), a, 0.0)
            att = att + a
        c_st = c_ref[h]
        o = _dot(att.astype(BF16), v) + _dot_nt((q * e_cum[:, ks]).astype(BF16), c_st.astype(BF16))
        c_ref[h] = e_cum[L - 1:L, ks] * c_st + _dot_tn(v, (k * e_rev[:, ks]).astype(BF16))
        o = _rms_rows(o, hnorm_ref[:, vs])
        rr = r_ref[:, vs].astype(F32)
        o_ref[:, vs] = (o * rr * _sigmoid(rr)).astype(BF16)


def _gla_layer(x, bsz, seq, g_mix, p, g_ffn, w1, w2):
    t, d = x.shape
    hk, hv = GLA_HEADS * GLA_K, GLA_HEADS * GLA_V
    row = lambda i: (i, 0)
    q, k, v, r, la = pl.pallas_call(
        _gla_proj_kernel,
        grid=(t // PROJ_TILE,),
        in_specs=[pl.BlockSpec((PROJ_TILE, d), row), _const_spec((1, d)), _const_spec((d, 2 * hk + 2 * hv)),
                  _const_spec((d, LANES)), _const_spec((LANES, hk)), _const_spec((1, hk))],
        out_specs=[pl.BlockSpec((PROJ_TILE, hk), row), pl.BlockSpec((PROJ_TILE, hk), row),
                   pl.BlockSpec((PROJ_TILE, hv), row), pl.BlockSpec((PROJ_TILE, hv), row),
                   pl.BlockSpec((PROJ_TILE, hk), row)],
        out_shape=[jax.ShapeDtypeStruct((t, hk), BF16), jax.ShapeDtypeStruct((t, hk), BF16),
                   jax.ShapeDtypeStruct((t, hv), BF16), jax.ShapeDtypeStruct((t, hv), BF16),
                   jax.ShapeDtypeStruct((t, hk), F32)],
        compiler_params=_cparams("parallel"),
        name="gla_proj",
    )(x, g_mix, p["w_in"], p["w_a1"], p["w_a2"], p["b_a"])

    nc = seq // REC_TILE
    wmats = _gla_consts()
    blk = lambda b, c: (b * nc + c, 0)
    a = pl.pallas_call(
        _gla_rec_kernel,
        grid=(bsz, nc),
        in_specs=[pl.BlockSpec((REC_TILE, hk), blk), pl.BlockSpec((REC_TILE, hk), blk),
                  pl.BlockSpec((REC_TILE, hv), blk), pl.BlockSpec((REC_TILE, hv), blk),
                  pl.BlockSpec((REC_TILE, hk), blk), _const_spec((1, hv)), _const_spec(wmats.shape)],
        out_specs=pl.BlockSpec((REC_TILE, hv), blk),
        out_shape=jax.ShapeDtypeStruct((t, hv), BF16),
        scratch_shapes=[pltpu.VMEM((GLA_HEADS, GLA_V, GLA_K), F32)],
        compiler_params=_cparams("parallel", "arbitrary"),
        name="gla_rec",
    )(q, k, v, r, la, p["head_norm"], wmats)
    return _mlp_call(x, a, p["w_o"], None, g_ffn, w1, w2)


def _gla_params(w_in, w_a1, w_a2, b_a, head_norm, w_o):
    return dict(w_in=w_in.astype(BF16),
                w_a1=jnp.pad(w_a1, ((0, 0), (0, LANES - GLA_GATE_RANK))).astype(BF16),
                w_a2=jnp.pad(w_a2, ((0, LANES - GLA_GATE_RANK), (0, 0))).astype(BF16),
                b_a=b_a.reshape(1, -1), head_norm=head_norm.reshape(1, -1), w_o=w_o.astype(BF16))


def _conv_kernel(x_ref, g_ref, w1_ref, b1_ref, wdw_ref, lng_ref, lnb_ref, o_ref, u_ref, c_ref):
    tm = TOK_TILE
    d = D_MODEL
    nbuf = CONV_HALO + tm

    @pl.when(pl.program_id(1) == 0)
    def _():
        u_ref[0, :, 0:CONV_HALO, :] = jnp.zeros((d // LANES, CONV_HALO, LANES), F32)

    @pl.when(pl.program_id(1) != 0)
    def _():
        u_ref[0, :, 0:CONV_HALO, :] = u_ref[0, :, tm:tm + CONV_HALO, :]

    hn = _rms_rows(x_ref[...], g_ref[...]).astype(BF16)
    y = _dot(hn, w1_ref[...]) + b1_ref[...]
    u = y[:, :d] * _sigmoid(y[:, d:])
    lead = CONV_HALO - (CONV_WIDTH - 1)
    groups = CONV_ROWS // SUBLANES
    tiles = nbuf // SUBLANES
    sub = lax.broadcasted_iota(jnp.int32, (tiles, SUBLANES, LANES), 1)

    for cb in range(d // LANES):
        cols = slice(cb * LANES, (cb + 1) * LANES)
        u_ref[0, cb, CONV_HALO:, :] = u[:, cols]
        full = u_ref[0, cb].reshape(tiles, SUBLANES, LANES)
        for s in range(1, SUBLANES):
            rot = pltpu.roll(full, SUBLANES - s, 1)
            nxt = jnp.concatenate([rot[1:], rot[:1]], axis=0)
            u_ref[s, cb] = jnp.where(sub < SUBLANES - s, rot, nxt).reshape(nbuf, LANES)

        def taps(c, carry):
            r0 = pl.multiple_of(c * CONV_ROWS, CONV_ROWS)
            acc = jnp.broadcast_to(wdw_ref[CONV_WIDTH, :, cols][None], (groups, SUBLANES, LANES))
            for j in range(CONV_WIDTH):
                off = lead + j
                start = pl.multiple_of(r0 + (off // SUBLANES) * SUBLANES, SUBLANES)
                uj = u_ref[off % SUBLANES, cb, pl.ds(start, CONV_ROWS), :].reshape(groups, SUBLANES, LANES)
                acc = acc + uj * wdw_ref[j, :, cols][None]
            c_ref[pl.ds(r0, CONV_ROWS), cols] = acc.reshape(CONV_ROWS, LANES)
            return carry

        lax.fori_loop(0, tm // CONV_ROWS, taps, 0)

    def norm(c, carry):
        r0 = pl.multiple_of(c * CONV_NORM_ROWS, CONV_NORM_ROWS)
        acc = c_ref[pl.ds(r0, CONV_NORM_ROWS), :]
        mu = jnp.mean(acc, axis=-1, keepdims=True)
        cen = acc - mu
        var = jnp.mean(cen * cen, axis=-1, keepdims=True)
        z = cen * lax.rsqrt(var + EPS) * lng_ref[...] + lnb_ref[...]
        o_ref[pl.ds(r0, CONV_NORM_ROWS), :] = (z * _sigmoid(z)).astype(BF16)
        return carry

    lax.fori_loop(0, tm // CONV_NORM_ROWS, norm, 0, unroll=8)


def _conv_layer(x, bsz, seq, g_mix, p, g_ffn, w1, w2):
    t, d = x.shape
    nt = seq // TOK_TILE
    blk = lambda b, i: (b * nt + i, 0)
    c = pl.pallas_call(
        _conv_kernel,
        grid=(bsz, nt),
        in_specs=[pl.BlockSpec((TOK_TILE, d), blk), _const_spec((1, d)), _const_spec((d, 2 * d)),
                  _const_spec((1, 2 * d)), _const_spec((CONV_WIDTH + 1, SUBLANES, d)),
                  _const_spec((1, d)), _const_spec((1, d))],
        out_specs=pl.BlockSpec((TOK_TILE, d), blk),
        out_shape=jax.ShapeDtypeStruct((t, d), BF16),
        scratch_shapes=[pltpu.VMEM((SUBLANES, d // LANES, CONV_HALO + TOK_TILE, LANES), F32),
                        pltpu.VMEM((TOK_TILE, d), F32)],
        compiler_params=_cparams("parallel", "arbitrary"),
        name="conv",
    )(x, g_mix, p["w_pw1"], p["b_pw1"], p["w_dw"], p["ln_g"], p["ln_b"])
    return _mlp_call(x, c, p["w_pw2"], p["b_pw2"], g_ffn, w1, w2)


def _conv_params(w_pw1, b_pw1, w_dw, b_dw, ln_g, ln_b, w_pw2, b_pw2):
    return dict(w_pw1=w_pw1.astype(BF16), b_pw1=b_pw1.reshape(1, -1),
                w_dw=jnp.broadcast_to(jnp.concatenate([w_dw, b_dw[None]], axis=0)[:, None, :],
                                      (CONV_WIDTH + 1, SUBLANES, w_dw.shape[1])),
                ln_g=ln_g.reshape(1, -1), ln_b=ln_b.reshape(1, -1),
                w_pw2=w_pw2.astype(BF16), b_pw2=b_pw2.reshape(1, -1))


def kernel(x, norm_mix, norm_ffn, mla_w_dq, mla_q_norm, mla_w_uq, mla_w_dkv, mla_kv_norm, mla_w_ukv, mla_q_gain, mla_k_gain, mla_w_o, mlstm_w_in, mlstm_w_if, mlstm_b_if, mlstm_head_norm, mlstm_w_o, gla_w_in, gla_w_a1, gla_w_a2, gla_b_a, gla_head_norm, gla_w_o, conv_w_pw1, conv_b_pw1, conv_w_dw, conv_b_dw, conv_ln_g, conv_ln_b, conv_w_pw2, conv_b_pw2, ffn_w1, ffn_w2):
    bsz, seq, d = x.shape
    depth = norm_mix.shape[0]
    assert d == D_MODEL and all(seq % tile == 0 for tile in (ATT_Q_TILE, TOK_TILE, PROJ_TILE, REC_TILE))
    h = x.reshape(bsz * seq, d)
    for i in range(depth):
        kind, j = i % 4, i // 4
        g_mix = norm_mix[i].reshape(1, d)
        g_ffn = norm_ffn[i].reshape(1, d)
        w1 = ffn_w1[i].astype(BF16)
        w2 = ffn_w2[i].astype(BF16)
        if kind == 0:
            p = _mla_params(seq, mla_w_dq[j], mla_q_norm[j], mla_w_uq[j], mla_w_dkv[j], mla_kv_norm[j],
                            mla_w_ukv[j], mla_q_gain[j], mla_k_gain[j], mla_w_o[j])
            h = _mla_layer(h, bsz, seq, g_mix, p, g_ffn, w1, w2)
        elif kind == 1:
            p = _mlstm_params(mlstm_w_in[j], mlstm_w_if[j], mlstm_b_if[j], mlstm_head_norm[j], mlstm_w_o[j])
            h = _mlstm_layer(h, bsz, seq, g_mix, p, g_ffn, w1, w2)
        elif kind == 2:
            p = _gla_params(gla_w_in[j], gla_w_a1[j], gla_w_a2[j], gla_b_a[j], gla_head_norm[j], gla_w_o[j])
            h = _gla_layer(h, bsz, seq, g_mix, p, g_ffn, w1, w2)
        else:
            p = _conv_params(conv_w_pw1[j], conv_b_pw1[j], conv_w_dw[j], conv_b_dw[j], conv_ln_g[j],
                             conv_ln_b[j], conv_w_pw2[j], conv_b_pw2[j])
            h = _conv_layer(h, bsz, seq, g_mix, p, g_ffn, w1, w2)
    return h.reshape(bsz, seq, d)
```

```python
import functools
import math

import numpy as np
import jax
import jax.numpy as jnp
from jax import lax
from jax.experimental import pallas as pl
from jax.experimental.pallas import tpu as pltpu

F32 = jnp.float32
BF16 = jnp.bfloat16

D_MODEL = 1024
D_FF = 4 * D_MODEL
EPS = 1e-6
CHUNK = 64

MLA_HEADS = 16
MLA_NOPE = 64
MLA_ROPE = 32
MLA_QK = MLA_NOPE + MLA_ROPE
MLA_V = 64
MLA_Q_RANK = 384
MLA_KV_RANK = 256
ROPE_BASE = 10000.0
MLA_VROWS = MLA_V + 16

ML_HEADS = 4
ML_QK = D_MODEL // 8
ML_V = D_MODEL // 4
GATE_CAP = 15.0

GLA_HEADS = 4
GLA_K = D_MODEL // 8
GLA_V = D_MODEL // 4
GLA_GATE_RANK = 16
GLA_TAU = 16.0

CONV_WIDTH = 31

LANES = 128
SUBLANES = 8
VMEM_LIMIT = 56 * 1024 * 1024

TOK_TILE = 512
PROJ_TILE = 1024
FF_TILE = 1024
ATT_TILE = 512
ATT_Q_TILE = 1024
ATT_HEADS = 4
REC_TILE = 256
CONV_HALO = 32
CONV_ROWS = 128
CONV_NORM_ROWS = 32
NEG_BIG = -1e30


def _cparams(*sem):
    return pltpu.CompilerParams(dimension_semantics=sem, vmem_limit_bytes=VMEM_LIMIT)


def _const_spec(shape):
    nd = len(shape)
    return pl.BlockSpec(shape, lambda *_: (0,) * nd, pipeline_mode=pl.Buffered(1))


def _dot(a, b):
    return jnp.dot(a, b, preferred_element_type=F32)


def _dot_nt(a, b):
    return lax.dot_general(a, b, (((1,), (1,)), ((), ())), preferred_element_type=F32)


def _dot_tn(a, b):
    return lax.dot_general(a, b, (((0,), (0,)), ((), ())), preferred_element_type=F32)


def _split2(a):
    hi = a.astype(BF16)
    lo = (a - hi.astype(F32)).astype(BF16)
    return hi, lo


def _split3(a):
    hi = a.astype(BF16)
    r = a - hi.astype(F32)
    mid = r.astype(BF16)
    lo = (r - mid.astype(F32)).astype(BF16)
    return hi, mid, lo


def _rms_rows(x, g):
    return x * lax.rsqrt(jnp.mean(x * x, axis=-1, keepdims=True) + EPS) * g


def _log_sigmoid(z):
    return jnp.minimum(z, 0.0) - jnp.log1p(jnp.exp(-jnp.abs(z)))


def _sigmoid(z):
    return 1.0 / (1.0 + jnp.exp(-z))


def _mlp_kernel(*refs, has_bias, a_transposed):
    if has_bias:
        x_ref, a_ref, wo_ref, bo_ref, g_ref, w1_ref, w2_ref, o_ref = refs
    else:
        x_ref, a_ref, wo_ref, g_ref, w1_ref, w2_ref, o_ref = refs
    if a_transposed:
        x1 = x_ref[...] + _dot_tn(a_ref[...], wo_ref[...])
    else:
        x1 = x_ref[...] + _dot(a_ref[...], wo_ref[...])
    if has_bias:
        x1 = x1 + bo_ref[...]
    hn = _rms_rows(x1, g_ref[...]).astype(BF16)
    acc = x1
    for c in range(D_FF // FF_TILE):
        h = _dot(hn, w1_ref[:, c * FF_TILE:(c + 1) * FF_TILE])
        h = jnp.maximum(h, 0.0)
        acc = acc + _dot((h * h).astype(BF16), w2_ref[c * FF_TILE:(c + 1) * FF_TILE, :])
    o_ref[...] = acc


def _mlp_call(x, a, w_o, b_o, g, w1, w2, a_transposed=False):
    t, d = x.shape
    din = w_o.shape[0]
    has_bias = b_o is not None
    (w1, layer), (w2, _) = w1, w2
    layer_spec = lambda shape: pl.BlockSpec((None,) + shape, lambda *_: (layer, 0, 0), pipeline_mode=pl.Buffered(1))
    row = lambda i: (i, 0)
    a_spec = pl.BlockSpec((din, TOK_TILE), lambda i: (0, i)) if a_transposed else pl.BlockSpec((TOK_TILE, din), row)
    in_specs = [pl.BlockSpec((TOK_TILE, d), row), a_spec, _const_spec((din, d))]
    args = [x, a, w_o]
    if has_bias:
        in_specs.append(_const_spec((1, d)))
        args.append(b_o)
    in_specs += [_const_spec((1, d)), layer_spec((d, D_FF)), layer_spec((D_FF, d))]
    args += [g, w1, w2]
    return pl.pallas_call(
        functools.partial(_mlp_kernel, has_bias=has_bias, a_transposed=a_transposed),
        grid=(t // TOK_TILE,),
        in_specs=in_specs,
        out_specs=pl.BlockSpec((TOK_TILE, d), row),
        out_shape=jax.ShapeDtypeStruct((t, d), F32),
        compiler_params=_cparams("parallel"),
        name="mlp",
    )(*args)


def _mla_proj_kernel(x_ref, g_ref, wdq_ref, qn_ref, wuqt_ref, wdkv_ref, kvn_ref, wukt_ref, wuvt_ref,
                     vone_ref, qc_ref, qsa_ref, qsb_ref, kc_ref, ksa_ref, ksb_ref,
                     qt_ref, k_ref, vt_ref):
    half = MLA_ROPE // 2
    hn = _rms_rows(x_ref[...], g_ref[...]).astype(BF16)
    cq = _rms_rows(_dot(hn, wdq_ref[...]), qn_ref[...]).astype(BF16)
    dkv = _dot(hn, wdkv_ref[...])
    ckv = _rms_rows(dkv[:, :MLA_KV_RANK], kvn_ref[...]).astype(BF16)

    vt_ref[0] = (_dot_nt(wuvt_ref[...], ckv) + vone_ref[...]).astype(BF16)

    n0, a0, b0, e0 = 0, MLA_NOPE, MLA_NOPE + half, MLA_QK

    def ssq(t):
        return jnp.sum(t * t, axis=0, keepdims=True)

    def rope(a, b, c, sa, sb):
        return a * c[a0:b0] + b * sa[a0:b0], b * c[b0:e0] + a * sb[b0:e0]

    def inv_rms(s):
        return lax.rsqrt(s * (1.0 / MLA_QK) + EPS)

    tm = x_ref.shape[0]
    pad_rows = jnp.zeros((LANES - MLA_QK, tm), F32)

    qt = _dot_nt(wuqt_ref[...], cq)
    qc, qsa, qsb = qc_ref[...], qsa_ref[...], qsb_ref[...]
    for h in range(MLA_HEADS):
        t = qt[h * MLA_QK:(h + 1) * MLA_QK]
        n, a, b = t[n0:a0], t[a0:b0], t[b0:e0]
        r = inv_rms(ssq(n) + ssq(a) + ssq(b))
        ra, rb = rope(a, b, qc, qsa, qsb)
        qt_ref[h * LANES:h * LANES + MLA_QK, :] = jnp.concatenate(
            [n * qc[n0:a0] * r, ra * r, rb * r], axis=0).astype(BF16)
        qt_ref[h * LANES + MLA_QK:(h + 1) * LANES, :] = pad_rows.astype(BF16)

    krt = dkv[:, MLA_KV_RANK:].T
    kt = _dot_nt(wukt_ref[...], ckv)
    kc, ksa, ksb = kc_ref[...], ksa_ref[...], ksb_ref[...]
    ka, kb = krt[a0:b0], krt[b0:e0]
    ssq_rope = ssq(ka) + ssq(kb)
    ra, rb = rope(ka, kb, kc, ksa, ksb)
    for h in range(MLA_HEADS):
        n = kt[h * MLA_NOPE:(h + 1) * MLA_NOPE]
        r = inv_rms(ssq(n) + ssq_rope)
        kh = jnp.concatenate([n * kc[n0:a0] * r, ra * r, rb * r, pad_rows], axis=0)
        k_ref[:, h * LANES:(h + 1) * LANES] = kh.T.astype(BF16)


def _attn_kernel(qt_ref, k_ref, vt_ref, o_ref, s_ref, acc_ref, m_ref, mb_ref):
    i = pl.program_id(2)
    tq, tk = ATT_Q_TILE, ATT_TILE
    nh = ATT_HEADS
    kpq = tq // tk
    assert kpq == 2
    vrows = vt_ref.shape[1] // nh
    krow = lax.broadcasted_iota(jnp.int32, (tk, tq), 0)
    qcol = lax.broadcasted_iota(jnp.int32, (tk, tq), 1)
    diag_mask = (krow // CHUNK) <= (qcol // CHUNK)
    late = slice(tk, tq)

    def produce(h, kb, qcols=slice(None), mask=None):
        start = pl.multiple_of(kb * tk, tk)
        st = _dot(k_ref[pl.ds(start, tk), h * LANES:(h + 1) * LANES], qt_ref[h * LANES:(h + 1) * LANES, qcols])
        if mask is not None:
            st = jnp.where(mask, st, NEG_BIG)
        s_ref[h, :, qcols] = st
        mb_ref[h, :, qcols] = jnp.max(st, axis=0, keepdims=True)

    def consume(h, kb, mask=None, qcols=slice(None)):
        st = s_ref[h, :, qcols]
        if mask is None:
            mblk = mb_ref[h, :, qcols]
        else:
            st = jnp.where(mask, st, NEG_BIG)
            mblk = jnp.max(st, axis=0, keepdims=True)
        m = m_ref[h, :, qcols]
        m_new = jnp.maximum(m, mblk)
        m_ref[h, :, qcols] = m_new
        pt = jnp.exp2(st - m_new).astype(BF16)
        acc_ref[h, :, qcols] = (jnp.exp2(m - m_new) * acc_ref[h, :, qcols]
                                + _dot(vt_ref[kb, h * vrows:(h + 1) * vrows, :], pt))

    def body(j, carry):
        for h in range(nh):
            produce((h + 1) % nh, j + (h + 1) // nh)
            consume(h, j)
        return carry

    m_ref[...] = jnp.full(m_ref.shape, NEG_BIG, F32)
    acc_ref[...] = jnp.zeros(acc_ref.shape, F32)
    first = kpq * i
    produce(0, 0)
    lax.fori_loop(0, i, lambda t, c: body(kpq * t + 1, body(kpq * t, c)), 0)
    late_mask = diag_mask[:, :tk]
    for h in range(nh):
        if h + 1 < nh:
            produce(h + 1, first, mask=diag_mask)
        else:
            produce(0, first + 1, late, late_mask)
        consume(h, first, diag_mask if h == 0 else None)
    for h in range(nh):
        if h + 1 < nh:
            produce(h + 1, first + 1, late, late_mask)
        consume(h, first + 1, None, late)
        acc = acc_ref[h]
        o_ref[h * MLA_V:(h + 1) * MLA_V, :] = (acc[:MLA_V] / acc[MLA_V:MLA_V + 1]).astype(BF16)


def _mla_layer(x, bsz, seq, g_mix, p, g_ffn, w1, w2):
    t, d = x.shape
    assert TOK_TILE == ATT_TILE
    hp = MLA_HEADS * LANES
    hv = MLA_HEADS * MLA_VROWS
    nt = seq // TOK_TILE
    row = lambda i: (i, 0)
    ttab = lambda i: (i % nt, 0)
    ftab = lambda i: (0, i % nt)
    qt, k, vt = pl.pallas_call(
        _mla_proj_kernel,
        grid=(t // TOK_TILE,),
        in_specs=[pl.BlockSpec((TOK_TILE, d), row), _const_spec((1, d)),
                  _const_spec((d, MLA_Q_RANK)), _const_spec((1, MLA_Q_RANK)),
                  _const_spec((MLA_HEADS * MLA_QK, MLA_Q_RANK)),
                  _const_spec((d, MLA_KV_RANK + LANES)), _const_spec((1, MLA_KV_RANK)),
                  _const_spec((MLA_HEADS * MLA_NOPE, MLA_KV_RANK)), _const_spec((hv, MLA_KV_RANK)),
                  _const_spec((hv, TOK_TILE))] + [pl.BlockSpec((LANES, TOK_TILE), ftab)] * 6,
        out_specs=[pl.BlockSpec((hp, TOK_TILE), lambda i: (0, i)), pl.BlockSpec((TOK_TILE, hp), row),
                   pl.BlockSpec((1, hv, TOK_TILE), lambda i: (i, 0, 0))],
        out_shape=[jax.ShapeDtypeStruct((hp, t), BF16), jax.ShapeDtypeStruct((t, hp), BF16),
                   jax.ShapeDtypeStruct((t // TOK_TILE, hv, TOK_TILE), BF16)],
        compiler_params=_cparams("parallel"),
        name="mla_proj",
    )(x, g_mix, p["w_dq"], p["q_norm"], p["w_uq_t"], p["w_dkv"], p["kv_norm"], p["w_uk_t"], p["w_uv_t"],
      p["vone"], p["qc"], p["qsa"], p["qsb"], p["kc"], p["ksa"], p["ksb"])

    nq = seq // ATT_Q_TILE
    ot = pl.pallas_call(
        _attn_kernel,
        grid=(bsz, MLA_HEADS // ATT_HEADS, nq),
        in_specs=[pl.BlockSpec((ATT_HEADS * LANES, ATT_Q_TILE), lambda b, h, i: (h, b * nq + i)),
                  pl.BlockSpec((seq, ATT_HEADS * LANES), lambda b, h, i: (b, h)),
                  pl.BlockSpec((seq // ATT_TILE, ATT_HEADS * MLA_VROWS, ATT_TILE), lambda b, h, i: (b, h, 0))],
        out_specs=pl.BlockSpec((ATT_HEADS * MLA_V, ATT_Q_TILE), lambda b, h, i: (h, b * nq + i)),
        out_shape=jax.ShapeDtypeStruct((MLA_HEADS * MLA_V, t), BF16),
        scratch_shapes=[pltpu.VMEM((ATT_HEADS, ATT_TILE, ATT_Q_TILE), F32),
                        pltpu.VMEM((ATT_HEADS, MLA_VROWS, ATT_Q_TILE), F32),
                        pltpu.VMEM((ATT_HEADS, 1, ATT_Q_TILE), F32), pltpu.VMEM((ATT_HEADS, 1, ATT_Q_TILE), F32)],
        compiler_params=_cparams("parallel", "parallel", "arbitrary"),
        name="mla_attn",
    )(qt, k, vt)
    return _mlp_call(x, ot, p["w_o"], None, g_ffn, w1, w2, a_transposed=True)


def _mla_params(seq, w_dq, q_norm, w_uq, w_dkv, kv_norm, w_ukv, q_gain, k_gain, w_o):
    pad_head = LANES - MLA_QK
    half = MLA_ROPE // 2
    w_ukv_r = w_ukv.reshape(MLA_KV_RANK, MLA_HEADS, MLA_NOPE + MLA_V)
    w_uv_p = jnp.pad(w_ukv_r[:, :, MLA_NOPE:], ((0, 0), (0, 0), (0, MLA_VROWS - MLA_V)))
    w_rope = jnp.pad(w_dkv[:, MLA_KV_RANK:], ((0, 0), (MLA_NOPE, pad_head)))
    w_dkv_p = jnp.concatenate([w_dkv[:, :MLA_KV_RANK], w_rope], axis=1)
    vone = np.zeros((MLA_HEADS * MLA_VROWS, TOK_TILE), np.float32)
    vone[np.arange(MLA_HEADS) * MLA_VROWS + MLA_V] = 1.0
    gq = q_gain * ((MLA_QK ** -0.5) * math.log2(math.e))
    inv_freq = ROPE_BASE ** (-jnp.arange(half, dtype=F32) / half)
    ang = jnp.arange(seq).astype(F32)[:, None] * inv_freq[None, :]
    cos, sin = jnp.cos(ang), jnp.sin(ang)
    g1 = lambda g: g[MLA_NOPE:MLA_NOPE + half][None, :]
    g2 = lambda g: g[MLA_NOPE + half:MLA_QK][None, :]
    z = lambda n: jnp.zeros((seq, n), F32)

    def tables(g):
        nope = jnp.broadcast_to(g[None, :MLA_NOPE], (seq, MLA_NOPE))
        c = jnp.concatenate([nope, g1(g) * cos, g2(g) * cos, z(pad_head)], axis=1)
        sa = jnp.concatenate([z(MLA_NOPE), -g2(g) * sin, z(half + pad_head)], axis=1)
        sb = jnp.concatenate([z(MLA_NOPE + half), g1(g) * sin, z(pad_head)], axis=1)
        return c.T, sa.T, sb.T

    qc, qsa, qsb = tables(gq)
    kc, ksa, ksb = tables(k_gain)
    return dict(w_dq=w_dq.astype(BF16), q_norm=q_norm.reshape(1, -1),
                w_uq_t=w_uq.T.astype(BF16), w_dkv=w_dkv_p.astype(BF16), kv_norm=kv_norm.reshape(1, -1),
                w_uk_t=w_ukv_r[:, :, :MLA_NOPE].reshape(MLA_KV_RANK, -1).T.astype(BF16),
                w_uv_t=w_uv_p.reshape(MLA_KV_RANK, -1).T.astype(BF16), vone=jnp.asarray(vone, BF16),
                qc=qc, qsa=qsa, qsb=qsb, kc=kc, ksa=ksa, ksb=ksb, w_o=w_o.astype(BF16))


def _mlstm_proj_kernel(x_ref, g_ref, wt_ref, wk_ref, wr_hi_ref, br_ref,
                       qt_ref, k_ref, vt_ref, ot_ref, gr_ref):
    hn = _rms_rows(x_ref[...], g_ref[...])
    hi, lo = _split2(hn)
    hq, hv = ML_HEADS * ML_QK, ML_HEADS * ML_V
    yt = _dot_nt(wt_ref[...], hi)
    qt_ref[...] = yt[:hq].astype(BF16)
    vt_ref[...] = yt[hq:hq + hv].astype(BF16)
    ot_ref[...] = yt[hq + hv:hq + 2 * hv].astype(BF16)
    k_ref[...] = (_dot(hi, wk_ref[...]) * (ML_QK ** -0.5)).astype(BF16)
    g0 = hq + 2 * hv
    gr_ref[...] = (yt[g0:g0 + SUBLANES] + yt[g0 + SUBLANES:] + _dot_nt(wr_hi_ref[...], lo)) + br_ref[...]


def _cap(g):
    return GATE_CAP * jnp.tanh(g * (1.0 / GATE_CAP))


def _mlstm_rec_kernel(qt_ref, k_ref, vt_ref, opt_ref, gr_ref, hnorm_ref, triu_ref, sel_ref,
                      o_ref, c_ref, m_ref):
    L = REC_TILE

    @pl.when(pl.program_id(1) == 0)
    def _():
        c_ref[...] = jnp.zeros_like(c_ref)
        m_ref[...] = jnp.zeros_like(m_ref)

    gr = _cap(gr_ref[...])
    r1, r2, r3 = _split3(_log_sigmoid(gr))
    triu = triu_ref[...]
    bcum_r = _dot(r1, triu) + _dot(r2, triu) + _dot(r3, triu)
    grow = lax.broadcasted_iota(jnp.int32, (SUBLANES, L), 0)
    xr = jnp.concatenate([jnp.where(grow < ML_HEADS, gr, bcum_r), jnp.zeros((LANES - SUBLANES, L), F32)], axis=0)
    x1, x2, x3 = _split3(xr)
    sel = sel_ref[...]
    cb_all = _dot_tn(x1, sel) + _dot_tn(x2, sel) + _dot_tn(x3, sel)
    src = lax.broadcasted_iota(jnp.int32, (L, L), 0)
    tgt = lax.broadcasted_iota(jnp.int32, (L, L), 1)
    causal = src <= tgt
    ones = jnp.ones((LANES, L), BF16)

    for h in range(ML_HEADS):
        fh = ML_HEADS + h
        cb = cb_all[:, h * L:(h + 1) * L]
        bt = bcum_r[fh:fh + 1, :]
        li_r = gr[h:h + 1, :]
        m_prev = m_ref[h:h + 1, 0:1]
        qt = qt_ref[h * ML_QK:(h + 1) * ML_QK, :]
        k = k_ref[:, h * ML_QK:(h + 1) * ML_QK]
        vaug = jnp.concatenate([vt_ref[h * ML_V:(h + 1) * ML_V, :], ones], axis=0)
        c_st = c_ref[h]

        dmat = jnp.where(causal, cb + bt, NEG_BIG)
        inter = bt + m_prev
        m_t = jnp.maximum(inter, jnp.max(dmat, axis=0, keepdims=True))
        w_intra = jnp.exp(dmat - m_t)
        w_inter = jnp.exp(inter - m_t)
        pt = (_dot(k, qt) * w_intra).astype(BF16)
        nd = _dot(vaug, pt) + w_inter * _dot(c_st.astype(BF16), qt)
        den = jnp.maximum(jnp.abs(nd[ML_V:ML_V + 1]), jnp.exp(-m_t))
        hc = nd[:ML_V] / den

        b_last = bt[:, L - 1:L]
        m_new = jnp.maximum(b_last + m_prev, jnp.max(b_last - bt + li_r, axis=-1, keepdims=True))
        ws = jnp.exp(cb[:, :ML_QK] + (b_last - m_new))
        wc = jnp.exp(b_last + m_prev - m_new)
        kw = (k.astype(F32) * ws).astype(BF16)
        c_ref[h] = wc * c_st + _dot(vaug, kw)
        m_ref[h:h + 1, :] = jnp.broadcast_to(m_new, (1, LANES))

        rows = slice(h * ML_V, (h + 1) * ML_V)
        hs = hc * lax.rsqrt(jnp.mean(hc * hc, axis=0, keepdims=True) + EPS) * hnorm_ref[rows, :]
        o_ref[rows, :] = (_sigmoid(opt_ref[rows, :].astype(F32)) * hs).astype(BF16)


def _mlstm_consts(n):
    r = np.arange(n)
    triu = (r[:, None] <= r[None, :]).astype(np.float32)
    sel = np.zeros((LANES, ML_HEADS * n), np.float32)
    for h in range(ML_HEADS):
        sel[h, h * n:(h + 1) * n] = 1.0
        sel[ML_HEADS + h, h * n:(h + 1) * n] = -1.0
    return jnp.asarray(triu, BF16), jnp.asarray(sel, BF16)


def _mlstm_layer(x, bsz, seq, g_mix, p, g_ffn, w1, w2):
    t, d = x.shape
    hq, hv = ML_HEADS * ML_QK, ML_HEADS * ML_V
    row = lambda i: (i, 0)
    col = lambda i: (0, i)
    qt, k, vt, opt, grow = pl.pallas_call(
        _mlstm_proj_kernel,
        grid=(t // PROJ_TILE,),
        in_specs=[pl.BlockSpec((PROJ_TILE, d), row), _const_spec((1, d)),
                  _const_spec((hq + 2 * hv + 2 * SUBLANES, d)),
                  _const_spec((d, hq)), _const_spec((SUBLANES, d)), _const_spec((SUBLANES, 1))],
        out_specs=[pl.BlockSpec((hq, PROJ_TILE), col), pl.BlockSpec((PROJ_TILE, hq), row),
                   pl.BlockSpec((hv, PROJ_TILE), col), pl.BlockSpec((hv, PROJ_TILE), col),
                   pl.BlockSpec((SUBLANES, PROJ_TILE), col)],
        out_shape=[jax.ShapeDtypeStruct((hq, t), BF16), jax.ShapeDtypeStruct((t, hq), BF16),
                   jax.ShapeDtypeStruct((hv, t), BF16), jax.ShapeDtypeStruct((hv, t), BF16),
                   jax.ShapeDtypeStruct((SUBLANES, t), F32)],
        compiler_params=_cparams("parallel"),
        name="mlstm_proj",
    )(x, g_mix, p["w_t"], p["w_k"], p["wr_hi"], p["b_row"])

    nc = seq // REC_TILE
    triu, sel = _mlstm_consts(REC_TILE)
    hnorm = jnp.broadcast_to(p["head_norm"].reshape(hv, 1), (hv, REC_TILE))
    rblk = lambda b, c: (b * nc + c, 0)
    cblk = lambda b, c: (0, b * nc + c)
    at = pl.pallas_call(
        _mlstm_rec_kernel,
        grid=(bsz, nc),
        in_specs=[pl.BlockSpec((hq, REC_TILE), cblk), pl.BlockSpec((REC_TILE, hq), rblk),
                  pl.BlockSpec((hv, REC_TILE), cblk), pl.BlockSpec((hv, REC_TILE), cblk),
                  pl.BlockSpec((SUBLANES, REC_TILE), cblk),
                  _const_spec((hv, REC_TILE)), _const_spec((REC_TILE, REC_TILE)),
                  _const_spec((LANES, ML_HEADS * REC_TILE))],
        out_specs=pl.BlockSpec((hv, REC_TILE), cblk),
        out_shape=jax.ShapeDtypeStruct((hv, t), BF16),
        scratch_shapes=[pltpu.VMEM((ML_HEADS, ML_V + LANES, ML_QK), F32), pltpu.VMEM((SUBLANES, LANES), F32)],
        compiler_params=_cparams("parallel", "arbitrary"),
        name="mlstm_rec",
    )(qt, k, vt, opt, grow, hnorm, triu, sel)
    return _mlp_call(x, at, p["w_o"], None, g_ffn, w1, w2, a_transposed=True)


def _mlstm_params(w_in, w_if, b_if, head_norm, w_o):
    ng = 2 * ML_HEADS
    hq = ML_HEADS * ML_QK
    w_row = w_if.T
    wr_hi = w_row.astype(BF16)
    wr_lo = (w_row - wr_hi.astype(F32)).astype(BF16)
    w_t = jnp.concatenate([w_in[:, :hq], w_in[:, 2 * hq:]], axis=1).T.astype(BF16)
    return dict(w_t=jnp.concatenate([w_t, wr_hi, wr_lo], axis=0), w_k=w_in[:, hq:2 * hq].astype(BF16),
                wr_hi=wr_hi, b_row=b_if.reshape(ng, 1),
                head_norm=head_norm, w_o=w_o.astype(BF16))


def _gla_proj_kernel(x_ref, g_ref, win_ref, wa1_ref, wa2_ref, ba_ref, q_ref, k_ref, v_ref, r_ref, la_ref):
    hn = _rms_rows(x_ref[...], g_ref[...]).astype(BF16)
    y = _dot(hn, win_ref[...])
    hk, hv = GLA_HEADS * GLA_K, GLA_HEADS * GLA_V
    q_ref[...] = (y[:, :hk] * (GLA_K ** -0.5)).astype(BF16)
    k_ref[...] = y[:, hk:2 * hk].astype(BF16)
    v_ref[...] = y[:, 2 * hk:2 * hk + hv].astype(BF16)
    r_ref[...] = y[:, 2 * hk + hv:].astype(BF16)
    z = _dot(_dot(hn, wa1_ref[...]).astype(BF16), wa2_ref[...]) + ba_ref[...]
    la_ref[...] = _log_sigmoid(z) * (1.0 / GLA_TAU)


def _gla_levels():
    return [2 ** j for j in range(1, int(math.log2(REC_TILE)) + 1)]


def _gla_consts():
    n = REC_TILE
    t = np.arange(n)[:, None]
    u = np.arange(n)[None, :]
    mats = []
    for p in _gla_levels():
        first_upper = (t // p) * p + p // 2
        upper = (t % p) >= p // 2
        m_up = (u > first_upper) & (u <= t)
        m_lo = (u > t) & (u <= first_upper)
        mats.append(np.where(upper, m_up, m_lo))
    mats.append(u <= t)
    mats.append(u > t)
    return jnp.asarray(np.stack(mats).astype(np.float32), BF16)


def _gla_rec_kernel(q_ref, k_ref, v_ref, r_ref, la_ref, hnorm_ref, w_ref, o_ref, c_ref):
    L = REC_TILE
    levels = _gla_levels()
    nl = len(levels)

    @pl.when(pl.program_id(1) == 0)
    def _():
        c_ref[...] = jnp.zeros_like(c_ref)

    la_hi, la_lo = _split2(la_ref[...])

    def decay(j, two_term):
        w = w_ref[j]
        x = _dot(w, la_hi)
        if two_term:
            x = x + _dot(w, la_lo)
        return jnp.exp(x)

    row = lax.broadcasted_iota(jnp.int32, (L, L), 0)
    col = lax.broadcasted_iota(jnp.int32, (L, L), 1)
    row_xor_col = row ^ col
    rowk = lax.broadcasted_iota(jnp.int32, (L, GLA_K), 0)
    e_cum = decay(nl, True)
    e_rev = decay(nl + 1, True)
    e_lvl = [decay(j, False) for j in range(nl)]

    for h in range(GLA_HEADS):
        ks = slice(h * GLA_K, (h + 1) * GLA_K)
        vs = slice(h * GLA_V, (h + 1) * GLA_V)
        q = q_ref[:, ks].astype(F32)
        k = k_ref[:, ks].astype(F32)
        v = v_ref[:, vs]
        att = jnp.where(row == col, _dot_nt(q_ref[:, ks], k_ref[:, ks]), 0.0)
        for j, p in enumerate(levels):
            upper = (rowk & (p - 1)) >= (p // 2)
            e = e_lvl[j][:, ks]
            qf = jnp.where(upper, q * e, 0.0).astype(BF16)
            kf = jnp.where(upper, 0.0, k * e).astype(BF16)
            a = _dot_nt(qf, kf)
            if p < L:
                a = jnp.where(row_xor_col < p, a, 0.0)
            att = att + a
        c_st = c_ref[h]
        o = _dot(att.astype(BF16), v) + _dot_nt((q * e_cum[:, ks]).astype(BF16), c_st.astype(BF16))
        c_ref[h] = e_cum[L - 1:L, ks] * c_st + _dot_tn(v, (k * e_rev[:, ks]).astype(BF16))
        o = _rms_rows(o, hnorm_ref[:, vs])
        rr = r_ref[:, vs].astype(F32)
        o_ref[:, vs] = (o * rr * _sigmoid(rr)).astype(BF16)


def _gla_layer(x, bsz, seq, g_mix, p, g_ffn, w1, w2):
    t, d = x.shape
    hk, hv = GLA_HEADS * GLA_K, GLA_HEADS * GLA_V
    row = lambda i: (i, 0)
    q, k, v, r, la = pl.pallas_call(
        _gla_proj_kernel,
        grid=(t // PROJ_TILE,),
        in_specs=[pl.BlockSpec((PROJ_TILE, d), row), _const_spec((1, d)), _const_spec((d, 2 * hk + 2 * hv)),
                  _const_spec((d, LANES)), _const_spec((LANES, hk)), _const_spec((1, hk))],
        out_specs=[pl.BlockSpec((PROJ_TILE, hk), row), pl.BlockSpec((PROJ_TILE, hk), row),
                   pl.BlockSpec((PROJ_TILE, hv), row), pl.BlockSpec((PROJ_TILE, hv), row),
                   pl.BlockSpec((PROJ_TILE, hk), row)],
        out_shape=[jax.ShapeDtypeStruct((t, hk), BF16), jax.ShapeDtypeStruct((t, hk), BF16),
                   jax.ShapeDtypeStruct((t, hv), BF16), jax.ShapeDtypeStruct((t, hv), BF16),
                   jax.ShapeDtypeStruct((t, hk), F32)],
        compiler_params=_cparams("parallel"),
        name="gla_proj",
    )(x, g_mix, p["w_in"], p["w_a1"], p["w_a2"], p["b_a"])

    nc = seq // REC_TILE
    wmats = _gla_consts()
    blk = lambda b, c: (b * nc + c, 0)
    a = pl.pallas_call(
        _gla_rec_kernel,
        grid=(bsz, nc),
        in_specs=[pl.BlockSpec((REC_TILE, hk), blk), pl.BlockSpec((REC_TILE, hk), blk),
                  pl.BlockSpec((REC_TILE, hv), blk), pl.BlockSpec((REC_TILE, hv), blk),
                  pl.BlockSpec((REC_TILE, hk), blk), _const_spec((1, hv)), _const_spec(wmats.shape)],
        out_specs=pl.BlockSpec((REC_TILE, hv), blk),
        out_shape=jax.ShapeDtypeStruct((t, hv), BF16),
        scratch_shapes=[pltpu.VMEM((GLA_HEADS, GLA_V, GLA_K), F32)],
        compiler_params=_cparams("parallel", "arbitrary"),
        name="gla_rec",
    )(q, k, v, r, la, p["head_norm"], wmats)
    return _mlp_call(x, a, p["w_o"], None, g_ffn, w1, w2)


def _gla_params(w_in, w_a1, w_a2, b_a, head_norm, w_o):
    return dict(w_in=w_in.astype(BF16),
                w_a1=jnp.pad(w_a1, ((0, 0), (0, LANES - GLA_GATE_RANK))).astype(BF16),
                w_a2=jnp.pad(w_a2, ((0, LANES - GLA_GATE_RANK), (0, 0))).astype(BF16),
                b_a=b_a.reshape(1, -1), head_norm=head_norm.reshape(1, -1), w_o=w_o.astype(BF16))


def _conv_kernel(x_ref, g_ref, w1_ref, b1_ref, wdw_ref, lng_ref, lnb_ref, o_ref, u_ref, c_ref):
    tm = TOK_TILE
    d = D_MODEL
    nbuf = CONV_HALO + tm

    @pl.when(pl.program_id(1) == 0)
    def _():
        u_ref[0, :, 0:CONV_HALO, :] = jnp.zeros((d // LANES, CONV_HALO, LANES), F32)

    @pl.when(pl.program_id(1) != 0)
    def _():
        u_ref[0, :, 0:CONV_HALO, :] = u_ref[0, :, tm:tm + CONV_HALO, :]

    hn = _rms_rows(x_ref[...], g_ref[...]).astype(BF16)
    y = _dot(hn, w1_ref[...]) + b1_ref[...]
    u = y[:, :d] * _sigmoid(y[:, d:])
    lead = CONV_HALO - (CONV_WIDTH - 1)
    groups = CONV_ROWS // SUBLANES
    tiles = nbuf // SUBLANES
    sub = lax.broadcasted_iota(jnp.int32, (tiles, SUBLANES, LANES), 1)

    for cb in range(d // LANES):
        cols = slice(cb * LANES, (cb + 1) * LANES)
        u_ref[0, cb, CONV_HALO:, :] = u[:, cols]
        full = u_ref[0, cb].reshape(tiles, SUBLANES, LANES)
        for s in range(1, SUBLANES):
            rot = pltpu.roll(full, SUBLANES - s, 1)
            nxt = jnp.concatenate([rot[1:], rot[:1]], axis=0)
            u_ref[s, cb] = jnp.where(sub < SUBLANES - s, rot, nxt).reshape(nbuf, LANES)

        def taps(c, carry):
            r0 = pl.multiple_of(c * CONV_ROWS, CONV_ROWS)
            acc = jnp.broadcast_to(wdw_ref[CONV_WIDTH, :, cols][None], (groups, SUBLANES, LANES))
            for j in range(CONV_WIDTH):
                off = lead + j
                start = pl.multiple_of(r0 + (off // SUBLANES) * SUBLANES, SUBLANES)
                uj = u_ref[off % SUBLANES, cb, pl.ds(start, CONV_ROWS), :].reshape(groups, SUBLANES, LANES)
                acc = acc + uj * wdw_ref[j, :, cols][None]
            c_ref[pl.ds(r0, CONV_ROWS), cols] = acc.reshape(CONV_ROWS, LANES)
            return carry

        lax.fori_loop(0, tm // CONV_ROWS, taps, 0)

    def norm(c, carry):
        r0 = pl.multiple_of(c * CONV_NORM_ROWS, CONV_NORM_ROWS)
        acc = c_ref[pl.ds(r0, CONV_NORM_ROWS), :]
        mu = jnp.mean(acc, axis=-1, keepdims=True)
        cen = acc - mu
        var = jnp.mean(cen * cen, axis=-1, keepdims=True)
        z = cen * lax.rsqrt(var + EPS) * lng_ref[...] + lnb_ref[...]
        o_ref[pl.ds(r0, CONV_NORM_ROWS), :] = (z * _sigmoid(z)).astype(BF16)
        return carry

    lax.fori_loop(0, tm // CONV_NORM_ROWS, norm, 0, unroll=8)


def _conv_layer(x, bsz, seq, g_mix, p, g_ffn, w1, w2):
    t, d = x.shape
    nt = seq // TOK_TILE
    blk = lambda b, i: (b * nt + i, 0)
    c = pl.pallas_call(
        _conv_kernel,
        grid=(bsz, nt),
        in_specs=[pl.BlockSpec((TOK_TILE, d), blk), _const_spec((1, d)), _const_spec((d, 2 * d)),
                  _const_spec((1, 2 * d)), _const_spec((CONV_WIDTH + 1, SUBLANES, d)),
                  _const_spec((1, d)), _const_spec((1, d))],
        out_specs=pl.BlockSpec((TOK_TILE, d), blk),
        out_shape=jax.ShapeDtypeStruct((t, d), BF16),
        scratch_shapes=[pltpu.VMEM((SUBLANES, d // LANES, CONV_HALO + TOK_TILE, LANES), F32),
                        pltpu.VMEM((TOK_TILE, d), F32)],
        compiler_params=_cparams("parallel", "arbitrary"),
        name="conv",
    )(x, g_mix, p["w_pw1"], p["b_pw1"], p["w_dw"], p["ln_g"], p["ln_b"])
    return _mlp_call(x, c, p["w_pw2"], p["b_pw2"], g_ffn, w1, w2)


def _conv_params(w_pw1, b_pw1, w_dw, b_dw, ln_g, ln_b, w_pw2, b_pw2):
    return dict(w_pw1=w_pw1.astype(BF16), b_pw1=b_pw1.reshape(1, -1),
                w_dw=jnp.broadcast_to(jnp.concatenate([w_dw, b_dw[None]], axis=0)[:, None, :],
                                      (CONV_WIDTH + 1, SUBLANES, w_dw.shape[1])),
                ln_g=ln_g.reshape(1, -1), ln_b=ln_b.reshape(1, -1),
                w_pw2=w_pw2.astype(BF16), b_pw2=b_pw2.reshape(1, -1))


def kernel(x, norm_mix, norm_ffn, mla_w_dq, mla_q_norm, mla_w_uq, mla_w_dkv, mla_kv_norm, mla_w_ukv, mla_q_gain, mla_k_gain, mla_w_o, mlstm_w_in, mlstm_w_if, mlstm_b_if, mlstm_head_norm, mlstm_w_o, gla_w_in, gla_w_a1, gla_w_a2, gla_b_a, gla_head_norm, gla_w_o, conv_w_pw1, conv_b_pw1, conv_w_dw, conv_b_dw, conv_ln_g, conv_ln_b, conv_w_pw2, conv_b_pw2, ffn_w1, ffn_w2):
    bsz, seq, d = x.shape
    depth = norm_mix.shape[0]
    assert d == D_MODEL and all(seq % tile == 0 for tile in (ATT_Q_TILE, TOK_TILE, PROJ_TILE, REC_TILE))
    h = x.reshape(bsz * seq, d)
    ffn_w1, ffn_w2 = ffn_w1.astype(BF16), ffn_w2.astype(BF16)
    for i in range(depth):
        kind, j = i % 4, i // 4
        g_mix = norm_mix[i].reshape(1, d)
        g_ffn = norm_ffn[i].reshape(1, d)
        w1, w2 = (ffn_w1, i), (ffn_w2, i)
        if kind == 0:
            p = _mla_params(seq, mla_w_dq[j], mla_q_norm[j], mla_w_uq[j], mla_w_dkv[j], mla_kv_norm[j],
                            mla_w_ukv[j], mla_q_gain[j], mla_k_gain[j], mla_w_o[j])
            h = _mla_layer(h, bsz, seq, g_mix, p, g_ffn, w1, w2)
        elif kind == 1:
            p = _mlstm_params(mlstm_w_in[j], mlstm_w_if[j], mlstm_b_if[j], mlstm_head_norm[j], mlstm_w_o[j])
            h = _mlstm_layer(h, bsz, seq, g_mix, p, g_ffn, w1, w2)
        elif kind == 2:
            p = _gla_params(gla_w_in[j], gla_w_a1[j], gla_w_a2[j], gla_b_a[j], gla_head_norm[j], gla_w_o[j])
            h = _gla_layer(h, bsz, seq, g_mix, p, g_ffn, w1, w2)
        else:
            p = _conv_params(conv_w_pw1[j], conv_b_pw1[j], conv_w_dw[j], conv_b_dw[j], conv_ln_g[j],
                             conv_ln_b[j], conv_w_pw2[j], conv_b_pw2[j])
            h = _conv_layer(h, bsz, seq, g_mix, p, g_ffn, w1, w2)
    return h.reshape(bsz, seq, d)
```

```python
import functools
import math

import numpy as np
import jax
import jax.numpy as jnp
from jax import lax
from jax.experimental import pallas as pl
from jax.experimental.pallas import tpu as pltpu

F32 = jnp.float32
BF16 = jnp.bfloat16

D_MODEL = 1024
D_FF = 4 * D_MODEL
EPS = 1e-6
CHUNK = 64

MLA_HEADS = 16
MLA_NOPE = 64
MLA_ROPE = 32
MLA_QK = MLA_NOPE + MLA_ROPE
MLA_V = 64
MLA_Q_RANK = 384
MLA_KV_RANK = 256
ROPE_BASE = 10000.0
MLA_VROWS = MLA_V + 16

ML_HEADS = 4
ML_QK = D_MODEL // 8
ML_V = D_MODEL // 4
GATE_CAP = 15.0

GLA_HEADS = 4
GLA_K = D_MODEL // 8
GLA_V = D_MODEL // 4
GLA_GATE_RANK = 16
GLA_TAU = 16.0

CONV_WIDTH = 31

LANES = 128
SUBLANES = 8
VMEM_LIMIT = 56 * 1024 * 1024

TOK_TILE = 512
MLP_TILE = 1024
PROJ_TILE = 1024
FF_TILE = 1024
ATT_TILE = 512
ATT_Q_TILE = 1024
ATT_HEADS = 4
REC_TILE = 256
CONV_HALO = 32
CONV_ROWS = 128
CONV_NORM_ROWS = 32
NEG_BIG = -1e30


def _cparams(*sem):
    return pltpu.CompilerParams(dimension_semantics=sem, vmem_limit_bytes=VMEM_LIMIT)


def _const_spec(shape):
    nd = len(shape)
    return pl.BlockSpec(shape, lambda *_: (0,) * nd, pipeline_mode=pl.Buffered(1))


def _dot(a, b):
    return jnp.dot(a, b, preferred_element_type=F32)


def _dot_nt(a, b):
    return lax.dot_general(a, b, (((1,), (1,)), ((), ())), preferred_element_type=F32)


def _dot_tn(a, b):
    return lax.dot_general(a, b, (((0,), (0,)), ((), ())), preferred_element_type=F32)


def _split2(a):
    hi = a.astype(BF16)
    lo = (a - hi.astype(F32)).astype(BF16)
    return hi, lo


def _split3(a):
    hi = a.astype(BF16)
    r = a - hi.astype(F32)
    mid = r.astype(BF16)
    lo = (r - mid.astype(F32)).astype(BF16)
    return hi, mid, lo


def _rms_rows(x, g):
    return x * lax.rsqrt(jnp.mean(x * x, axis=-1, keepdims=True) + EPS) * g


def _log_sigmoid(z):
    return jnp.minimum(z, 0.0) - jnp.log1p(jnp.exp(-jnp.abs(z)))


def _sigmoid(z):
    return 1.0 / (1.0 + jnp.exp(-z))


def _mlp_kernel(*refs, has_bias, a_transposed):
    if has_bias:
        x_ref, a_ref, wo_ref, bo_ref, g_ref, w1_ref, w2_ref, o_ref = refs
    else:
        x_ref, a_ref, wo_ref, g_ref, w1_ref, w2_ref, o_ref = refs
    if a_transposed:
        x1 = x_ref[...] + _dot_tn(a_ref[...], wo_ref[...])
    else:
        x1 = x_ref[...] + _dot(a_ref[...], wo_ref[...])
    if has_bias:
        x1 = x1 + bo_ref[...]
    hn = _rms_rows(x1, g_ref[...]).astype(BF16)
    acc = x1
    for c in range(D_FF // FF_TILE):
        h = _dot(hn, w1_ref[:, c * FF_TILE:(c + 1) * FF_TILE])
        h = jnp.maximum(h, 0.0)
        acc = acc + _dot((h * h).astype(BF16), w2_ref[c * FF_TILE:(c + 1) * FF_TILE, :])
    o_ref[...] = acc


def _mlp_call(x, a, w_o, b_o, g, w1, w2, a_transposed=False):
    t, d = x.shape
    din = w_o.shape[0]
    has_bias = b_o is not None
    (w1, layer), (w2, _) = w1, w2
    layer_spec = lambda shape: pl.BlockSpec((None,) + shape, lambda *_: (layer, 0, 0), pipeline_mode=pl.Buffered(1))
    row = lambda i: (i, 0)
    a_spec = pl.BlockSpec((din, MLP_TILE), lambda i: (0, i)) if a_transposed else pl.BlockSpec((MLP_TILE, din), row)
    in_specs = [pl.BlockSpec((MLP_TILE, d), row), a_spec, _const_spec((din, d))]
    args = [x, a, w_o]
    if has_bias:
        in_specs.append(_const_spec((1, d)))
        args.append(b_o)
    in_specs += [_const_spec((1, d)), layer_spec((d, D_FF)), layer_spec((D_FF, d))]
    args += [g, w1, w2]
    return pl.pallas_call(
        functools.partial(_mlp_kernel, has_bias=has_bias, a_transposed=a_transposed),
        grid=(t // MLP_TILE,),
        in_specs=in_specs,
        out_specs=pl.BlockSpec((MLP_TILE, d), row),
        out_shape=jax.ShapeDtypeStruct((t, d), F32),
        compiler_params=_cparams("parallel"),
        name="mlp",
    )(*args)


def _mla_proj_kernel(x_ref, g_ref, wdq_ref, qn_ref, wuqt_ref, wdkv_ref, kvn_ref, wukt_ref, wuvt_ref,
                     vone_ref, qc_ref, qsa_ref, qsb_ref, kc_ref, ksa_ref, ksb_ref,
                     qt_ref, k_ref, vt_ref):
    half = MLA_ROPE // 2
    hn = _rms_rows(x_ref[...], g_ref[...]).astype(BF16)
    cq = _rms_rows(_dot(hn, wdq_ref[...]), qn_ref[...]).astype(BF16)
    dkv = _dot(hn, wdkv_ref[...])
    ckv = _rms_rows(dkv[:, :MLA_KV_RANK], kvn_ref[...]).astype(BF16)

    vt_ref[0] = (_dot_nt(wuvt_ref[...], ckv) + vone_ref[...]).astype(BF16)

    n0, a0, b0, e0 = 0, MLA_NOPE, MLA_NOPE + half, MLA_QK

    def ssq(t):
        return jnp.sum(t * t, axis=0, keepdims=True)

    def rope(a, b, c, sa, sb):
        return a * c[a0:b0] + b * sa[a0:b0], b * c[b0:e0] + a * sb[b0:e0]

    def inv_rms(s):
        return lax.rsqrt(s * (1.0 / MLA_QK) + EPS)

    tm = x_ref.shape[0]
    pad_rows = jnp.zeros((LANES - MLA_QK, tm), F32)

    qt = _dot_nt(wuqt_ref[...], cq)
    qc, qsa, qsb = qc_ref[...], qsa_ref[...], qsb_ref[...]
    for h in range(MLA_HEADS):
        t = qt[h * MLA_QK:(h + 1) * MLA_QK]
        n, a, b = t[n0:a0], t[a0:b0], t[b0:e0]
        r = inv_rms(ssq(n) + ssq(a) + ssq(b))
        ra, rb = rope(a, b, qc, qsa, qsb)
        qt_ref[h * LANES:h * LANES + MLA_QK, :] = jnp.concatenate(
            [n * qc[n0:a0] * r, ra * r, rb * r], axis=0).astype(BF16)
        qt_ref[h * LANES + MLA_QK:(h + 1) * LANES, :] = pad_rows.astype(BF16)

    krt = dkv[:, MLA_KV_RANK:].T
    kt = _dot_nt(wukt_ref[...], ckv)
    kc, ksa, ksb = kc_ref[...], ksa_ref[...], ksb_ref[...]
    ka, kb = krt[a0:b0], krt[b0:e0]
    ssq_rope = ssq(ka) + ssq(kb)
    ra, rb = rope(ka, kb, kc, ksa, ksb)
    for h in range(MLA_HEADS):
        n = kt[h * MLA_NOPE:(h + 1) * MLA_NOPE]
        r = inv_rms(ssq(n) + ssq_rope)
        kh = jnp.concatenate([n * kc[n0:a0] * r, ra * r, rb * r, pad_rows], axis=0)
        k_ref[:, h * LANES:(h + 1) * LANES] = kh.T.astype(BF16)


def _attn_kernel(qt_ref, k_ref, vt_ref, o_ref, s_ref, acc_ref, m_ref, mb_ref):
    i = pl.program_id(2)
    tq, tk = ATT_Q_TILE, ATT_TILE
    nh = ATT_HEADS
    kpq = tq // tk
    assert kpq == 2
    vrows = vt_ref.shape[1] // nh
    krow = lax.broadcasted_iota(jnp.int32, (tk, tq), 0)
    qcol = lax.broadcasted_iota(jnp.int32, (tk, tq), 1)
    diag_mask = (krow // CHUNK) <= (qcol // CHUNK)
    late = slice(tk, tq)

    def produce(h, kb, qcols=slice(None), mask=None):
        start = pl.multiple_of(kb * tk, tk)
        st = _dot(k_ref[pl.ds(start, tk), h * LANES:(h + 1) * LANES], qt_ref[h * LANES:(h + 1) * LANES, qcols])
        if mask is not None:
            st = jnp.where(mask, st, NEG_BIG)
        s_ref[h, :, qcols] = st
        mb_ref[h, :, qcols] = jnp.max(st, axis=0, keepdims=True)

    def consume(h, kb, mask=None, qcols=slice(None)):
        st = s_ref[h, :, qcols]
        if mask is None:
            mblk = mb_ref[h, :, qcols]
        else:
            st = jnp.where(mask, st, NEG_BIG)
            mblk = jnp.max(st, axis=0, keepdims=True)
        m = m_ref[h, :, qcols]
        m_new = jnp.maximum(m, mblk)
        m_ref[h, :, qcols] = m_new
        pt = jnp.exp2(st - m_new).astype(BF16)
        acc_ref[h, :, qcols] = (jnp.exp2(m - m_new) * acc_ref[h, :, qcols]
                                + _dot(vt_ref[kb, h * vrows:(h + 1) * vrows, :], pt))

    def body(j, carry):
        for h in range(nh):
            produce((h + 1) % nh, j + (h + 1) // nh)
            consume(h, j)
        return carry

    m_ref[...] = jnp.full(m_ref.shape, NEG_BIG, F32)
    acc_ref[...] = jnp.zeros(acc_ref.shape, F32)
    first = kpq * i
    produce(0, 0)
    lax.fori_loop(0, i, lambda t, c: body(kpq * t + 1, body(kpq * t, c)), 0)
    late_mask = diag_mask[:, :tk]
    for h in range(nh):
        if h + 1 < nh:
            produce(h + 1, first, mask=diag_mask)
        else:
            produce(0, first + 1, late, late_mask)
        consume(h, first, diag_mask if h == 0 else None)
    for h in range(nh):
        if h + 1 < nh:
            produce(h + 1, first + 1, late, late_mask)
        consume(h, first + 1, None, late)
        acc = acc_ref[h]
        o_ref[h * MLA_V:(h + 1) * MLA_V, :] = (acc[:MLA_V] / acc[MLA_V:MLA_V + 1]).astype(BF16)


def _mla_layer(x, bsz, seq, g_mix, p, g_ffn, w1, w2):
    t, d = x.shape
    assert TOK_TILE == ATT_TILE
    hp = MLA_HEADS * LANES
    hv = MLA_HEADS * MLA_VROWS
    nt = seq // TOK_TILE
    row = lambda i: (i, 0)
    ttab = lambda i: (i % nt, 0)
    ftab = lambda i: (0, i % nt)
    qt, k, vt = pl.pallas_call(
        _mla_proj_kernel,
        grid=(t // TOK_TILE,),
        in_specs=[pl.BlockSpec((TOK_TILE, d), row), _const_spec((1, d)),
                  _const_spec((d, MLA_Q_RANK)), _const_spec((1, MLA_Q_RANK)),
                  _const_spec((MLA_HEADS * MLA_QK, MLA_Q_RANK)),
                  _const_spec((d, MLA_KV_RANK + LANES)), _const_spec((1, MLA_KV_RANK)),
                  _const_spec((MLA_HEADS * MLA_NOPE, MLA_KV_RANK)), _const_spec((hv, MLA_KV_RANK)),
                  _const_spec((hv, TOK_TILE))] + [pl.BlockSpec((LANES, TOK_TILE), ftab)] * 6,
        out_specs=[pl.BlockSpec((hp, TOK_TILE), lambda i: (0, i)), pl.BlockSpec((TOK_TILE, hp), row),
                   pl.BlockSpec((1, hv, TOK_TILE), lambda i: (i, 0, 0))],
        out_shape=[jax.ShapeDtypeStruct((hp, t), BF16), jax.ShapeDtypeStruct((t, hp), BF16),
                   jax.ShapeDtypeStruct((t // TOK_TILE, hv, TOK_TILE), BF16)],
        compiler_params=_cparams("parallel"),
        name="mla_proj",
    )(x, g_mix, p["w_dq"], p["q_norm"], p["w_uq_t"], p["w_dkv"], p["kv_norm"], p["w_uk_t"], p["w_uv_t"],
      p["vone"], p["qc"], p["qsa"], p["qsb"], p["kc"], p["ksa"], p["ksb"])

    nq = seq // ATT_Q_TILE
    ot = pl.pallas_call(
        _attn_kernel,
        grid=(bsz, MLA_HEADS // ATT_HEADS, nq),
        in_specs=[pl.BlockSpec((ATT_HEADS * LANES, ATT_Q_TILE), lambda b, h, i: (h, b * nq + i)),
                  pl.BlockSpec((seq, ATT_HEADS * LANES), lambda b, h, i: (b, h)),
                  pl.BlockSpec((seq // ATT_TILE, ATT_HEADS * MLA_VROWS, ATT_TILE), lambda b, h, i: (b, h, 0))],
        out_specs=pl.BlockSpec((ATT_HEADS * MLA_V, ATT_Q_TILE), lambda b, h, i: (h, b * nq + i)),
        out_shape=jax.ShapeDtypeStruct((MLA_HEADS * MLA_V, t), BF16),
        scratch_shapes=[pltpu.VMEM((ATT_HEADS, ATT_TILE, ATT_Q_TILE), F32),
                        pltpu.VMEM((ATT_HEADS, MLA_VROWS, ATT_Q_TILE), F32),
                        pltpu.VMEM((ATT_HEADS, 1, ATT_Q_TILE), F32), pltpu.VMEM((ATT_HEADS, 1, ATT_Q_TILE), F32)],
        compiler_params=_cparams("parallel", "parallel", "arbitrary"),
        name="mla_attn",
    )(qt, k, vt)
    return _mlp_call(x, ot, p["w_o"], None, g_ffn, w1, w2, a_transposed=True)


def _mla_params(seq, w_dq, q_norm, w_uq, w_dkv, kv_norm, w_ukv, q_gain, k_gain, w_o):
    pad_head = LANES - MLA_QK
    half = MLA_ROPE // 2
    w_ukv_r = w_ukv.reshape(MLA_KV_RANK, MLA_HEADS, MLA_NOPE + MLA_V)
    w_uv_p = jnp.pad(w_ukv_r[:, :, MLA_NOPE:], ((0, 0), (0, 0), (0, MLA_VROWS - MLA_V)))
    w_rope = jnp.pad(w_dkv[:, MLA_KV_RANK:], ((0, 0), (MLA_NOPE, pad_head)))
    w_dkv_p = jnp.concatenate([w_dkv[:, :MLA_KV_RANK], w_rope], axis=1)
    vone = np.zeros((MLA_HEADS * MLA_VROWS, TOK_TILE), np.float32)
    vone[np.arange(MLA_HEADS) * MLA_VROWS + MLA_V] = 1.0
    gq = q_gain * ((MLA_QK ** -0.5) * math.log2(math.e))
    inv_freq = ROPE_BASE ** (-jnp.arange(half, dtype=F32) / half)
    ang = jnp.arange(seq).astype(F32)[:, None] * inv_freq[None, :]
    cos, sin = jnp.cos(ang), jnp.sin(ang)
    g1 = lambda g: g[MLA_NOPE:MLA_NOPE + half][None, :]
    g2 = lambda g: g[MLA_NOPE + half:MLA_QK][None, :]
    z = lambda n: jnp.zeros((seq, n), F32)

    def tables(g):
        nope = jnp.broadcast_to(g[None, :MLA_NOPE], (seq, MLA_NOPE))
        c = jnp.concatenate([nope, g1(g) * cos, g2(g) * cos, z(pad_head)], axis=1)
        sa = jnp.concatenate([z(MLA_NOPE), -g2(g) * sin, z(half + pad_head)], axis=1)
        sb = jnp.concatenate([z(MLA_NOPE + half), g1(g) * sin, z(pad_head)], axis=1)
        return c.T, sa.T, sb.T

    qc, qsa, qsb = tables(gq)
    kc, ksa, ksb = tables(k_gain)
    return dict(w_dq=w_dq.astype(BF16), q_norm=q_norm.reshape(1, -1),
                w_uq_t=w_uq.T.astype(BF16), w_dkv=w_dkv_p.astype(BF16), kv_norm=kv_norm.reshape(1, -1),
                w_uk_t=w_ukv_r[:, :, :MLA_NOPE].reshape(MLA_KV_RANK, -1).T.astype(BF16),
                w_uv_t=w_uv_p.reshape(MLA_KV_RANK, -1).T.astype(BF16), vone=jnp.asarray(vone, BF16),
                qc=qc, qsa=qsa, qsb=qsb, kc=kc, ksa=ksa, ksb=ksb, w_o=w_o.astype(BF16))


def _mlstm_proj_kernel(x_ref, g_ref, wt_ref, wk_ref, wr_hi_ref, br_ref,
                       qt_ref, k_ref, vt_ref, ot_ref, gr_ref):
    hn = _rms_rows(x_ref[...], g_ref[...])
    hi, lo = _split2(hn)
    hq, hv = ML_HEADS * ML_QK, ML_HEADS * ML_V
    yt = _dot_nt(wt_ref[...], hi)
    qt_ref[...] = yt[:hq].astype(BF16)
    vt_ref[...] = yt[hq:hq + hv].astype(BF16)
    ot_ref[...] = yt[hq + hv:hq + 2 * hv].astype(BF16)
    k_ref[...] = (_dot(hi, wk_ref[...]) * (ML_QK ** -0.5)).astype(BF16)
    g0 = hq + 2 * hv
    gr_ref[...] = (yt[g0:g0 + SUBLANES] + yt[g0 + SUBLANES:] + _dot_nt(wr_hi_ref[...], lo)) + br_ref[...]


def _cap(g):
    return GATE_CAP * jnp.tanh(g * (1.0 / GATE_CAP))


def _mlstm_rec_kernel(qt_ref, k_ref, vt_ref, opt_ref, gr_ref, hnorm_ref, triu_ref, sel_ref,
                      o_ref, c_ref, m_ref):
    L = REC_TILE

    @pl.when(pl.program_id(1) == 0)
    def _():
        c_ref[...] = jnp.zeros_like(c_ref)
        m_ref[...] = jnp.zeros_like(m_ref)

    gr = _cap(gr_ref[...])
    r1, r2, r3 = _split3(_log_sigmoid(gr))
    triu = triu_ref[...]
    bcum_r = _dot(r1, triu) + _dot(r2, triu) + _dot(r3, triu)
    grow = lax.broadcasted_iota(jnp.int32, (SUBLANES, L), 0)
    xr = jnp.concatenate([jnp.where(grow < ML_HEADS, gr, bcum_r), jnp.zeros((LANES - SUBLANES, L), F32)], axis=0)
    x1, x2, x3 = _split3(xr)
    sel = sel_ref[...]
    cb_all = _dot_tn(x1, sel) + _dot_tn(x2, sel) + _dot_tn(x3, sel)
    src = lax.broadcasted_iota(jnp.int32, (L, L), 0)
    tgt = lax.broadcasted_iota(jnp.int32, (L, L), 1)
    causal = src <= tgt
    ones = jnp.ones((LANES, L), BF16)

    for h in range(ML_HEADS):
        fh = ML_HEADS + h
        cb = cb_all[:, h * L:(h + 1) * L]
        bt = bcum_r[fh:fh + 1, :]
        li_r = gr[h:h + 1, :]
        m_prev = m_ref[h:h + 1, 0:1]
        qt = qt_ref[h * ML_QK:(h + 1) * ML_QK, :]
        k = k_ref[:, h * ML_QK:(h + 1) * ML_QK]
        vaug = jnp.concatenate([vt_ref[h * ML_V:(h + 1) * ML_V, :], ones], axis=0)
        c_st = c_ref[h]

        dmat = jnp.where(causal, cb + bt, NEG_BIG)
        inter = bt + m_prev
        m_t = jnp.maximum(inter, jnp.max(dmat, axis=0, keepdims=True))
        w_intra = jnp.exp(dmat - m_t)
        w_inter = jnp.exp(inter - m_t)
        pt = (_dot(k, qt) * w_intra).astype(BF16)
        nd = _dot(vaug, pt) + w_inter * _dot(c_st.astype(BF16), qt)
        den = jnp.maximum(jnp.abs(nd[ML_V:ML_V + 1]), jnp.exp(-m_t))
        hc = nd[:ML_V] / den

        b_last = bt[:, L - 1:L]
        m_new = jnp.maximum(b_last + m_prev, jnp.max(b_last - bt + li_r, axis=-1, keepdims=True))
        ws = jnp.exp(cb[:, :ML_QK] + (b_last - m_new))
        wc = jnp.exp(b_last + m_prev - m_new)
        kw = (k.astype(F32) * ws).astype(BF16)
        c_ref[h] = wc * c_st + _dot(vaug, kw)
        m_ref[h:h + 1, :] = jnp.broadcast_to(m_new, (1, LANES))

        rows = slice(h * ML_V, (h + 1) * ML_V)
        hs = hc * lax.rsqrt(jnp.mean(hc * hc, axis=0, keepdims=True) + EPS) * hnorm_ref[rows, :]
        o_ref[rows, :] = (_sigmoid(opt_ref[rows, :].astype(F32)) * hs).astype(BF16)


def _mlstm_consts(n):
    r = np.arange(n)
    triu = (r[:, None] <= r[None, :]).astype(np.float32)
    sel = np.zeros((LANES, ML_HEADS * n), np.float32)
    for h in range(ML_HEADS):
        sel[h, h * n:(h + 1) * n] = 1.0
        sel[ML_HEADS + h, h * n:(h + 1) * n] = -1.0
    return jnp.asarray(triu, BF16), jnp.asarray(sel, BF16)


def _mlstm_layer(x, bsz, seq, g_mix, p, g_ffn, w1, w2):
    t, d = x.shape
    hq, hv = ML_HEADS * ML_QK, ML_HEADS * ML_V
    row = lambda i: (i, 0)
    col = lambda i: (0, i)
    qt, k, vt, opt, grow = pl.pallas_call(
        _mlstm_proj_kernel,
        grid=(t // PROJ_TILE,),
        in_specs=[pl.BlockSpec((PROJ_TILE, d), row), _const_spec((1, d)),
                  _const_spec((hq + 2 * hv + 2 * SUBLANES, d)),
                  _const_spec((d, hq)), _const_spec((SUBLANES, d)), _const_spec((SUBLANES, 1))],
        out_specs=[pl.BlockSpec((hq, PROJ_TILE), col), pl.BlockSpec((PROJ_TILE, hq), row),
                   pl.BlockSpec((hv, PROJ_TILE), col), pl.BlockSpec((hv, PROJ_TILE), col),
                   pl.BlockSpec((SUBLANES, PROJ_TILE), col)],
        out_shape=[jax.ShapeDtypeStruct((hq, t), BF16), jax.ShapeDtypeStruct((t, hq), BF16),
                   jax.ShapeDtypeStruct((hv, t), BF16), jax.ShapeDtypeStruct((hv, t), BF16),
                   jax.ShapeDtypeStruct((SUBLANES, t), F32)],
        compiler_params=_cparams("parallel"),
        name="mlstm_proj",
    )(x, g_mix, p["w_t"], p["w_k"], p["wr_hi"], p["b_row"])

    nc = seq // REC_TILE
    triu, sel = _mlstm_consts(REC_TILE)
    hnorm = jnp.broadcast_to(p["head_norm"].reshape(hv, 1), (hv, REC_TILE))
    rblk = lambda b, c: (b * nc + c, 0)
    cblk = lambda b, c: (0, b * nc + c)
    at = pl.pallas_call(
        _mlstm_rec_kernel,
        grid=(bsz, nc),
        in_specs=[pl.BlockSpec((hq, REC_TILE), cblk), pl.BlockSpec((REC_TILE, hq), rblk),
                  pl.BlockSpec((hv, REC_TILE), cblk), pl.BlockSpec((hv, REC_TILE), cblk),
                  pl.BlockSpec((SUBLANES, REC_TILE), cblk),
                  _const_spec((hv, REC_TILE)), _const_spec((REC_TILE, REC_TILE)),
                  _const_spec((LANES, ML_HEADS * REC_TILE))],
        out_specs=pl.BlockSpec((hv, REC_TILE), cblk),
        out_shape=jax.ShapeDtypeStruct((hv, t), BF16),
        scratch_shapes=[pltpu.VMEM((ML_HEADS, ML_V + LANES, ML_QK), F32), pltpu.VMEM((SUBLANES, LANES), F32)],
        compiler_params=_cparams("parallel", "arbitrary"),
        name="mlstm_rec",
    )(qt, k, vt, opt, grow, hnorm, triu, sel)
    return _mlp_call(x, at, p["w_o"], None, g_ffn, w1, w2, a_transposed=True)


def _mlstm_params(w_in, w_if, b_if, head_norm, w_o):
    ng = 2 * ML_HEADS
    hq = ML_HEADS * ML_QK
    w_row = w_if.T
    wr_hi = w_row.astype(BF16)
    wr_lo = (w_row - wr_hi.astype(F32)).astype(BF16)
    w_t = jnp.concatenate([w_in[:, :hq], w_in[:, 2 * hq:]], axis=1).T.astype(BF16)
    return dict(w_t=jnp.concatenate([w_t, wr_hi, wr_lo], axis=0), w_k=w_in[:, hq:2 * hq].astype(BF16),
                wr_hi=wr_hi, b_row=b_if.reshape(ng, 1),
                head_norm=head_norm, w_o=w_o.astype(BF16))


def _gla_proj_kernel(x_ref, g_ref, win_ref, wa1_ref, wa2_ref, ba_ref, q_ref, k_ref, v_ref, r_ref, la_ref):
    hn = _rms_rows(x_ref[...], g_ref[...]).astype(BF16)
    y = _dot(hn, win_ref[...])
    hk, hv = GLA_HEADS * GLA_K, GLA_HEADS * GLA_V
    q_ref[...] = (y[:, :hk] * (GLA_K ** -0.5)).astype(BF16)
    k_ref[...] = y[:, hk:2 * hk].astype(BF16)
    v_ref[...] = y[:, 2 * hk:2 * hk + hv].astype(BF16)
    r_ref[...] = y[:, 2 * hk + hv:].astype(BF16)
    z = _dot(_dot(hn, wa1_ref[...]).astype(BF16), wa2_ref[...]) + ba_ref[...]
    la_ref[...] = _log_sigmoid(z) * (1.0 / GLA_TAU)


def _gla_levels():
    return [2 ** j for j in range(1, int(math.log2(REC_TILE)) + 1)]


def _gla_consts():
    n = REC_TILE
    t = np.arange(n)[:, None]
    u = np.arange(n)[None, :]
    mats = []
    for p in _gla_levels():
        first_upper = (t // p) * p + p // 2
        upper = (t % p) >= p // 2
        m_up = (u > first_upper) & (u <= t)
        m_lo = (u > t) & (u <= first_upper)
        mats.append(np.where(upper, m_up, m_lo))
    mats.append(u <= t)
    mats.append(u > t)
    return jnp.asarray(np.stack(mats).astype(np.float32), BF16)


def _gla_rec_kernel(q_ref, k_ref, v_ref, r_ref, la_ref, hnorm_ref, w_ref, o_ref, c_ref):
    L = REC_TILE
    levels = _gla_levels()
    nl = len(levels)

    @pl.when(pl.program_id(1) == 0)
    def _():
        c_ref[...] = jnp.zeros_like(c_ref)

    la_hi, la_lo = _split2(la_ref[...])

    def decay(j, two_term):
        w = w_ref[j]
        x = _dot(w, la_hi)
        if two_term:
            x = x + _dot(w, la_lo)
        return jnp.exp(x)

    row = lax.broadcasted_iota(jnp.int32, (L, L), 0)
    col = lax.broadcasted_iota(jnp.int32, (L, L), 1)
    row_xor_col = row ^ col
    rowk = lax.broadcasted_iota(jnp.int32, (L, GLA_K), 0)
    e_cum = decay(nl, True)
    e_rev = decay(nl + 1, True)
    e_lvl = [decay(j, False) for j in range(nl)]

    for h in range(GLA_HEADS):
        ks = slice(h * GLA_K, (h + 1) * GLA_K)
        vs = slice(h * GLA_V, (h + 1) * GLA_V)
        q = q_ref[:, ks].astype(F32)
        k = k_ref[:, ks].astype(F32)
        v = v_ref[:, vs]
        att = jnp.where(row == col, _dot_nt(q_ref[:, ks], k_ref[:, ks]), 0.0)
        for j, p in enumerate(levels):
            upper = (rowk & (p - 1)) >= (p // 2)
            e = e_lvl[j][:, ks]
            qf = jnp.where(upper, q * e, 0.0).astype(BF16)
            kf = jnp.where(upper, 0.0, k * e).astype(BF16)
            a = _dot_nt(qf, kf)
            if p < L:
                a = jnp.where(row_xor_col < p, a, 0.0)
            att = att + a
        c_st = c_ref[h]
        o = _dot(att.astype(BF16), v) + _dot_nt((q * e_cum[:, ks]).astype(BF16), c_st.astype(BF16))
        c_ref[h] = e_cum[L - 1:L, ks] * c_st + _dot_tn(v, (k * e_rev[:, ks]).astype(BF16))
        o = _rms_rows(o, hnorm_ref[:, vs])
        rr = r_ref[:, vs].astype(F32)
        o_ref[:, vs] = (o * rr * _sigmoid(rr)).astype(BF16)


def _gla_layer(x, bsz, seq, g_mix, p, g_ffn, w1, w2):
    t, d = x.shape
    hk, hv = GLA_HEADS * GLA_K, GLA_HEADS * GLA_V
    row = lambda i: (i, 0)
    q, k, v, r, la = pl.pallas_call(
        _gla_proj_kernel,
        grid=(t // PROJ_TILE,),
        in_specs=[pl.BlockSpec((PROJ_TILE, d), row), _const_spec((1, d)), _const_spec((d, 2 * hk + 2 * hv)),
                  _const_spec((d, LANES)), _const_spec((LANES, hk)), _const_spec((1, hk))],
        out_specs=[pl.BlockSpec((PROJ_TILE, hk), row), pl.BlockSpec((PROJ_TILE, hk), row),
                   pl.BlockSpec((PROJ_TILE, hv), row), pl.BlockSpec((PROJ_TILE, hv), row),
                   pl.BlockSpec((PROJ_TILE, hk), row)],
        out_shape=[jax.ShapeDtypeStruct((t, hk), BF16), jax.ShapeDtypeStruct((t, hk), BF16),
                   jax.ShapeDtypeStruct((t, hv), BF16), jax.ShapeDtypeStruct((t, hv), BF16),
                   jax.ShapeDtypeStruct((t, hk), F32)],
        compiler_params=_cparams("parallel"),
        name="gla_proj",
    )(x, g_mix, p["w_in"], p["w_a1"], p["w_a2"], p["b_a"])

    nc = seq // REC_TILE
    wmats = _gla_consts()
    blk = lambda b, c: (b * nc + c, 0)
    a = pl.pallas_call(
        _gla_rec_kernel,
        grid=(bsz, nc),
        in_specs=[pl.BlockSpec((REC_TILE, hk), blk), pl.BlockSpec((REC_TILE, hk), blk),
                  pl.BlockSpec((REC_TILE, hv), blk), pl.BlockSpec((REC_TILE, hv), blk),
                  pl.BlockSpec((REC_TILE, hk), blk), _const_spec((1, hv)), _const_spec(wmats.shape)],
        out_specs=pl.BlockSpec((REC_TILE, hv), blk),
        out_shape=jax.ShapeDtypeStruct((t, hv), BF16),
        scratch_shapes=[pltpu.VMEM((GLA_HEADS, GLA_V, GLA_K), F32)],
        compiler_params=_cparams("parallel", "arbitrary"),
        name="gla_rec",
    )(q, k, v, r, la, p["head_norm"], wmats)
    return _mlp_call(x, a, p["w_o"], None, g_ffn, w1, w2)


def _gla_params(w_in, w_a1, w_a2, b_a, head_norm, w_o):
    return dict(w_in=w_in.astype(BF16),
                w_a1=jnp.pad(w_a1, ((0, 0), (0, LANES - GLA_GATE_RANK))).astype(BF16),
                w_a2=jnp.pad(w_a2, ((0, LANES - GLA_GATE_RANK), (0, 0))).astype(BF16),
                b_a=b_a.reshape(1, -1), head_norm=head_norm.reshape(1, -1), w_o=w_o.astype(BF16))


def _conv_kernel(x_ref, g_ref, w1_ref, b1_ref, wdw_ref, lng_ref, lnb_ref, o_ref, u_ref, c_ref):
    tm = TOK_TILE
    d = D_MODEL
    nbuf = CONV_HALO + tm

    @pl.when(pl.program_id(1) == 0)
    def _():
        u_ref[0, :, 0:CONV_HALO, :] = jnp.zeros((d // LANES, CONV_HALO, LANES), F32)

    @pl.when(pl.program_id(1) != 0)
    def _():
        u_ref[0, :, 0:CONV_HALO, :] = u_ref[0, :, tm:tm + CONV_HALO, :]

    hn = _rms_rows(x_ref[...], g_ref[...]).astype(BF16)
    lead = CONV_HALO - (CONV_WIDTH - 1)
    groups = CONV_ROWS // SUBLANES
    tiles = nbuf // SUBLANES
    sub = lax.broadcasted_iota(jnp.int32, (tiles, SUBLANES, LANES), 1)

    for cb in range(d // LANES):
        pair = slice(2 * cb * LANES, 2 * (cb + 1) * LANES)
        y = _dot(hn, w1_ref[:, pair]) + b1_ref[:, pair]
        u_ref[0, cb, CONV_HALO:, :] = y[:, :LANES] * _sigmoid(y[:, LANES:])
        full = u_ref[0, cb].reshape(tiles, SUBLANES, LANES)
        for s in range(1, SUBLANES):
            rot = pltpu.roll(full, SUBLANES - s, 1)
            nxt = jnp.concatenate([rot[1:], rot[:1]], axis=0)
            u_ref[s, cb] = jnp.where(sub < SUBLANES - s, rot, nxt).reshape(nbuf, LANES)

    for cb in range(d // LANES):
        cols = slice(cb * LANES, (cb + 1) * LANES)

        def taps(c, carry):
            r0 = pl.multiple_of(c * CONV_ROWS, CONV_ROWS)
            acc = jnp.broadcast_to(wdw_ref[CONV_WIDTH, :, cols][None], (groups, SUBLANES, LANES))
            for j in range(CONV_WIDTH):
                off = lead + j
                start = pl.multiple_of(r0 + (off // SUBLANES) * SUBLANES, SUBLANES)
                uj = u_ref[off % SUBLANES, cb, pl.ds(start, CONV_ROWS), :].reshape(groups, SUBLANES, LANES)
                acc = acc + uj * wdw_ref[j, :, cols][None]
            c_ref[pl.ds(r0, CONV_ROWS), cols] = acc.reshape(CONV_ROWS, LANES)
            return carry

        lax.fori_loop(0, tm // CONV_ROWS, taps, 0)

    def norm(c, carry):
        r0 = pl.multiple_of(c * CONV_NORM_ROWS, CONV_NORM_ROWS)
        acc = c_ref[pl.ds(r0, CONV_NORM_ROWS), :]
        mu = jnp.mean(acc, axis=-1, keepdims=True)
        cen = acc - mu
        var = jnp.mean(cen * cen, axis=-1, keepdims=True)
        z = cen * lax.rsqrt(var + EPS) * lng_ref[...] + lnb_ref[...]
        o_ref[pl.ds(r0, CONV_NORM_ROWS), :] = (z * _sigmoid(z)).astype(BF16)
        return carry

    lax.fori_loop(0, tm // CONV_NORM_ROWS, norm, 0, unroll=8)


def _conv_layer(x, bsz, seq, g_mix, p, g_ffn, w1, w2):
    t, d = x.shape
    nt = seq // TOK_TILE
    blk = lambda b, i: (b * nt + i, 0)
    c = pl.pallas_call(
        _conv_kernel,
        grid=(bsz, nt),
        in_specs=[pl.BlockSpec((TOK_TILE, d), blk), _const_spec((1, d)), _const_spec((d, 2 * d)),
                  _const_spec((1, 2 * d)), _const_spec((CONV_WIDTH + 1, SUBLANES, d)),
                  _const_spec((1, d)), _const_spec((1, d))],
        out_specs=pl.BlockSpec((TOK_TILE, d), blk),
        out_shape=jax.ShapeDtypeStruct((t, d), BF16),
        scratch_shapes=[pltpu.VMEM((SUBLANES, d // LANES, CONV_HALO + TOK_TILE, LANES), F32),
                        pltpu.VMEM((TOK_TILE, d), F32)],
        compiler_params=_cparams("parallel", "arbitrary"),
        name="conv",
    )(x, g_mix, p["w_pw1"], p["b_pw1"], p["w_dw"], p["ln_g"], p["ln_b"])
    return _mlp_call(x, c, p["w_pw2"], p["b_pw2"], g_ffn, w1, w2)


def _conv_params(w_pw1, b_pw1, w_dw, b_dw, ln_g, ln_b, w_pw2, b_pw2):
    d = w_pw1.shape[0]

    def pair_up(w):
        return w.reshape(-1, 2, d // LANES, LANES).transpose(0, 2, 1, 3).reshape(-1, 2 * d)

    return dict(w_pw1=pair_up(w_pw1).astype(BF16), b_pw1=pair_up(b_pw1.reshape(1, -1)),
                w_dw=jnp.broadcast_to(jnp.concatenate([w_dw, b_dw[None]], axis=0)[:, None, :],
                                      (CONV_WIDTH + 1, SUBLANES, w_dw.shape[1])),
                ln_g=ln_g.reshape(1, -1), ln_b=ln_b.reshape(1, -1),
                w_pw2=w_pw2.astype(BF16), b_pw2=b_pw2.reshape(1, -1))


def kernel(x, norm_mix, norm_ffn, mla_w_dq, mla_q_norm, mla_w_uq, mla_w_dkv, mla_kv_norm, mla_w_ukv, mla_q_gain, mla_k_gain, mla_w_o, mlstm_w_in, mlstm_w_if, mlstm_b_if, mlstm_head_norm, mlstm_w_o, gla_w_in, gla_w_a1, gla_w_a2, gla_b_a, gla_head_norm, gla_w_o, conv_w_pw1, conv_b_pw1, conv_w_dw, conv_b_dw, conv_ln_g, conv_ln_b, conv_w_pw2, conv_b_pw2, ffn_w1, ffn_w2):
    bsz, seq, d = x.shape
    depth = norm_mix.shape[0]
    assert d == D_MODEL and all(seq % tile == 0 for tile in (ATT_Q_TILE, TOK_TILE, PROJ_TILE, REC_TILE))
    h = x.reshape(bsz * seq, d)
    ffn_w1, ffn_w2 = ffn_w1.astype(BF16), ffn_w2.astype(BF16)
    for i in range(depth):
        kind, j = i % 4, i // 4
        g_mix = norm_mix[i].reshape(1, d)
        g_ffn = norm_ffn[i].reshape(1, d)
        w1, w2 = (ffn_w1, i), (ffn_w2, i)
        if kind == 0:
            p = _mla_params(seq, mla_w_dq[j], mla_q_norm[j], mla_w_uq[j], mla_w_dkv[j], mla_kv_norm[j],
                            mla_w_ukv[j], mla_q_gain[j], mla_k_gain[j], mla_w_o[j])
            h = _mla_layer(h, bsz, seq, g_mix, p, g_ffn, w1, w2)
        elif kind == 1:
            p = _mlstm_params(mlstm_w_in[j], mlstm_w_if[j], mlstm_b_if[j], mlstm_head_norm[j], mlstm_w_o[j])
            h = _mlstm_layer(h, bsz, seq, g_mix, p, g_ffn, w1, w2)
        elif kind == 2:
            p = _gla_params(gla_w_in[j], gla_w_a1[j], gla_w_a2[j], gla_b_a[j], gla_head_norm[j], gla_w_o[j])
            h = _gla_layer(h, bsz, seq, g_mix, p, g_ffn, w1, w2)
        else:
            p = _conv_params(conv_w_pw1[j], conv_b_pw1[j], conv_w_dw[j], conv_b_dw[j], conv_ln_g[j],
                             conv_ln_b[j], conv_w_pw2[j], conv_b_pw2[j])
            h = _conv_layer(h, bsz, seq, g_mix, p, g_ffn, w1, w2)
    return h.reshape(bsz, seq, d)
```

```python
import functools
import math

import numpy as np
import jax
import jax.numpy as jnp
from jax import lax
from jax.experimental import pallas as pl
from jax.experimental.pallas import tpu as pltpu

F32 = jnp.float32
BF16 = jnp.bfloat16

D_MODEL = 1024
D_FF = 4 * D_MODEL
EPS = 1e-6
CHUNK = 64

MLA_HEADS = 16
MLA_NOPE = 64
MLA_ROPE = 32
MLA_QK = MLA_NOPE + MLA_ROPE
MLA_V = 64
MLA_Q_RANK = 384
MLA_KV_RANK = 256
ROPE_BASE = 10000.0
MLA_VROWS = MLA_V + 16

ML_HEADS = 4
ML_QK = D_MODEL // 8
ML_V = D_MODEL // 4
GATE_CAP = 15.0

GLA_HEADS = 4
GLA_K = D_MODEL // 8
GLA_V = D_MODEL // 4
GLA_GATE_RANK = 16
GLA_TAU = 16.0

CONV_WIDTH = 31

LANES = 128
SUBLANES = 8
VMEM_LIMIT = 56 * 1024 * 1024

TOK_TILE = 512
MLP_TILE = 1024
PROJ_TILE = 1024
FF_TILE = 1024
ATT_TILE = 512
ATT_Q_TILE = 1024
ATT_HEADS = 4
REC_TILE = 256
CONV_HALO = 32
CONV_ROWS = 128
CONV_NORM_ROWS = 32
NEG_BIG = -1e30


def _cparams(*sem):
    return pltpu.CompilerParams(dimension_semantics=sem, vmem_limit_bytes=VMEM_LIMIT)


def _const_spec(shape):
    nd = len(shape)
    return pl.BlockSpec(shape, lambda *_: (0,) * nd, pipeline_mode=pl.Buffered(1))


def _dot(a, b):
    return jnp.dot(a, b, preferred_element_type=F32)


def _dot_nt(a, b):
    return lax.dot_general(a, b, (((1,), (1,)), ((), ())), preferred_element_type=F32)


def _dot_tn(a, b):
    return lax.dot_general(a, b, (((0,), (0,)), ((), ())), preferred_element_type=F32)


def _split2(a):
    hi = a.astype(BF16)
    lo = (a - hi.astype(F32)).astype(BF16)
    return hi, lo


def _split3(a):
    hi = a.astype(BF16)
    r = a - hi.astype(F32)
    mid = r.astype(BF16)
    lo = (r - mid.astype(F32)).astype(BF16)
    return hi, mid, lo


def _rms_rows(x, g):
    return x * lax.rsqrt(jnp.mean(x * x, axis=-1, keepdims=True) + EPS) * g


def _log_sigmoid(z):
    return jnp.minimum(z, 0.0) - jnp.log1p(jnp.exp(-jnp.abs(z)))


def _sigmoid(z):
    return 1.0 / (1.0 + jnp.exp(-z))


def _mlp_kernel(*refs, has_bias, a_transposed):
    if has_bias:
        x_ref, a_ref, wo_ref, bo_ref, g_ref, w1_ref, w2_ref, o_ref = refs
    else:
        x_ref, a_ref, wo_ref, g_ref, w1_ref, w2_ref, o_ref = refs
    if a_transposed:
        x1 = x_ref[...] + _dot_tn(a_ref[...], wo_ref[...])
    else:
        x1 = x_ref[...] + _dot(a_ref[...], wo_ref[...])
    if has_bias:
        x1 = x1 + bo_ref[...]
    hn = _rms_rows(x1, g_ref[...]).astype(BF16)
    acc = x1
    for c in range(D_FF // FF_TILE):
        h = _dot(hn, w1_ref[:, c * FF_TILE:(c + 1) * FF_TILE])
        h = jnp.maximum(h, 0.0)
        acc = acc + _dot((h * h).astype(BF16), w2_ref[c * FF_TILE:(c + 1) * FF_TILE, :])
    o_ref[...] = acc


def _mlp_call(x, a, w_o, b_o, g, w1, w2, a_transposed=False):
    t, d = x.shape
    din = w_o.shape[0]
    has_bias = b_o is not None
    (w1, layer), (w2, _) = w1, w2
    layer_spec = lambda shape: pl.BlockSpec((None,) + shape, lambda *_: (layer, 0, 0), pipeline_mode=pl.Buffered(1))
    row = lambda i: (i, 0)
    a_spec = pl.BlockSpec((din, MLP_TILE), lambda i: (0, i)) if a_transposed else pl.BlockSpec((MLP_TILE, din), row)
    in_specs = [pl.BlockSpec((MLP_TILE, d), row), a_spec, _const_spec((din, d))]
    args = [x, a, w_o]
    if has_bias:
        in_specs.append(_const_spec((1, d)))
        args.append(b_o)
    in_specs += [_const_spec((1, d)), layer_spec((d, D_FF)), layer_spec((D_FF, d))]
    args += [g, w1, w2]
    return pl.pallas_call(
        functools.partial(_mlp_kernel, has_bias=has_bias, a_transposed=a_transposed),
        grid=(t // MLP_TILE,),
        in_specs=in_specs,
        out_specs=pl.BlockSpec((MLP_TILE, d), row),
        out_shape=jax.ShapeDtypeStruct((t, d), F32),
        compiler_params=_cparams("parallel"),
        name="mlp",
    )(*args)


def _mla_proj_kernel(x_ref, g_ref, wdq_ref, qn_ref, wuqt_ref, wdkv_ref, kvn_ref, wukt_ref, wuvt_ref,
                     vone_ref, qc_ref, qsa_ref, qsb_ref, kc_ref, ksa_ref, ksb_ref,
                     qt_ref, k_ref, vt_ref):
    half = MLA_ROPE // 2
    hn = _rms_rows(x_ref[...], g_ref[...]).astype(BF16)
    cq = _rms_rows(_dot(hn, wdq_ref[...]), qn_ref[...]).astype(BF16)
    dkv = _dot(hn, wdkv_ref[...])
    ckv = _rms_rows(dkv[:, :MLA_KV_RANK], kvn_ref[...]).astype(BF16)

    vt_ref[0] = (_dot_nt(wuvt_ref[...], ckv) + vone_ref[...]).astype(BF16)

    n0, a0, b0, e0 = 0, MLA_NOPE, MLA_NOPE + half, MLA_QK

    def ssq(t):
        return jnp.sum(t * t, axis=0, keepdims=True)

    def rope(a, b, c, sa, sb):
        return a * c[a0:b0] + b * sa[a0:b0], b * c[b0:e0] + a * sb[b0:e0]

    def inv_rms(s):
        return lax.rsqrt(s * (1.0 / MLA_QK) + EPS)

    tm = x_ref.shape[0]
    pad_rows = jnp.zeros((LANES - MLA_QK, tm), F32)

    qt = _dot_nt(wuqt_ref[...], cq)
    qc, qsa, qsb = qc_ref[...], qsa_ref[...], qsb_ref[...]
    for h in range(MLA_HEADS):
        t = qt[h * MLA_QK:(h + 1) * MLA_QK]
        n, a, b = t[n0:a0], t[a0:b0], t[b0:e0]
        r = inv_rms(ssq(n) + ssq(a) + ssq(b))
        ra, rb = rope(a, b, qc, qsa, qsb)
        qt_ref[h * LANES:h * LANES + MLA_QK, :] = jnp.concatenate(
            [n * qc[n0:a0] * r, ra * r, rb * r], axis=0).astype(BF16)
        qt_ref[h * LANES + MLA_QK:(h + 1) * LANES, :] = pad_rows.astype(BF16)

    krt = dkv[:, MLA_KV_RANK:].T
    kt = _dot_nt(wukt_ref[...], ckv)
    kc, ksa, ksb = kc_ref[...], ksa_ref[...], ksb_ref[...]
    ka, kb = krt[a0:b0], krt[b0:e0]
    ssq_rope = ssq(ka) + ssq(kb)
    ra, rb = rope(ka, kb, kc, ksa, ksb)
    for h in range(MLA_HEADS):
        n = kt[h * MLA_NOPE:(h + 1) * MLA_NOPE]
        r = inv_rms(ssq(n) + ssq_rope)
        kh = jnp.concatenate([n * kc[n0:a0] * r, ra * r, rb * r, pad_rows], axis=0)
        k_ref[:, h * LANES:(h + 1) * LANES] = kh.T.astype(BF16)


def _attn_kernel(qt_ref, k_ref, vt_ref, o_ref, s_ref, acc_ref, m_ref, mb_ref):
    i = pl.program_id(2)
    tq, tk = ATT_Q_TILE, ATT_TILE
    nh = ATT_HEADS
    kpq = tq // tk
    assert kpq == 2
    vrows = vt_ref.shape[1] // nh
    krow = lax.broadcasted_iota(jnp.int32, (tk, tq), 0)
    qcol = lax.broadcasted_iota(jnp.int32, (tk, tq), 1)
    diag_mask = (krow // CHUNK) <= (qcol // CHUNK)
    late = slice(tk, tq)

    def produce(h, kb, qcols=slice(None), mask=None):
        start = pl.multiple_of(kb * tk, tk)
        st = _dot(k_ref[pl.ds(start, tk), h * LANES:(h + 1) * LANES], qt_ref[h * LANES:(h + 1) * LANES, qcols])
        if mask is not None:
            st = jnp.where(mask, st, NEG_BIG)
        s_ref[h, :, qcols] = st
        mb_ref[h, :, qcols] = jnp.max(st, axis=0, keepdims=True)

    def consume(h, kb, mask=None, qcols=slice(None)):
        st = s_ref[h, :, qcols]
        if mask is None:
            mblk = mb_ref[h, :, qcols]
        else:
            st = jnp.where(mask, st, NEG_BIG)
            mblk = jnp.max(st, axis=0, keepdims=True)
        m = m_ref[h, :, qcols]
        m_new = jnp.maximum(m, mblk)
        m_ref[h, :, qcols] = m_new
        pt = jnp.exp2(st - m_new).astype(BF16)
        acc_ref[h, :, qcols] = (jnp.exp2(m - m_new) * acc_ref[h, :, qcols]
                                + _dot(vt_ref[kb, h * vrows:(h + 1) * vrows, :], pt))

    def body(j, carry):
        for h in range(nh):
            produce((h + 1) % nh, j + (h + 1) // nh)
            consume(h, j)
        return carry

    m_ref[...] = jnp.full(m_ref.shape, NEG_BIG, F32)
    acc_ref[...] = jnp.zeros(acc_ref.shape, F32)
    first = kpq * i
    produce(0, 0)
    lax.fori_loop(0, i, lambda t, c: body(kpq * t + 1, body(kpq * t, c)), 0)
    late_mask = diag_mask[:, :tk]
    for h in range(nh):
        if h + 1 < nh:
            produce(h + 1, first, mask=diag_mask)
        else:
            produce(0, first + 1, late, late_mask)
        consume(h, first, diag_mask if h == 0 else None)
    for h in range(nh):
        if h + 1 < nh:
            produce(h + 1, first + 1, late, late_mask)
        consume(h, first + 1, None, late)
        acc = acc_ref[h]
        o_ref[h * MLA_V:(h + 1) * MLA_V, :] = (acc[:MLA_V] / acc[MLA_V:MLA_V + 1]).astype(BF16)


def _mla_layer(x, bsz, seq, g_mix, p, g_ffn, w1, w2):
    t, d = x.shape
    assert TOK_TILE == ATT_TILE
    hp = MLA_HEADS * LANES
    hv = MLA_HEADS * MLA_VROWS
    nt = seq // TOK_TILE
    row = lambda i: (i, 0)
    ttab = lambda i: (i % nt, 0)
    ftab = lambda i: (0, i % nt)
    qt, k, vt = pl.pallas_call(
        _mla_proj_kernel,
        grid=(t // TOK_TILE,),
        in_specs=[pl.BlockSpec((TOK_TILE, d), row), _const_spec((1, d)),
                  _const_spec((d, MLA_Q_RANK)), _const_spec((1, MLA_Q_RANK)),
                  _const_spec((MLA_HEADS * MLA_QK, MLA_Q_RANK)),
                  _const_spec((d, MLA_KV_RANK + LANES)), _const_spec((1, MLA_KV_RANK)),
                  _const_spec((MLA_HEADS * MLA_NOPE, MLA_KV_RANK)), _const_spec((hv, MLA_KV_RANK)),
                  _const_spec((hv, TOK_TILE))] + [pl.BlockSpec((LANES, TOK_TILE), ftab)] * 6,
        out_specs=[pl.BlockSpec((hp, TOK_TILE), lambda i: (0, i)), pl.BlockSpec((TOK_TILE, hp), row),
                   pl.BlockSpec((1, hv, TOK_TILE), lambda i: (i, 0, 0))],
        out_shape=[jax.ShapeDtypeStruct((hp, t), BF16), jax.ShapeDtypeStruct((t, hp), BF16),
                   jax.ShapeDtypeStruct((t // TOK_TILE, hv, TOK_TILE), BF16)],
        compiler_params=_cparams("parallel"),
        name="mla_proj",
    )(x, g_mix, p["w_dq"], p["q_norm"], p["w_uq_t"], p["w_dkv"], p["kv_norm"], p["w_uk_t"], p["w_uv_t"],
      p["vone"], p["qc"], p["qsa"], p["qsb"], p["kc"], p["ksa"], p["ksb"])

    nq = seq // ATT_Q_TILE
    ot = pl.pallas_call(
        _attn_kernel,
        grid=(bsz, MLA_HEADS // ATT_HEADS, nq),
        in_specs=[pl.BlockSpec((ATT_HEADS * LANES, ATT_Q_TILE), lambda b, h, i: (h, b * nq + i)),
                  pl.BlockSpec((seq, ATT_HEADS * LANES), lambda b, h, i: (b, h)),
                  pl.BlockSpec((seq // ATT_TILE, ATT_HEADS * MLA_VROWS, ATT_TILE), lambda b, h, i: (b, h, 0))],
        out_specs=pl.BlockSpec((ATT_HEADS * MLA_V, ATT_Q_TILE), lambda b, h, i: (h, b * nq + i)),
        out_shape=jax.ShapeDtypeStruct((MLA_HEADS * MLA_V, t), BF16),
        scratch_shapes=[pltpu.VMEM((ATT_HEADS, ATT_TILE, ATT_Q_TILE), F32),
                        pltpu.VMEM((ATT_HEADS, MLA_VROWS, ATT_Q_TILE), F32),
                        pltpu.VMEM((ATT_HEADS, 1, ATT_Q_TILE), F32), pltpu.VMEM((ATT_HEADS, 1, ATT_Q_TILE), F32)],
        compiler_params=_cparams("parallel", "parallel", "arbitrary"),
        name="mla_attn",
    )(qt, k, vt)
    return _mlp_call(x, ot, p["w_o"], None, g_ffn, w1, w2, a_transposed=True)


def _mla_params(seq, w_dq, q_norm, w_uq, w_dkv, kv_norm, w_ukv, q_gain, k_gain, w_o):
    pad_head = LANES - MLA_QK
    half = MLA_ROPE // 2
    w_ukv_r = w_ukv.reshape(MLA_KV_RANK, MLA_HEADS, MLA_NOPE + MLA_V)
    w_uv_p = jnp.pad(w_ukv_r[:, :, MLA_NOPE:], ((0, 0), (0, 0), (0, MLA_VROWS - MLA_V)))
    w_rope = jnp.pad(w_dkv[:, MLA_KV_RANK:], ((0, 0), (MLA_NOPE, pad_head)))
    w_dkv_p = jnp.concatenate([w_dkv[:, :MLA_KV_RANK], w_rope], axis=1)
    vone = np.zeros((MLA_HEADS * MLA_VROWS, TOK_TILE), np.float32)
    vone[np.arange(MLA_HEADS) * MLA_VROWS + MLA_V] = 1.0
    gq = q_gain * ((MLA_QK ** -0.5) * math.log2(math.e))
    inv_freq = ROPE_BASE ** (-jnp.arange(half, dtype=F32) / half)
    ang = jnp.arange(seq).astype(F32)[:, None] * inv_freq[None, :]
    cos, sin = jnp.cos(ang), jnp.sin(ang)
    g1 = lambda g: g[MLA_NOPE:MLA_NOPE + half][None, :]
    g2 = lambda g: g[MLA_NOPE + half:MLA_QK][None, :]
    z = lambda n: jnp.zeros((seq, n), F32)

    def tables(g):
        nope = jnp.broadcast_to(g[None, :MLA_NOPE], (seq, MLA_NOPE))
        c = jnp.concatenate([nope, g1(g) * cos, g2(g) * cos, z(pad_head)], axis=1)
        sa = jnp.concatenate([z(MLA_NOPE), -g2(g) * sin, z(half + pad_head)], axis=1)
        sb = jnp.concatenate([z(MLA_NOPE + half), g1(g) * sin, z(pad_head)], axis=1)
        return c.T, sa.T, sb.T

    qc, qsa, qsb = tables(gq)
    kc, ksa, ksb = tables(k_gain)
    return dict(w_dq=w_dq.astype(BF16), q_norm=q_norm.reshape(1, -1),
                w_uq_t=w_uq.T.astype(BF16), w_dkv=w_dkv_p.astype(BF16), kv_norm=kv_norm.reshape(1, -1),
                w_uk_t=w_ukv_r[:, :, :MLA_NOPE].reshape(MLA_KV_RANK, -1).T.astype(BF16),
                w_uv_t=w_uv_p.reshape(MLA_KV_RANK, -1).T.astype(BF16), vone=jnp.asarray(vone, BF16),
                qc=qc, qsa=qsa, qsb=qsb, kc=kc, ksa=ksa, ksb=ksb, w_o=w_o.astype(BF16))


def _mlstm_proj_kernel(x_ref, g_ref, wt_ref, wk_ref, wr_hi_ref, br_ref,
                       qt_ref, k_ref, vt_ref, ot_ref, gr_ref):
    hn = _rms_rows(x_ref[...], g_ref[...])
    hi, lo = _split2(hn)
    hq, hv = ML_HEADS * ML_QK, ML_HEADS * ML_V
    yt = _dot_nt(wt_ref[...], hi)
    qt_ref[...] = yt[:hq].astype(BF16)
    vt_ref[...] = yt[hq:hq + hv].astype(BF16)
    ot_ref[...] = yt[hq + hv:hq + 2 * hv].astype(BF16)
    k_ref[...] = (_dot(hi, wk_ref[...]) * (ML_QK ** -0.5)).astype(BF16)
    g0 = hq + 2 * hv
    gr_ref[...] = (yt[g0:g0 + SUBLANES] + yt[g0 + SUBLANES:] + _dot_nt(wr_hi_ref[...], lo)) + br_ref[...]


def _cap(g):
    return GATE_CAP * jnp.tanh(g * (1.0 / GATE_CAP))


def _mlstm_rec_kernel(qt_ref, k_ref, vt_ref, opt_ref, gr_ref, hnorm_ref, triu_ref, sel_ref,
                      o_ref, c_ref, m_ref):
    L = REC_TILE

    @pl.when(pl.program_id(1) == 0)
    def _():
        c_ref[...] = jnp.zeros_like(c_ref)
        m_ref[...] = jnp.zeros_like(m_ref)

    gr = _cap(gr_ref[...])
    r1, r2, r3 = _split3(_log_sigmoid(gr))
    triu = triu_ref[...]
    bcum_r = _dot(r1, triu) + _dot(r2, triu) + _dot(r3, triu)
    grow = lax.broadcasted_iota(jnp.int32, (SUBLANES, L), 0)
    xr = jnp.concatenate([jnp.where(grow < ML_HEADS, gr, bcum_r), jnp.zeros((LANES - SUBLANES, L), F32)], axis=0)
    x1, x2, x3 = _split3(xr)
    sel = sel_ref[...]
    cb_all = _dot_tn(x1, sel) + _dot_tn(x2, sel) + _dot_tn(x3, sel)
    src = lax.broadcasted_iota(jnp.int32, (L, L), 0)
    tgt = lax.broadcasted_iota(jnp.int32, (L, L), 1)
    causal = src <= tgt
    ones = jnp.ones((LANES, L), BF16)

    for h in range(ML_HEADS):
        fh = ML_HEADS + h
        cb = cb_all[:, h * L:(h + 1) * L]
        bt = bcum_r[fh:fh + 1, :]
        li_r = gr[h:h + 1, :]
        m_prev = m_ref[h:h + 1, 0:1]
        qt = qt_ref[h * ML_QK:(h + 1) * ML_QK, :]
        k = k_ref[:, h * ML_QK:(h + 1) * ML_QK]
        vaug = jnp.concatenate([vt_ref[h * ML_V:(h + 1) * ML_V, :], ones], axis=0)
        c_st = c_ref[h]

        dmat = jnp.where(causal, cb + bt, NEG_BIG)
        inter = bt + m_prev
        m_t = jnp.maximum(inter, jnp.max(dmat, axis=0, keepdims=True))
        w_intra = jnp.exp(dmat - m_t)
        w_inter = jnp.exp(inter - m_t)
        pt = (_dot(k, qt) * w_intra).astype(BF16)
        nd = _dot(vaug, pt) + w_inter * _dot(c_st.astype(BF16), qt)
        den = jnp.maximum(jnp.abs(nd[ML_V:ML_V + 1]), jnp.exp(-m_t))
        hc = nd[:ML_V] / den

        b_last = bt[:, L - 1:L]
        m_new = jnp.maximum(b_last + m_prev, jnp.max(b_last - bt + li_r, axis=-1, keepdims=True))
        ws = jnp.exp(cb[:, :ML_QK] + (b_last - m_new))
        wc = jnp.exp(b_last + m_prev - m_new)
        kw = (k.astype(F32) * ws).astype(BF16)
        c_ref[h] = wc * c_st + _dot(vaug, kw)
        m_ref[h:h + 1, :] = jnp.broadcast_to(m_new, (1, LANES))

        rows = slice(h * ML_V, (h + 1) * ML_V)
        hs = hc * lax.rsqrt(jnp.mean(hc * hc, axis=0, keepdims=True) + EPS) * hnorm_ref[rows, :]
        o_ref[rows, :] = (_sigmoid(opt_ref[rows, :].astype(F32)) * hs).astype(BF16)


def _mlstm_consts(n):
    r = np.arange(n)
    triu = (r[:, None] <= r[None, :]).astype(np.float32)
    sel = np.zeros((LANES, ML_HEADS * n), np.float32)
    for h in range(ML_HEADS):
        sel[h, h * n:(h + 1) * n] = 1.0
        sel[ML_HEADS + h, h * n:(h + 1) * n] = -1.0
    return jnp.asarray(triu, BF16), jnp.asarray(sel, BF16)


def _mlstm_layer(x, bsz, seq, g_mix, p, g_ffn, w1, w2):
    t, d = x.shape
    hq, hv = ML_HEADS * ML_QK, ML_HEADS * ML_V
    row = lambda i: (i, 0)
    col = lambda i: (0, i)
    qt, k, vt, opt, grow = pl.pallas_call(
        _mlstm_proj_kernel,
        grid=(t // PROJ_TILE,),
        in_specs=[pl.BlockSpec((PROJ_TILE, d), row), _const_spec((1, d)),
                  _const_spec((hq + 2 * hv + 2 * SUBLANES, d)),
                  _const_spec((d, hq)), _const_spec((SUBLANES, d)), _const_spec((SUBLANES, 1))],
        out_specs=[pl.BlockSpec((hq, PROJ_TILE), col), pl.BlockSpec((PROJ_TILE, hq), row),
                   pl.BlockSpec((hv, PROJ_TILE), col), pl.BlockSpec((hv, PROJ_TILE), col),
                   pl.BlockSpec((SUBLANES, PROJ_TILE), col)],
        out_shape=[jax.ShapeDtypeStruct((hq, t), BF16), jax.ShapeDtypeStruct((t, hq), BF16),
                   jax.ShapeDtypeStruct((hv, t), BF16), jax.ShapeDtypeStruct((hv, t), BF16),
                   jax.ShapeDtypeStruct((SUBLANES, t), F32)],
        compiler_params=_cparams("parallel"),
        name="mlstm_proj",
    )(x, g_mix, p["w_t"], p["w_k"], p["wr_hi"], p["b_row"])

    nc = seq // REC_TILE
    triu, sel = _mlstm_consts(REC_TILE)
    hnorm = jnp.broadcast_to(p["head_norm"].reshape(hv, 1), (hv, REC_TILE))
    rblk = lambda b, c: (b * nc + c, 0)
    cblk = lambda b, c: (0, b * nc + c)
    at = pl.pallas_call(
        _mlstm_rec_kernel,
        grid=(bsz, nc),
        in_specs=[pl.BlockSpec((hq, REC_TILE), cblk), pl.BlockSpec((REC_TILE, hq), rblk),
                  pl.BlockSpec((hv, REC_TILE), cblk), pl.BlockSpec((hv, REC_TILE), cblk),
                  pl.BlockSpec((SUBLANES, REC_TILE), cblk),
                  _const_spec((hv, REC_TILE)), _const_spec((REC_TILE, REC_TILE)),
                  _const_spec((LANES, ML_HEADS * REC_TILE))],
        out_specs=pl.BlockSpec((hv, REC_TILE), cblk),
        out_shape=jax.ShapeDtypeStruct((hv, t), BF16),
        scratch_shapes=[pltpu.VMEM((ML_HEADS, ML_V + LANES, ML_QK), F32), pltpu.VMEM((SUBLANES, LANES), F32)],
        compiler_params=_cparams("parallel", "arbitrary"),
        name="mlstm_rec",
    )(qt, k, vt, opt, grow, hnorm, triu, sel)
    return _mlp_call(x, at, p["w_o"], None, g_ffn, w1, w2, a_transposed=True)


def _mlstm_params(w_in, w_if, b_if, head_norm, w_o):
    ng = 2 * ML_HEADS
    hq = ML_HEADS * ML_QK
    w_row = w_if.T
    wr_hi = w_row.astype(BF16)
    wr_lo = (w_row - wr_hi.astype(F32)).astype(BF16)
    w_t = jnp.concatenate([w_in[:, :hq], w_in[:, 2 * hq:]], axis=1).T.astype(BF16)
    return dict(w_t=jnp.concatenate([w_t, wr_hi, wr_lo], axis=0), w_k=w_in[:, hq:2 * hq].astype(BF16),
                wr_hi=wr_hi, b_row=b_if.reshape(ng, 1),
                head_norm=head_norm, w_o=w_o.astype(BF16))


def _gla_proj_kernel(x_ref, g_ref, win_ref, wa1_ref, wa2_ref, ba_ref, q_ref, k_ref, v_ref, r_ref, la_ref):
    hn = _rms_rows(x_ref[...], g_ref[...]).astype(BF16)
    y = _dot(hn, win_ref[...])
    hk, hv = GLA_HEADS * GLA_K, GLA_HEADS * GLA_V
    q_ref[...] = (y[:, :hk] * (GLA_K ** -0.5)).astype(BF16)
    k_ref[...] = y[:, hk:2 * hk].astype(BF16)
    v_ref[...] = y[:, 2 * hk:2 * hk + hv].astype(BF16)
    r_ref[...] = y[:, 2 * hk + hv:].astype(BF16)
    z = _dot(_dot(hn, wa1_ref[...]).astype(BF16), wa2_ref[...]) + ba_ref[...]
    la_ref[...] = _log_sigmoid(z) * (1.0 / GLA_TAU)


def _gla_levels():
    return [2 ** j for j in range(1, int(math.log2(REC_TILE)) + 1)]


def _gla_consts():
    n = REC_TILE
    t = np.arange(n)[:, None]
    u = np.arange(n)[None, :]
    mats = []
    for p in _gla_levels():
        first_upper = (t // p) * p + p // 2
        upper = (t % p) >= p // 2
        m_up = (u > first_upper) & (u <= t)
        m_lo = (u > t) & (u <= first_upper)
        mats.append(np.where(upper, m_up, m_lo))
    mats.append(u <= t)
    mats.append(u > t)
    return jnp.asarray(np.stack(mats).astype(np.float32), BF16)


def _gla_rec_kernel(q_ref, k_ref, v_ref, r_ref, la_ref, hnorm_ref, w_ref, o_ref, c_ref):
    L = REC_TILE
    levels = _gla_levels()
    nl = len(levels)

    @pl.when(pl.program_id(1) == 0)
    def _():
        c_ref[...] = jnp.zeros_like(c_ref)

    la_hi, la_lo = _split2(la_ref[...])

    def decay(j, two_term):
        w = w_ref[j]
        x = _dot(w, la_hi)
        if two_term:
            x = x + _dot(w, la_lo)
        return jnp.exp(x)

    row = lax.broadcasted_iota(jnp.int32, (L, L), 0)
    col = lax.broadcasted_iota(jnp.int32, (L, L), 1)
    row_xor_col = row ^ col
    rowk = lax.broadcasted_iota(jnp.int32, (L, GLA_K), 0)
    e_cum = decay(nl, True)
    e_rev = decay(nl + 1, True)
    e_lvl = [decay(j, False) for j in range(nl)]

    for h in range(GLA_HEADS):
        ks = slice(h * GLA_K, (h + 1) * GLA_K)
        vs = slice(h * GLA_V, (h + 1) * GLA_V)
        q = q_ref[:, ks].astype(F32)
        k = k_ref[:, ks].astype(F32)
        v = v_ref[:, vs]
        att = jnp.where(row == col, _dot_nt(q_ref[:, ks], k_ref[:, ks]), 0.0)
        for j, p in enumerate(levels):
            upper = (rowk & (p - 1)) >= (p // 2)
            e = e_lvl[j][:, ks]
            qf = jnp.where(upper, q * e, 0.0).astype(BF16)
            kf = jnp.where(upper, 0.0, k * e).astype(BF16)
            a = _dot_nt(qf, kf)
            if p < L:
                a = jnp.where(row_xor_col < p, a, 0.0)
            att = att + a
        c_st = c_ref[h]
        o = _dot(att.astype(BF16), v) + _dot_nt((q * e_cum[:, ks]).astype(BF16), c_st.astype(BF16))
        c_ref[h] = e_cum[L - 1:L, ks] * c_st + _dot_tn(v, (k * e_rev[:, ks]).astype(BF16))
        o = _rms_rows(o, hnorm_ref[:, vs])
        rr = r_ref[:, vs].astype(F32)
        o_ref[:, vs] = (o * rr * _sigmoid(rr)).astype(BF16)


def _gla_layer(x, bsz, seq, g_mix, p, g_ffn, w1, w2):
    t, d = x.shape
    hk, hv = GLA_HEADS * GLA_K, GLA_HEADS * GLA_V
    row = lambda i: (i, 0)
    q, k, v, r, la = pl.pallas_call(
        _gla_proj_kernel,
        grid=(t // PROJ_TILE,),
        in_specs=[pl.BlockSpec((PROJ_TILE, d), row), _const_spec((1, d)), _const_spec((d, 2 * hk + 2 * hv)),
                  _const_spec((d, LANES)), _const_spec((LANES, hk)), _const_spec((1, hk))],
        out_specs=[pl.BlockSpec((PROJ_TILE, hk), row), pl.BlockSpec((PROJ_TILE, hk), row),
                   pl.BlockSpec((PROJ_TILE, hv), row), pl.BlockSpec((PROJ_TILE, hv), row),
                   pl.BlockSpec((PROJ_TILE, hk), row)],
        out_shape=[jax.ShapeDtypeStruct((t, hk), BF16), jax.ShapeDtypeStruct((t, hk), BF16),
                   jax.ShapeDtypeStruct((t, hv), BF16), jax.ShapeDtypeStruct((t, hv), BF16),
                   jax.ShapeDtypeStruct((t, hk), F32)],
        compiler_params=_cparams("parallel"),
        name="gla_proj",
    )(x, g_mix, p["w_in"], p["w_a1"], p["w_a2"], p["b_a"])

    nc = seq // REC_TILE
    wmats = _gla_consts()
    blk = lambda b, c: (b * nc + c, 0)
    a = pl.pallas_call(
        _gla_rec_kernel,
        grid=(bsz, nc),
        in_specs=[pl.BlockSpec((REC_TILE, hk), blk), pl.BlockSpec((REC_TILE, hk), blk),
                  pl.BlockSpec((REC_TILE, hv), blk), pl.BlockSpec((REC_TILE, hv), blk),
                  pl.BlockSpec((REC_TILE, hk), blk), _const_spec((1, hv)), _const_spec(wmats.shape)],
        out_specs=pl.BlockSpec((REC_TILE, hv), blk),
        out_shape=jax.ShapeDtypeStruct((t, hv), BF16),
        scratch_shapes=[pltpu.VMEM((GLA_HEADS, GLA_V, GLA_K), F32)],
        compiler_params=_cparams("parallel", "arbitrary"),
        name="gla_rec",
    )(q, k, v, r, la, p["head_norm"], wmats)
    return _mlp_call(x, a, p["w_o"], None, g_ffn, w1, w2)


def _gla_params(w_in, w_a1, w_a2, b_a, head_norm, w_o):
    return dict(w_in=w_in.astype(BF16),
                w_a1=jnp.pad(w_a1, ((0, 0), (0, LANES - GLA_GATE_RANK))).astype(BF16),
                w_a2=jnp.pad(w_a2, ((0, LANES - GLA_GATE_RANK), (0, 0))).astype(BF16),
                b_a=b_a.reshape(1, -1), head_norm=head_norm.reshape(1, -1), w_o=w_o.astype(BF16))


def _conv_kernel(x_ref, g_ref, w1_ref, b1_ref, wdw_ref, lng_ref, lnb_ref, o_ref, u_ref, c_ref):
    tm = TOK_TILE
    d = D_MODEL
    nbuf = CONV_HALO + tm

    @pl.when(pl.program_id(1) == 0)
    def _():
        u_ref[0, :, 0:CONV_HALO, :] = jnp.zeros((d // LANES, CONV_HALO, LANES), F32)

    @pl.when(pl.program_id(1) != 0)
    def _():
        u_ref[0, :, 0:CONV_HALO, :] = u_ref[0, :, tm:tm + CONV_HALO, :]

    hn = _rms_rows(x_ref[...], g_ref[...]).astype(BF16)
    y = _dot(hn, w1_ref[...]) + b1_ref[...]
    u = y[:, :d] * _sigmoid(y[:, d:])
    lead = CONV_HALO - (CONV_WIDTH - 1)
    groups = CONV_ROWS // SUBLANES
    tiles = nbuf // SUBLANES
    sub = lax.broadcasted_iota(jnp.int32, (tiles, SUBLANES, LANES), 1)

    for cb in range(d // LANES):
        cols = slice(cb * LANES, (cb + 1) * LANES)
        u_ref[0, cb, CONV_HALO:, :] = u[:, cols]
        full = u_ref[0, cb].reshape(tiles, SUBLANES, LANES)
        for s in range(1, SUBLANES):
            rot = pltpu.roll(full, SUBLANES - s, 1)
            nxt = jnp.concatenate([rot[1:], rot[:1]], axis=0)
            u_ref[s, cb] = jnp.where(sub < SUBLANES - s, rot, nxt).reshape(nbuf, LANES)

        def taps(c, carry):
            r0 = pl.multiple_of(c * CONV_ROWS, CONV_ROWS)
            acc = jnp.broadcast_to(wdw_ref[CONV_WIDTH, :, cols][None], (groups, SUBLANES, LANES))
            for j in range(CONV_WIDTH):
                off = lead + j
                start = pl.multiple_of(r0 + (off // SUBLANES) * SUBLANES, SUBLANES)
                uj = u_ref[off % SUBLANES, cb, pl.ds(start, CONV_ROWS), :].reshape(groups, SUBLANES, LANES)
                acc = acc + uj * wdw_ref[j, :, cols][None]
            c_ref[pl.ds(r0, CONV_ROWS), cols] = acc.reshape(CONV_ROWS, LANES)
            return carry

        lax.fori_loop(0, tm // CONV_ROWS, taps, 0)

    def norm(c, carry):
        r0 = pl.multiple_of(c * CONV_NORM_ROWS, CONV_NORM_ROWS)
        acc = c_ref[pl.ds(r0, CONV_NORM_ROWS), :]
        mu = jnp.mean(acc, axis=-1, keepdims=True)
        cen = acc - mu
        var = jnp.mean(cen * cen, axis=-1, keepdims=True)
        z = cen * lax.rsqrt(var + EPS) * lng_ref[...] + lnb_ref[...]
        o_ref[pl.ds(r0, CONV_NORM_ROWS), :] = (z * _sigmoid(z)).astype(BF16)
        return carry

    lax.fori_loop(0, tm // CONV_NORM_ROWS, norm, 0, unroll=8)


def _conv_layer(x, bsz, seq, g_mix, p, g_ffn, w1, w2):
    t, d = x.shape
    nt = seq // TOK_TILE
    blk = lambda b, i: (b * nt + i, 0)
    c = pl.pallas_call(
        _conv_kernel,
        grid=(bsz, nt),
        in_specs=[pl.BlockSpec((TOK_TILE, d), blk), _const_spec((1, d)), _const_spec((d, 2 * d)),
                  _const_spec((1, 2 * d)), _const_spec((CONV_WIDTH + 1, SUBLANES, d)),
                  _const_spec((1, d)), _const_spec((1, d))],
        out_specs=pl.BlockSpec((TOK_TILE, d), blk),
        out_shape=jax.ShapeDtypeStruct((t, d), BF16),
        scratch_shapes=[pltpu.VMEM((SUBLANES, d // LANES, CONV_HALO + TOK_TILE, LANES), F32),
                        pltpu.VMEM((TOK_TILE, d), F32)],
        compiler_params=_cparams("parallel", "arbitrary"),
        name="conv",
    )(x, g_mix, p["w_pw1"], p["b_pw1"], p["w_dw"], p["ln_g"], p["ln_b"])
    return _mlp_call(x, c, p["w_pw2"], p["b_pw2"], g_ffn, w1, w2)


def _conv_params(w_pw1, b_pw1, w_dw, b_dw, ln_g, ln_b, w_pw2, b_pw2):
    return dict(w_pw1=w_pw1.astype(BF16), b_pw1=b_pw1.reshape(1, -1),
                w_dw=jnp.broadcast_to(jnp.concatenate([w_dw, b_dw[None]], axis=0)[:, None, :],
                                      (CONV_WIDTH + 1, SUBLANES, w_dw.shape[1])),
                ln_g=ln_g.reshape(1, -1), ln_b=ln_b.reshape(1, -1),
                w_pw2=w_pw2.astype(BF16), b_pw2=b_pw2.reshape(1, -1))


def kernel(x, norm_mix, norm_ffn, mla_w_dq, mla_q_norm, mla_w_uq, mla_w_dkv, mla_kv_norm, mla_w_ukv, mla_q_gain, mla_k_gain, mla_w_o, mlstm_w_in, mlstm_w_if, mlstm_b_if, mlstm_head_norm, mlstm_w_o, gla_w_in, gla_w_a1, gla_w_a2, gla_b_a, gla_head_norm, gla_w_o, conv_w_pw1, conv_b_pw1, conv_w_dw, conv_b_dw, conv_ln_g, conv_ln_b, conv_w_pw2, conv_b_pw2, ffn_w1, ffn_w2):
    bsz, seq, d = x.shape
    depth = norm_mix.shape[0]
    assert d == D_MODEL and all(seq % tile == 0 for tile in (ATT_Q_TILE, TOK_TILE, MLP_TILE, PROJ_TILE, REC_TILE))
    h = x.reshape(bsz * seq, d)
    ffn_w1, ffn_w2 = ffn_w1.astype(BF16), ffn_w2.astype(BF16)
    for i in range(depth):
        kind, j = i % 4, i // 4
        g_mix = norm_mix[i].reshape(1, d)
        g_ffn = norm_ffn[i].reshape(1, d)
        w1, w2 = (ffn_w1, i), (ffn_w2, i)
        if kind == 0:
            p = _mla_params(seq, mla_w_dq[j], mla_q_norm[j], mla_w_uq[j], mla_w_dkv[j], mla_kv_norm[j],
                            mla_w_ukv[j], mla_q_gain[j], mla_k_gain[j], mla_w_o[j])
            h = _mla_layer(h, bsz, seq, g_mix, p, g_ffn, w1, w2)
        elif kind == 1:
            p = _mlstm_params(mlstm_w_in[j], mlstm_w_if[j], mlstm_b_if[j], mlstm_head_norm[j], mlstm_w_o[j])
            h = _mlstm_layer(h, bsz, seq, g_mix, p, g_ffn, w1, w2)
        elif kind == 2:
            p = _gla_params(gla_w_in[j], gla_w_a1[j], gla_w_a2[j], gla_b_a[j], gla_head_norm[j], gla_w_o[j])
            h = _gla_layer(h, bsz, seq, g_mix, p, g_ffn, w1, w2)
        else:
            p = _conv_params(conv_w_pw1[j], conv_b_pw1[j], conv_w_dw[j], conv_b_dw[j], conv_ln_g[j],
                             conv_ln_b[j], conv_w_pw2[j], conv_b_pw2[j])
            h = _conv_layer(h, bsz, seq, g_mix, p, g_ffn, w1, w2)
    return h.reshape(bsz, seq, d)
```

```python
import functools
import math

import numpy as np
import jax
import jax.numpy as jnp
from jax import lax
from jax.experimental import pallas as pl
from jax.experimental.pallas import tpu as pltpu

F32 = jnp.float32
BF16 = jnp.bfloat16

D_MODEL = 1024
D_FF = 4 * D_MODEL
EPS = 1e-6
CHUNK = 64

MLA_HEADS = 16
MLA_NOPE = 64
MLA_ROPE = 32
MLA_QK = MLA_NOPE + MLA_ROPE
MLA_V = 64
MLA_Q_RANK = 384
MLA_KV_RANK = 256
ROPE_BASE = 10000.0
MLA_VROWS = MLA_V + 16

ML_HEADS = 4
ML_QK = D_MODEL // 8
ML_V = D_MODEL // 4
GATE_CAP = 15.0

GLA_HEADS = 4
GLA_K = D_MODEL // 8
GLA_V = D_MODEL // 4
GLA_GATE_RANK = 16
GLA_TAU = 16.0

CONV_WIDTH = 31

LANES = 128
SUBLANES = 8
VMEM_LIMIT = 56 * 1024 * 1024

TOK_TILE = 512
MLP_TILE = 1024
PROJ_TILE = 1024
FF_TILE = 1024
ATT_TILE = 512
ATT_Q_TILE = 1024
ATT_HEADS = 4
REC_TILE = 256
CONV_HALO = 32
CONV_ROWS = 128
CONV_NORM_ROWS = 32
NEG_BIG = -1e30


def _cparams(*sem):
    return pltpu.CompilerParams(dimension_semantics=sem, vmem_limit_bytes=VMEM_LIMIT)


def _const_spec(shape):
    nd = len(shape)
    return pl.BlockSpec(shape, lambda *_: (0,) * nd, pipeline_mode=pl.Buffered(1))


def _dot(a, b):
    return jnp.dot(a, b, preferred_element_type=F32)


def _dot_nt(a, b):
    return lax.dot_general(a, b, (((1,), (1,)), ((), ())), preferred_element_type=F32)


def _dot_tn(a, b):
    return lax.dot_general(a, b, (((0,), (0,)), ((), ())), preferred_element_type=F32)


def _split2(a):
    hi = a.astype(BF16)
    lo = (a - hi.astype(F32)).astype(BF16)
    return hi, lo


def _split3(a):
    hi = a.astype(BF16)
    r = a - hi.astype(F32)
    mid = r.astype(BF16)
    lo = (r - mid.astype(F32)).astype(BF16)
    return hi, mid, lo


def _rms_rows(x, g):
    return x * lax.rsqrt(jnp.mean(x * x, axis=-1, keepdims=True) + EPS) * g


def _log_sigmoid(z):
    return jnp.minimum(z, 0.0) - jnp.log1p(jnp.exp(-jnp.abs(z)))


def _sigmoid(z):
    return 1.0 / (1.0 + jnp.exp(-z))


def _mlp_kernel(*refs, has_bias, a_transposed):
    if has_bias:
        x_ref, a_ref, wo_ref, bo_ref, g_ref, w1_ref, w2_ref, o_ref = refs
    else:
        x_ref, a_ref, wo_ref, g_ref, w1_ref, w2_ref, o_ref = refs
    if a_transposed:
        x1 = x_ref[...] + _dot_tn(a_ref[...], wo_ref[...])
    else:
        x1 = x_ref[...] + _dot(a_ref[...], wo_ref[...])
    if has_bias:
        x1 = x1 + bo_ref[...]
    hn = _rms_rows(x1, g_ref[...]).astype(BF16)
    acc = x1
    for c in range(D_FF // FF_TILE):
        h = _dot(hn, w1_ref[:, c * FF_TILE:(c + 1) * FF_TILE])
        h = jnp.maximum(h, 0.0)
        acc = acc + _dot((h * h).astype(BF16), w2_ref[c * FF_TILE:(c + 1) * FF_TILE, :])
    o_ref[...] = acc


def _mlp_call(x, a, w_o, b_o, g, w1, w2, a_transposed=False):
    t, d = x.shape
    din = w_o.shape[0]
    has_bias = b_o is not None
    (w1, layer), (w2, _) = w1, w2
    layer_spec = lambda shape: pl.BlockSpec((None,) + shape, lambda *_: (layer, 0, 0), pipeline_mode=pl.Buffered(1))
    row = lambda i: (i, 0)
    a_spec = pl.BlockSpec((din, MLP_TILE), lambda i: (0, i)) if a_transposed else pl.BlockSpec((MLP_TILE, din), row)
    in_specs = [pl.BlockSpec((MLP_TILE, d), row), a_spec, _const_spec((din, d))]
    args = [x, a, w_o]
    if has_bias:
        in_specs.append(_const_spec((1, d)))
        args.append(b_o)
    in_specs += [_const_spec((1, d)), layer_spec((d, D_FF)), layer_spec((D_FF, d))]
    args += [g, w1, w2]
    return pl.pallas_call(
        functools.partial(_mlp_kernel, has_bias=has_bias, a_transposed=a_transposed),
        grid=(t // MLP_TILE,),
        in_specs=in_specs,
        out_specs=pl.BlockSpec((MLP_TILE, d), row),
        out_shape=jax.ShapeDtypeStruct((t, d), F32),
        compiler_params=_cparams("parallel"),
        name="mlp",
    )(*args)


def _mla_proj_kernel(x_ref, g_ref, wdq_ref, qn_ref, wuqt_ref, wdkv_ref, kvn_ref, wukt_ref, wuvt_ref,
                     vone_ref, qc_ref, qsa_ref, qsb_ref, kc_ref, ksa_ref, ksb_ref,
                     qt_ref, k_ref, vt_ref):
    half = MLA_ROPE // 2
    hn = _rms_rows(x_ref[...], g_ref[...]).astype(BF16)
    cq = _rms_rows(_dot(hn, wdq_ref[...]), qn_ref[...]).astype(BF16)
    dkv = _dot(hn, wdkv_ref[...])
    ckv = _rms_rows(dkv[:, :MLA_KV_RANK], kvn_ref[...]).astype(BF16)

    vt_ref[0] = (_dot_nt(wuvt_ref[...], ckv) + vone_ref[...]).astype(BF16)

    n0, a0, b0, e0 = 0, MLA_NOPE, MLA_NOPE + half, MLA_QK

    def ssq(t):
        return jnp.sum(t * t, axis=0, keepdims=True)

    def rope(a, b, c, sa, sb):
        return a * c[a0:b0] + b * sa[a0:b0], b * c[b0:e0] + a * sb[b0:e0]

    def inv_rms(s):
        return lax.rsqrt(s * (1.0 / MLA_QK) + EPS)

    tm = x_ref.shape[0]
    pad_rows = jnp.zeros((LANES - MLA_QK, tm), F32)

    qt = _dot_nt(wuqt_ref[...], cq)
    qc, qsa, qsb = qc_ref[...], qsa_ref[...], qsb_ref[...]
    for h in range(MLA_HEADS):
        t = qt[h * MLA_QK:(h + 1) * MLA_QK]
        n, a, b = t[n0:a0], t[a0:b0], t[b0:e0]
        r = inv_rms(ssq(n) + ssq(a) + ssq(b))
        ra, rb = rope(a, b, qc, qsa, qsb)
        qt_ref[h * LANES:h * LANES + MLA_QK, :] = jnp.concatenate(
            [n * qc[n0:a0] * r, ra * r, rb * r], axis=0).astype(BF16)
        qt_ref[h * LANES + MLA_QK:(h + 1) * LANES, :] = pad_rows.astype(BF16)

    krt = dkv[:, MLA_KV_RANK:].T
    kt = _dot_nt(wukt_ref[...], ckv)
    kc, ksa, ksb = kc_ref[...], ksa_ref[...], ksb_ref[...]
    ka, kb = krt[a0:b0], krt[b0:e0]
    ssq_rope = ssq(ka) + ssq(kb)
    ra, rb = rope(ka, kb, kc, ksa, ksb)
    for h in range(MLA_HEADS):
        n = kt[h * MLA_NOPE:(h + 1) * MLA_NOPE]
        r = inv_rms(ssq(n) + ssq_rope)
        kh = jnp.concatenate([n * kc[n0:a0] * r, ra * r, rb * r, pad_rows], axis=0)
        k_ref[:, h * LANES:(h + 1) * LANES] = kh.T.astype(BF16)


def _attn_kernel(qt_ref, k_ref, vt_ref, o_ref, s_ref, acc_ref, m_ref, mb_ref):
    i = pl.program_id(2)
    tq, tk = ATT_Q_TILE, ATT_TILE
    nh = ATT_HEADS
    kpq = tq // tk
    assert kpq == 2
    vrows = vt_ref.shape[1] // nh
    krow = lax.broadcasted_iota(jnp.int32, (tk, tq), 0)
    qcol = lax.broadcasted_iota(jnp.int32, (tk, tq), 1)
    diag_mask = (krow // CHUNK) <= (qcol // CHUNK)
    late = slice(tk, tq)

    def produce(h, kb, qcols=slice(None), mask=None):
        start = pl.multiple_of(kb * tk, tk)
        st = _dot(k_ref[pl.ds(start, tk), h * LANES:(h + 1) * LANES], qt_ref[h * LANES:(h + 1) * LANES, qcols])
        if mask is not None:
            st = jnp.where(mask, st, NEG_BIG)
        s_ref[h, :, qcols] = st
        mb_ref[h, :, qcols] = jnp.max(st, axis=0, keepdims=True)

    def consume(h, kb, mask=None, qcols=slice(None)):
        st = s_ref[h, :, qcols]
        if mask is None:
            mblk = mb_ref[h, :, qcols]
        else:
            st = jnp.where(mask, st, NEG_BIG)
            mblk = jnp.max(st, axis=0, keepdims=True)
        m = m_ref[h, :, qcols]
        m_new = jnp.maximum(m, mblk)
        m_ref[h, :, qcols] = m_new
        pt = jnp.exp2(st - m_new).astype(BF16)
        acc_ref[h, :, qcols] = (jnp.exp2(m - m_new) * acc_ref[h, :, qcols]
                                + _dot(vt_ref[kb, h * vrows:(h + 1) * vrows, :], pt))

    def body(j, carry):
        for h in range(nh):
            produce((h + 1) % nh, j + (h + 1) // nh)
            consume(h, j)
        return carry

    m_ref[...] = jnp.full(m_ref.shape, NEG_BIG, F32)
    acc_ref[...] = jnp.zeros(acc_ref.shape, F32)
    first = kpq * i
    produce(0, 0)
    lax.fori_loop(0, i, lambda t, c: body(kpq * t + 1, body(kpq * t, c)), 0)
    late_mask = diag_mask[:, :tk]
    for h in range(nh):
        if h + 1 < nh:
            produce(h + 1, first, mask=diag_mask)
        else:
            produce(0, first + 1, late, late_mask)
        consume(h, first, diag_mask if h == 0 else None)
    for h in range(nh):
        if h + 1 < nh:
            produce(h + 1, first + 1, late, late_mask)
        consume(h, first + 1, None, late)
        acc = acc_ref[h]
        o_ref[h * MLA_V:(h + 1) * MLA_V, :] = (acc[:MLA_V] / acc[MLA_V:MLA_V + 1]).astype(BF16)


def _mla_layer(x, bsz, seq, g_mix, p, g_ffn, w1, w2):
    t, d = x.shape
    assert TOK_TILE == ATT_TILE
    hp = MLA_HEADS * LANES
    hv = MLA_HEADS * MLA_VROWS
    nt = seq // TOK_TILE
    row = lambda i: (i, 0)
    ttab = lambda i: (i % nt, 0)
    ftab = lambda i: (0, i % nt)
    qt, k, vt = pl.pallas_call(
        _mla_proj_kernel,
        grid=(t // TOK_TILE,),
        in_specs=[pl.BlockSpec((TOK_TILE, d), row), _const_spec((1, d)),
                  _const_spec((d, MLA_Q_RANK)), _const_spec((1, MLA_Q_RANK)),
                  _const_spec((MLA_HEADS * MLA_QK, MLA_Q_RANK)),
                  _const_spec((d, MLA_KV_RANK + LANES)), _const_spec((1, MLA_KV_RANK)),
                  _const_spec((MLA_HEADS * MLA_NOPE, MLA_KV_RANK)), _const_spec((hv, MLA_KV_RANK)),
                  _const_spec((hv, TOK_TILE))] + [pl.BlockSpec((LANES, TOK_TILE), ftab)] * 6,
        out_specs=[pl.BlockSpec((hp, TOK_TILE), lambda i: (0, i)), pl.BlockSpec((TOK_TILE, hp), row),
                   pl.BlockSpec((1, hv, TOK_TILE), lambda i: (i, 0, 0))],
        out_shape=[jax.ShapeDtypeStruct((hp, t), BF16), jax.ShapeDtypeStruct((t, hp), BF16),
                   jax.ShapeDtypeStruct((t // TOK_TILE, hv, TOK_TILE), BF16)],
        compiler_params=_cparams("parallel"),
        name="mla_proj",
    )(x, g_mix, p["w_dq"], p["q_norm"], p["w_uq_t"], p["w_dkv"], p["kv_norm"], p["w_uk_t"], p["w_uv_t"],
      p["vone"], p["qc"], p["qsa"], p["qsb"], p["kc"], p["ksa"], p["ksb"])

    nq = seq // ATT_Q_TILE
    ot = pl.pallas_call(
        _attn_kernel,
        grid=(bsz, MLA_HEADS // ATT_HEADS, nq),
        in_specs=[pl.BlockSpec((ATT_HEADS * LANES, ATT_Q_TILE), lambda b, h, i: (h, b * nq + i)),
                  pl.BlockSpec((seq, ATT_HEADS * LANES), lambda b, h, i: (b, h)),
                  pl.BlockSpec((seq // ATT_TILE, ATT_HEADS * MLA_VROWS, ATT_TILE), lambda b, h, i: (b, h, 0))],
        out_specs=pl.BlockSpec((ATT_HEADS * MLA_V, ATT_Q_TILE), lambda b, h, i: (h, b * nq + i)),
        out_shape=jax.ShapeDtypeStruct((MLA_HEADS * MLA_V, t), BF16),
        scratch_shapes=[pltpu.VMEM((ATT_HEADS, ATT_TILE, ATT_Q_TILE), F32),
                        pltpu.VMEM((ATT_HEADS, MLA_VROWS, ATT_Q_TILE), F32),
                        pltpu.VMEM((ATT_HEADS, 1, ATT_Q_TILE), F32), pltpu.VMEM((ATT_HEADS, 1, ATT_Q_TILE), F32)],
        compiler_params=_cparams("parallel", "parallel", "arbitrary"),
        name="mla_attn",
    )(qt, k, vt)
    return _mlp_call(x, ot, p["w_o"], None, g_ffn, w1, w2, a_transposed=True)


def _mla_params(seq, w_dq, q_norm, w_uq, w_dkv, kv_norm, w_ukv, q_gain, k_gain, w_o):
    pad_head = LANES - MLA_QK
    half = MLA_ROPE // 2
    w_ukv_r = w_ukv.reshape(MLA_KV_RANK, MLA_HEADS, MLA_NOPE + MLA_V)
    w_uv_p = jnp.pad(w_ukv_r[:, :, MLA_NOPE:], ((0, 0), (0, 0), (0, MLA_VROWS - MLA_V)))
    w_rope = jnp.pad(w_dkv[:, MLA_KV_RANK:], ((0, 0), (MLA_NOPE, pad_head)))
    w_dkv_p = jnp.concatenate([w_dkv[:, :MLA_KV_RANK], w_rope], axis=1)
    vone = np.zeros((MLA_HEADS * MLA_VROWS, TOK_TILE), np.float32)
    vone[np.arange(MLA_HEADS) * MLA_VROWS + MLA_V] = 1.0
    gq = q_gain * ((MLA_QK ** -0.5) * math.log2(math.e))
    inv_freq = ROPE_BASE ** (-jnp.arange(half, dtype=F32) / half)
    ang = jnp.arange(seq).astype(F32)[:, None] * inv_freq[None, :]
    cos, sin = jnp.cos(ang), jnp.sin(ang)
    g1 = lambda g: g[MLA_NOPE:MLA_NOPE + half][None, :]
    g2 = lambda g: g[MLA_NOPE + half:MLA_QK][None, :]
    z = lambda n: jnp.zeros((seq, n), F32)

    def tables(g):
        nope = jnp.broadcast_to(g[None, :MLA_NOPE], (seq, MLA_NOPE))
        c = jnp.concatenate([nope, g1(g) * cos, g2(g) * cos, z(pad_head)], axis=1)
        sa = jnp.concatenate([z(MLA_NOPE), -g2(g) * sin, z(half + pad_head)], axis=1)
        sb = jnp.concatenate([z(MLA_NOPE + half), g1(g) * sin, z(pad_head)], axis=1)
        return c.T, sa.T, sb.T

    qc, qsa, qsb = tables(gq)
    kc, ksa, ksb = tables(k_gain)
    return dict(w_dq=w_dq.astype(BF16), q_norm=q_norm.reshape(1, -1),
                w_uq_t=w_uq.T.astype(BF16), w_dkv=w_dkv_p.astype(BF16), kv_norm=kv_norm.reshape(1, -1),
                w_uk_t=w_ukv_r[:, :, :MLA_NOPE].reshape(MLA_KV_RANK, -1).T.astype(BF16),
                w_uv_t=w_uv_p.reshape(MLA_KV_RANK, -1).T.astype(BF16), vone=jnp.asarray(vone, BF16),
                qc=qc, qsa=qsa, qsb=qsb, kc=kc, ksa=ksa, ksb=ksb, w_o=w_o.astype(BF16))


def _mlstm_proj_kernel(x_ref, g_ref, wt_ref, wk_ref, wr_hi_ref, br_ref,
                       qt_ref, k_ref, vt_ref, ot_ref, gr_ref):
    hn = _rms_rows(x_ref[...], g_ref[...])
    hi, lo = _split2(hn)
    hq, hv = ML_HEADS * ML_QK, ML_HEADS * ML_V
    yt = _dot_nt(wt_ref[...], hi)
    qt_ref[...] = yt[:hq].astype(BF16)
    vt_ref[...] = yt[hq:hq + hv].astype(BF16)
    ot_ref[...] = yt[hq + hv:hq + 2 * hv].astype(BF16)
    k_ref[...] = (_dot(hi, wk_ref[...]) * (ML_QK ** -0.5)).astype(BF16)
    g0 = hq + 2 * hv
    gr_ref[...] = (yt[g0:g0 + SUBLANES] + yt[g0 + SUBLANES:] + _dot_nt(wr_hi_ref[...], lo)) + br_ref[...]


def _cap(g):
    return GATE_CAP * jnp.tanh(g * (1.0 / GATE_CAP))


def _mlstm_rec_kernel(qt_ref, k_ref, vt_ref, opt_ref, gr_ref, hnorm_ref, triu_ref, sel_ref,
                      o_ref, c_ref, m_ref):
    L = REC_TILE

    @pl.when(pl.program_id(1) == 0)
    def _():
        c_ref[...] = jnp.zeros_like(c_ref)
        m_ref[...] = jnp.zeros_like(m_ref)

    gr = _cap(gr_ref[...])
    r1, r2, r3 = _split3(_log_sigmoid(gr))
    triu = triu_ref[...]
    bcum_r = _dot(r1, triu) + _dot(r2, triu) + _dot(r3, triu)
    grow = lax.broadcasted_iota(jnp.int32, (SUBLANES, L), 0)
    xr = jnp.concatenate([jnp.where(grow < ML_HEADS, gr, bcum_r), jnp.zeros((LANES - SUBLANES, L), F32)], axis=0)
    x1, x2, x3 = _split3(xr)
    sel = sel_ref[...]
    cb_all = _dot_tn(x1, sel) + _dot_tn(x2, sel) + _dot_tn(x3, sel)
    src = lax.broadcasted_iota(jnp.int32, (L, L), 0)
    tgt = lax.broadcasted_iota(jnp.int32, (L, L), 1)
    causal = src <= tgt
    ones = jnp.ones((LANES, L), BF16)

    for h in range(ML_HEADS):
        fh = ML_HEADS + h
        cb = cb_all[:, h * L:(h + 1) * L]
        bt = bcum_r[fh:fh + 1, :]
        li_r = gr[h:h + 1, :]
        m_prev = m_ref[h:h + 1, 0:1]
        qt = qt_ref[h * ML_QK:(h + 1) * ML_QK, :]
        k = k_ref[:, h * ML_QK:(h + 1) * ML_QK]
        vaug = jnp.concatenate([vt_ref[h * ML_V:(h + 1) * ML_V, :], ones], axis=0)
        c_st = c_ref[h]

        dmat = jnp.where(causal, cb + bt, NEG_BIG)
        inter = bt + m_prev
        m_t = jnp.maximum(inter, jnp.max(dmat, axis=0, keepdims=True))
        w_intra = jnp.exp(dmat - m_t)
        w_inter = jnp.exp(inter - m_t)
        pt = (_dot(k, qt) * w_intra).astype(BF16)
        nd = _dot(vaug, pt) + w_inter * _dot(c_st.astype(BF16), qt)
        den = jnp.maximum(jnp.abs(nd[ML_V:ML_V + 1]), jnp.exp(-m_t))
        hc = nd[:ML_V] / den

        b_last = bt[:, L - 1:L]
        m_new = jnp.maximum(b_last + m_prev, jnp.max(b_last - bt + li_r, axis=-1, keepdims=True))
        ws = jnp.exp(cb[:, :ML_QK] + (b_last - m_new))
        wc = jnp.exp(b_last + m_prev - m_new)
        kw = (k.astype(F32) * ws).astype(BF16)
        c_ref[h] = wc * c_st + _dot(vaug, kw)
        m_ref[h:h + 1, :] = jnp.broadcast_to(m_new, (1, LANES))

        rows = slice(h * ML_V, (h + 1) * ML_V)
        hs = hc * lax.rsqrt(jnp.mean(hc * hc, axis=0, keepdims=True) + EPS) * hnorm_ref[rows, :]
        o_ref[rows, :] = (_sigmoid(opt_ref[rows, :].astype(F32)) * hs).astype(BF16)


def _mlstm_consts(n):
    r = np.arange(n)
    triu = (r[:, None] <= r[None, :]).astype(np.float32)
    sel = np.zeros((LANES, ML_HEADS * n), np.float32)
    for h in range(ML_HEADS):
        sel[h, h * n:(h + 1) * n] = 1.0
        sel[ML_HEADS + h, h * n:(h + 1) * n] = -1.0
    return jnp.asarray(triu, BF16), jnp.asarray(sel, BF16)


def _mlstm_layer(x, bsz, seq, g_mix, p, g_ffn, w1, w2):
    t, d = x.shape
    hq, hv = ML_HEADS * ML_QK, ML_HEADS * ML_V
    row = lambda i: (i, 0)
    col = lambda i: (0, i)
    qt, k, vt, opt, grow = pl.pallas_call(
        _mlstm_proj_kernel,
        grid=(t // PROJ_TILE,),
        in_specs=[pl.BlockSpec((PROJ_TILE, d), row), _const_spec((1, d)),
                  _const_spec((hq + 2 * hv + 2 * SUBLANES, d)),
                  _const_spec((d, hq)), _const_spec((SUBLANES, d)), _const_spec((SUBLANES, 1))],
        out_specs=[pl.BlockSpec((hq, PROJ_TILE), col), pl.BlockSpec((PROJ_TILE, hq), row),
                   pl.BlockSpec((hv, PROJ_TILE), col), pl.BlockSpec((hv, PROJ_TILE), col),
                   pl.BlockSpec((SUBLANES, PROJ_TILE), col)],
        out_shape=[jax.ShapeDtypeStruct((hq, t), BF16), jax.ShapeDtypeStruct((t, hq), BF16),
                   jax.ShapeDtypeStruct((hv, t), BF16), jax.ShapeDtypeStruct((hv, t), BF16),
                   jax.ShapeDtypeStruct((SUBLANES, t), F32)],
        compiler_params=_cparams("parallel"),
        name="mlstm_proj",
    )(x, g_mix, p["w_t"], p["w_k"], p["wr_hi"], p["b_row"])

    nc = seq // REC_TILE
    triu, sel = _mlstm_consts(REC_TILE)
    hnorm = jnp.broadcast_to(p["head_norm"].reshape(hv, 1), (hv, REC_TILE))
    rblk = lambda b, c: (b * nc + c, 0)
    cblk = lambda b, c: (0, b * nc + c)
    at = pl.pallas_call(
        _mlstm_rec_kernel,
        grid=(bsz, nc),
        in_specs=[pl.BlockSpec((hq, REC_TILE), cblk), pl.BlockSpec((REC_TILE, hq), rblk),
                  pl.BlockSpec((hv, REC_TILE), cblk), pl.BlockSpec((hv, REC_TILE), cblk),
                  pl.BlockSpec((SUBLANES, REC_TILE), cblk),
                  _const_spec((hv, REC_TILE)), _const_spec((REC_TILE, REC_TILE)),
                  _const_spec((LANES, ML_HEADS * REC_TILE))],
        out_specs=pl.BlockSpec((hv, REC_TILE), cblk),
        out_shape=jax.ShapeDtypeStruct((hv, t), BF16),
        scratch_shapes=[pltpu.VMEM((ML_HEADS, ML_V + LANES, ML_QK), F32), pltpu.VMEM((SUBLANES, LANES), F32)],
        compiler_params=_cparams("parallel", "arbitrary"),
        name="mlstm_rec",
    )(qt, k, vt, opt, grow, hnorm, triu, sel)
    return _mlp_call(x, at, p["w_o"], None, g_ffn, w1, w2, a_transposed=True)


def _mlstm_params(w_in, w_if, b_if, head_norm, w_o):
    ng = 2 * ML_HEADS
    hq = ML_HEADS * ML_QK
    w_row = w_if.T
    wr_hi = w_row.astype(BF16)
    wr_lo = (w_row - wr_hi.astype(F32)).astype(BF16)
    w_t = jnp.concatenate([w_in[:, :hq], w_in[:, 2 * hq:]], axis=1).T.astype(BF16)
    return dict(w_t=jnp.concatenate([w_t, wr_hi, wr_lo], axis=0), w_k=w_in[:, hq:2 * hq].astype(BF16),
                wr_hi=wr_hi, b_row=b_if.reshape(ng, 1),
                head_norm=head_norm, w_o=w_o.astype(BF16))


def _gla_proj_kernel(x_ref, g_ref, win_ref, wa1_ref, wa2_ref, ba_ref, q_ref, k_ref, v_ref, r_ref, la_ref):
    hn = _rms_rows(x_ref[...], g_ref[...]).astype(BF16)
    y = _dot(hn, win_ref[...])
    hk, hv = GLA_HEADS * GLA_K, GLA_HEADS * GLA_V
    q_ref[...] = (y[:, :hk] * (GLA_K ** -0.5)).astype(BF16)
    k_ref[...] = y[:, hk:2 * hk].astype(BF16)
    v_ref[...] = y[:, 2 * hk:2 * hk + hv].astype(BF16)
    r_ref[...] = y[:, 2 * hk + hv:].astype(BF16)
    z = _dot(_dot(hn, wa1_ref[...]).astype(BF16), wa2_ref[...]) + ba_ref[...]
    la_ref[...] = _log_sigmoid(z) * (1.0 / GLA_TAU)


def _gla_levels():
    return [2 ** j for j in range(1, int(math.log2(REC_TILE)) + 1)]


def _gla_consts():
    n = REC_TILE
    t = np.arange(n)[:, None]
    u = np.arange(n)[None, :]
    mats = []
    for p in _gla_levels():
        ref_row = (t // p) * p + p // 2
        upper = (t % p) >= p // 2
        m_up = (u > ref_row) & (u <= t)
        m_lo = (u > t) & (u <= ref_row)
        mats.append(np.where(upper, m_up, m_lo))
    mats.append(u <= t)
    mats.append(u > t)
    return jnp.asarray(np.stack(mats).astype(np.float32), BF16)


def _gla_rec_kernel(q_ref, k_ref, v_ref, r_ref, la_ref, hnorm_ref, w_ref, o_ref, c_ref):
    L = REC_TILE
    levels = _gla_levels()
    nl = len(levels)

    @pl.when(pl.program_id(1) == 0)
    def _():
        c_ref[...] = jnp.zeros_like(c_ref)

    la_hi, la_lo = _split2(la_ref[...])

    def decay(j, two_term):
        w = w_ref[j]
        x = _dot(w, la_hi)
        if two_term:
            x = x + _dot(w, la_lo)
        return jnp.exp(x)

    row = lax.broadcasted_iota(jnp.int32, (L, L), 0)
    col = lax.broadcasted_iota(jnp.int32, (L, L), 1)
    row_xor_col = row ^ col
    rowk = lax.broadcasted_iota(jnp.int32, (L, GLA_K), 0)
    e_cum = decay(nl, True)
    e_rev = decay(nl + 1, True)
    e_lvl = [decay(j, False) for j in range(nl)]

    for h in range(GLA_HEADS):
        ks = slice(h * GLA_K, (h + 1) * GLA_K)
        vs = slice(h * GLA_V, (h + 1) * GLA_V)
        q = q_ref[:, ks].astype(F32)
        k = k_ref[:, ks].astype(F32)
        v = v_ref[:, vs]
        att = jnp.where(row == col, _dot_nt(q_ref[:, ks], k_ref[:, ks]), 0.0)
        for j, p in enumerate(levels):
            upper = (rowk & (p - 1)) >= (p // 2)
            e = e_lvl[j][:, ks]
            qf = jnp.where(upper, q * e, 0.0).astype(BF16)
            kf = jnp.where(upper, 0.0, k * e).astype(BF16)
            a = _dot_nt(qf, kf)
            if p < L:
                a = jnp.where(row_xor_col < p, a, 0.0)
            att = att + a
        c_st = c_ref[h]
        o = _dot(att.astype(BF16), v) + _dot_nt((q * e_cum[:, ks]).astype(BF16), c_st.astype(BF16))
        c_ref[h] = e_cum[L - 1:L, ks] * c_st + _dot_tn(v, (k * e_rev[:, ks]).astype(BF16))
        o = _rms_rows(o, hnorm_ref[:, vs])
        rr = r_ref[:, vs].astype(F32)
        o_ref[:, vs] = (o * rr * _sigmoid(rr)).astype(BF16)


def _gla_layer(x, bsz, seq, g_mix, p, g_ffn, w1, w2):
    t, d = x.shape
    hk, hv = GLA_HEADS * GLA_K, GLA_HEADS * GLA_V
    row = lambda i: (i, 0)
    q, k, v, r, la = pl.pallas_call(
        _gla_proj_kernel,
        grid=(t // PROJ_TILE,),
        in_specs=[pl.BlockSpec((PROJ_TILE, d), row), _const_spec((1, d)), _const_spec((d, 2 * hk + 2 * hv)),
                  _const_spec((d, LANES)), _const_spec((LANES, hk)), _const_spec((1, hk))],
        out_specs=[pl.BlockSpec((PROJ_TILE, hk), row), pl.BlockSpec((PROJ_TILE, hk), row),
                   pl.BlockSpec((PROJ_TILE, hv), row), pl.BlockSpec((PROJ_TILE, hv), row),
                   pl.BlockSpec((PROJ_TILE, hk), row)],
        out_shape=[jax.ShapeDtypeStruct((t, hk), BF16), jax.ShapeDtypeStruct((t, hk), BF16),
                   jax.ShapeDtypeStruct((t, hv), BF16), jax.ShapeDtypeStruct((t, hv), BF16),
                   jax.ShapeDtypeStruct((t, hk), F32)],
        compiler_params=_cparams("parallel"),
        name="gla_proj",
    )(x, g_mix, p["w_in"], p["w_a1"], p["w_a2"], p["b_a"])

    nc = seq // REC_TILE
    wmats = _gla_consts()
    blk = lambda b, c: (b * nc + c, 0)
    a = pl.pallas_call(
        _gla_rec_kernel,
        grid=(bsz, nc),
        in_specs=[pl.BlockSpec((REC_TILE, hk), blk), pl.BlockSpec((REC_TILE, hk), blk),
                  pl.BlockSpec((REC_TILE, hv), blk), pl.BlockSpec((REC_TILE, hv), blk),
                  pl.BlockSpec((REC_TILE, hk), blk), _const_spec((1, hv)), _const_spec(wmats.shape)],
        out_specs=pl.BlockSpec((REC_TILE, hv), blk),
        out_shape=jax.ShapeDtypeStruct((t, hv), BF16),
        scratch_shapes=[pltpu.VMEM((GLA_HEADS, GLA_V, GLA_K), F32)],
        compiler_params=_cparams("parallel", "arbitrary"),
        name="gla_rec",
    )(q, k, v, r, la, p["head_norm"], wmats)
    return _mlp_call(x, a, p["w_o"], None, g_ffn, w1, w2)


def _gla_params(w_in, w_a1, w_a2, b_a, head_norm, w_o):
    return dict(w_in=w_in.astype(BF16),
                w_a1=jnp.pad(w_a1, ((0, 0), (0, LANES - GLA_GATE_RANK))).astype(BF16),
                w_a2=jnp.pad(w_a2, ((0, LANES - GLA_GATE_RANK), (0, 0))).astype(BF16),
                b_a=b_a.reshape(1, -1), head_norm=head_norm.reshape(1, -1), w_o=w_o.astype(BF16))


def _conv_kernel(x_ref, g_ref, w1_ref, b1_ref, wdw_ref, lng_ref, lnb_ref, o_ref, u_ref, c_ref):
    tm = TOK_TILE
    d = D_MODEL
    nbuf = CONV_HALO + tm

    @pl.when(pl.program_id(1) == 0)
    def _():
        u_ref[0, :, 0:CONV_HALO, :] = jnp.zeros((d // LANES, CONV_HALO, LANES), F32)

    @pl.when(pl.program_id(1) != 0)
    def _():
        u_ref[0, :, 0:CONV_HALO, :] = u_ref[0, :, tm:tm + CONV_HALO, :]

    hn = _rms_rows(x_ref[...], g_ref[...]).astype(BF16)
    y = _dot(hn, w1_ref[...]) + b1_ref[...]
    u = y[:, :d] * _sigmoid(y[:, d:])
    lead = CONV_HALO - (CONV_WIDTH - 1)
    groups = CONV_ROWS // SUBLANES
    tiles = nbuf // SUBLANES
    sub = lax.broadcasted_iota(jnp.int32, (tiles, SUBLANES, LANES), 1)

    for cb in range(d // LANES):
        cols = slice(cb * LANES, (cb + 1) * LANES)
        u_ref[0, cb, CONV_HALO:, :] = u[:, cols]
        full = u_ref[0, cb].reshape(tiles, SUBLANES, LANES)
        for s in range(1, SUBLANES):
            rot = pltpu.roll(full, SUBLANES - s, 1)
            nxt = jnp.concatenate([rot[1:], rot[:1]], axis=0)
            u_ref[s, cb] = jnp.where(sub < SUBLANES - s, rot, nxt).reshape(nbuf, LANES)

        def taps(c, carry):
            r0 = pl.multiple_of(c * CONV_ROWS, CONV_ROWS)
            acc = jnp.broadcast_to(wdw_ref[CONV_WIDTH, :, cols][None], (groups, SUBLANES, LANES))
            for j in range(CONV_WIDTH):
                off = lead + j
                start = pl.multiple_of(r0 + (off // SUBLANES) * SUBLANES, SUBLANES)
                uj = u_ref[off % SUBLANES, cb, pl.ds(start, CONV_ROWS), :].reshape(groups, SUBLANES, LANES)
                acc = acc + uj * wdw_ref[j, :, cols][None]
            c_ref[pl.ds(r0, CONV_ROWS), cols] = acc.reshape(CONV_ROWS, LANES)
            return carry

        lax.fori_loop(0, tm // CONV_ROWS, taps, 0)

    def norm(c, carry):
        r0 = pl.multiple_of(c * CONV_NORM_ROWS, CONV_NORM_ROWS)
        acc = c_ref[pl.ds(r0, CONV_NORM_ROWS), :]
        mu = jnp.mean(acc, axis=-1, keepdims=True)
        cen = acc - mu
        var = jnp.mean(cen * cen, axis=-1, keepdims=True)
        z = cen * lax.rsqrt(var + EPS) * lng_ref[...] + lnb_ref[...]
        o_ref[pl.ds(r0, CONV_NORM_ROWS), :] = (z * _sigmoid(z)).astype(BF16)
        return carry

    lax.fori_loop(0, tm // CONV_NORM_ROWS, norm, 0, unroll=8)


def _conv_layer(x, bsz, seq, g_mix, p, g_ffn, w1, w2):
    t, d = x.shape
    nt = seq // TOK_TILE
    blk = lambda b, i: (b * nt + i, 0)
    c = pl.pallas_call(
        _conv_kernel,
        grid=(bsz, nt),
        in_specs=[pl.BlockSpec((TOK_TILE, d), blk), _const_spec((1, d)), _const_spec((d, 2 * d)),
                  _const_spec((1, 2 * d)), _const_spec((CONV_WIDTH + 1, SUBLANES, d)),
                  _const_spec((1, d)), _const_spec((1, d))],
        out_specs=pl.BlockSpec((TOK_TILE, d), blk),
        out_shape=jax.ShapeDtypeStruct((t, d), BF16),
        scratch_shapes=[pltpu.VMEM((SUBLANES, d // LANES, CONV_HALO + TOK_TILE, LANES), F32),
                        pltpu.VMEM((TOK_TILE, d), F32)],
        compiler_params=_cparams("parallel", "arbitrary"),
        name="conv",
    )(x, g_mix, p["w_pw1"], p["b_pw1"], p["w_dw"], p["ln_g"], p["ln_b"])
    return _mlp_call(x, c, p["w_pw2"], p["b_pw2"], g_ffn, w1, w2)


def _conv_params(w_pw1, b_pw1, w_dw, b_dw, ln_g, ln_b, w_pw2, b_pw2):
    return dict(w_pw1=w_pw1.astype(BF16), b_pw1=b_pw1.reshape(1, -1),
                w_dw=jnp.broadcast_to(jnp.concatenate([w_dw, b_dw[None]], axis=0)[:, None, :],
                                      (CONV_WIDTH + 1, SUBLANES, w_dw.shape[1])),
                ln_g=ln_g.reshape(1, -1), ln_b=ln_b.reshape(1, -1),
                w_pw2=w_pw2.astype(BF16), b_pw2=b_pw2.reshape(1, -1))


def kernel(x, norm_mix, norm_ffn, mla_w_dq, mla_q_norm, mla_w_uq, mla_w_dkv, mla_kv_norm, mla_w_ukv, mla_q_gain, mla_k_gain, mla_w_o, mlstm_w_in, mlstm_w_if, mlstm_b_if, mlstm_head_norm, mlstm_w_o, gla_w_in, gla_w_a1, gla_w_a2, gla_b_a, gla_head_norm, gla_w_o, conv_w_pw1, conv_b_pw1, conv_w_dw, conv_b_dw, conv_ln_g, conv_ln_b, conv_w_pw2, conv_b_pw2, ffn_w1, ffn_w2):
    bsz, seq, d = x.shape
    depth = norm_mix.shape[0]
    assert d == D_MODEL and all(seq % tile == 0 for tile in (ATT_Q_TILE, TOK_TILE, MLP_TILE, PROJ_TILE, REC_TILE))
    h = x.reshape(bsz * seq, d)
    ffn_w1, ffn_w2 = ffn_w1.astype(BF16), ffn_w2.astype(BF16)
    for i in range(depth):
        kind, j = i % 4, i // 4
        g_mix = norm_mix[i].reshape(1, d)
        g_ffn = norm_ffn[i].reshape(1, d)
        w1, w2 = (ffn_w1, i), (ffn_w2, i)
        if kind == 0:
            p = _mla_params(seq, mla_w_dq[j], mla_q_norm[j], mla_w_uq[j], mla_w_dkv[j], mla_kv_norm[j],
                            mla_w_ukv[j], mla_q_gain[j], mla_k_gain[j], mla_w_o[j])
            h = _mla_layer(h, bsz, seq, g_mix, p, g_ffn, w1, w2)
        elif kind == 1:
            p = _mlstm_params(mlstm_w_in[j], mlstm_w_if[j], mlstm_b_if[j], mlstm_head_norm[j], mlstm_w_o[j])
            h = _mlstm_layer(h, bsz, seq, g_mix, p, g_ffn, w1, w2)
        elif kind == 2:
            p = _gla_params(gla_w_in[j], gla_w_a1[j], gla_w_a2[j], gla_b_a[j], gla_head_norm[j], gla_w_o[j])
            h = _gla_layer(h, bsz, seq, g_mix, p, g_ffn, w1, w2)
        else:
            p = _conv_params(conv_w_pw1[j], conv_b_pw1[j], conv_w_dw[j], conv_b_dw[j], conv_ln_g[j],
                             conv_ln_b[j], conv_w_pw2[j], conv_b_pw2[j])
            h = _conv_layer(h, bsz, seq, g_mix, p, g_ffn, w1, w2)
    return h.reshape(bsz, seq, d)
```

```python
import functools
import math

import numpy as np
import jax
import jax.numpy as jnp
from jax import lax
from jax.experimental import pallas as pl
from jax.experimental.pallas import tpu as pltpu

F32 = jnp.float32
BF16 = jnp.bfloat16

D_MODEL = 1024
D_FF = 4 * D_MODEL
EPS = 1e-6
CHUNK = 64

MLA_HEADS = 16
MLA_NOPE = 64
MLA_ROPE = 32
MLA_QK = MLA_NOPE + MLA_ROPE
MLA_V = 64
MLA_Q_RANK = 384
MLA_KV_RANK = 256
ROPE_BASE = 10000.0
MLA_VROWS = MLA_V + 16

ML_HEADS = 4
ML_QK = D_MODEL // 8
ML_V = D_MODEL // 4
GATE_CAP = 15.0

GLA_HEADS = 4
GLA_K = D_MODEL // 8
GLA_V = D_MODEL // 4
GLA_GATE_RANK = 16
GLA_TAU = 16.0

CONV_WIDTH = 31

LANES = 128
SUBLANES = 8
VMEM_LIMIT = 56 * 1024 * 1024

TOK_TILE = 512
MLP_TILE = 1024
PROJ_TILE = 1024
FF_TILE = 1024
ATT_TILE = 512
ATT_Q_TILE = 1024
ATT_HEADS = 4
REC_TILE = 256
ML_SUBTILES = 2
CONV_HALO = 32
CONV_ROWS = 128
CONV_NORM_ROWS = 32
NEG_BIG = -1e30


def _cparams(*sem):
    return pltpu.CompilerParams(dimension_semantics=sem, vmem_limit_bytes=VMEM_LIMIT)


def _const_spec(shape):
    nd = len(shape)
    return pl.BlockSpec(shape, lambda *_: (0,) * nd, pipeline_mode=pl.Buffered(1))


def _dot(a, b):
    return jnp.dot(a, b, preferred_element_type=F32)


def _dot_nt(a, b):
    return lax.dot_general(a, b, (((1,), (1,)), ((), ())), preferred_element_type=F32)


def _dot_tn(a, b):
    return lax.dot_general(a, b, (((0,), (0,)), ((), ())), preferred_element_type=F32)


def _split2(a):
    hi = a.astype(BF16)
    lo = (a - hi.astype(F32)).astype(BF16)
    return hi, lo


def _split3(a):
    hi = a.astype(BF16)
    r = a - hi.astype(F32)
    mid = r.astype(BF16)
    lo = (r - mid.astype(F32)).astype(BF16)
    return hi, mid, lo


def _rms_rows(x, g):
    return x * lax.rsqrt(jnp.mean(x * x, axis=-1, keepdims=True) + EPS) * g


def _log_sigmoid(z):
    return jnp.minimum(z, 0.0) - jnp.log1p(jnp.exp(-jnp.abs(z)))


def _sigmoid(z):
    return 1.0 / (1.0 + jnp.exp(-z))


def _mlp_kernel(*refs, has_bias, a_transposed):
    if has_bias:
        x_ref, a_ref, wo_ref, bo_ref, g_ref, w1_ref, w2_ref, o_ref = refs
    else:
        x_ref, a_ref, wo_ref, g_ref, w1_ref, w2_ref, o_ref = refs
    if a_transposed:
        x1 = x_ref[...] + _dot_tn(a_ref[...], wo_ref[...])
    else:
        x1 = x_ref[...] + _dot(a_ref[...], wo_ref[...])
    if has_bias:
        x1 = x1 + bo_ref[...]
    hn = _rms_rows(x1, g_ref[...]).astype(BF16)
    acc = x1
    for c in range(D_FF // FF_TILE):
        h = _dot(hn, w1_ref[:, c * FF_TILE:(c + 1) * FF_TILE])
        h = jnp.maximum(h, 0.0)
        acc = acc + _dot((h * h).astype(BF16), w2_ref[c * FF_TILE:(c + 1) * FF_TILE, :])
    o_ref[...] = acc


def _mlp_call(x, a, w_o, b_o, g, w1, w2, a_transposed=False):
    t, d = x.shape
    din = w_o.shape[0]
    has_bias = b_o is not None
    (w1, layer), (w2, _) = w1, w2
    layer_spec = lambda shape: pl.BlockSpec((None,) + shape, lambda *_: (layer, 0, 0), pipeline_mode=pl.Buffered(1))
    row = lambda i: (i, 0)
    a_spec = pl.BlockSpec((din, MLP_TILE), lambda i: (0, i)) if a_transposed else pl.BlockSpec((MLP_TILE, din), row)
    in_specs = [pl.BlockSpec((MLP_TILE, d), row), a_spec, _const_spec((din, d))]
    args = [x, a, w_o]
    if has_bias:
        in_specs.append(_const_spec((1, d)))
        args.append(b_o)
    in_specs += [_const_spec((1, d)), layer_spec((d, D_FF)), layer_spec((D_FF, d))]
    args += [g, w1, w2]
    return pl.pallas_call(
        functools.partial(_mlp_kernel, has_bias=has_bias, a_transposed=a_transposed),
        grid=(t // MLP_TILE,),
        in_specs=in_specs,
        out_specs=pl.BlockSpec((MLP_TILE, d), row),
        out_shape=jax.ShapeDtypeStruct((t, d), F32),
        compiler_params=_cparams("parallel"),
        name="mlp",
    )(*args)


def _mla_proj_kernel(x_ref, g_ref, wdq_ref, qn_ref, wuqt_ref, wdkv_ref, kvn_ref, wukt_ref, wuvt_ref,
                     vone_ref, qc_ref, qsa_ref, qsb_ref, kc_ref, ksa_ref, ksb_ref,
                     qt_ref, k_ref, vt_ref):
    half = MLA_ROPE // 2
    hn = _rms_rows(x_ref[...], g_ref[...]).astype(BF16)
    cq = _rms_rows(_dot(hn, wdq_ref[...]), qn_ref[...]).astype(BF16)
    dkv = _dot(hn, wdkv_ref[...])
    ckv = _rms_rows(dkv[:, :MLA_KV_RANK], kvn_ref[...]).astype(BF16)

    vt_ref[0] = (_dot_nt(wuvt_ref[...], ckv) + vone_ref[...]).astype(BF16)

    n0, a0, b0, e0 = 0, MLA_NOPE, MLA_NOPE + half, MLA_QK

    def ssq(t):
        return jnp.sum(t * t, axis=0, keepdims=True)

    def rope(a, b, c, sa, sb):
        return a * c[a0:b0] + b * sa[a0:b0], b * c[b0:e0] + a * sb[b0:e0]

    def inv_rms(s):
        return lax.rsqrt(s * (1.0 / MLA_QK) + EPS)

    tm = x_ref.shape[0]
    pad_rows = jnp.zeros((LANES - MLA_QK, tm), F32)

    qt = _dot_nt(wuqt_ref[...], cq)
    qc, qsa, qsb = qc_ref[...], qsa_ref[...], qsb_ref[...]
    for h in range(MLA_HEADS):
        t = qt[h * MLA_QK:(h + 1) * MLA_QK]
        n, a, b = t[n0:a0], t[a0:b0], t[b0:e0]
        r = inv_rms(ssq(n) + ssq(a) + ssq(b))
        ra, rb = rope(a, b, qc, qsa, qsb)
        qt_ref[h * LANES:h * LANES + MLA_QK, :] = jnp.concatenate(
            [n * qc[n0:a0] * r, ra * r, rb * r], axis=0).astype(BF16)
        qt_ref[h * LANES + MLA_QK:(h + 1) * LANES, :] = pad_rows.astype(BF16)

    krt = dkv[:, MLA_KV_RANK:].T
    kt = _dot_nt(wukt_ref[...], ckv)
    kc, ksa, ksb = kc_ref[...], ksa_ref[...], ksb_ref[...]
    ka, kb = krt[a0:b0], krt[b0:e0]
    ssq_rope = ssq(ka) + ssq(kb)
    ra, rb = rope(ka, kb, kc, ksa, ksb)
    for h in range(MLA_HEADS):
        n = kt[h * MLA_NOPE:(h + 1) * MLA_NOPE]
        r = inv_rms(ssq(n) + ssq_rope)
        kh = jnp.concatenate([n * kc[n0:a0] * r, ra * r, rb * r, pad_rows], axis=0)
        k_ref[:, h * LANES:(h + 1) * LANES] = kh.T.astype(BF16)


def _attn_kernel(qt_ref, k_ref, vt_ref, o_ref, s_ref, acc_ref, m_ref, mb_ref):
    i = pl.program_id(2)
    tq, tk = ATT_Q_TILE, ATT_TILE
    nh = ATT_HEADS
    kpq = tq // tk
    assert kpq == 2
    vrows = vt_ref.shape[1] // nh
    krow = lax.broadcasted_iota(jnp.int32, (tk, tq), 0)
    qcol = lax.broadcasted_iota(jnp.int32, (tk, tq), 1)
    diag_mask = (krow // CHUNK) <= (qcol // CHUNK)
    late = slice(tk, tq)

    def produce(h, kb, qcols=slice(None), mask=None):
        start = pl.multiple_of(kb * tk, tk)
        st = _dot(k_ref[pl.ds(start, tk), h * LANES:(h + 1) * LANES], qt_ref[h * LANES:(h + 1) * LANES, qcols])
        if mask is not None:
            st = jnp.where(mask, st, NEG_BIG)
        s_ref[h, :, qcols] = st
        mb_ref[h, :, qcols] = jnp.max(st, axis=0, keepdims=True)

    def consume(h, kb, mask=None, qcols=slice(None)):
        st = s_ref[h, :, qcols]
        if mask is None:
            mblk = mb_ref[h, :, qcols]
        else:
            st = jnp.where(mask, st, NEG_BIG)
            mblk = jnp.max(st, axis=0, keepdims=True)
        m = m_ref[h, :, qcols]
        m_new = jnp.maximum(m, mblk)
        m_ref[h, :, qcols] = m_new
        pt = jnp.exp2(st - m_new).astype(BF16)
        acc_ref[h, :, qcols] = (jnp.exp2(m - m_new) * acc_ref[h, :, qcols]
                                + _dot(vt_ref[kb, h * vrows:(h + 1) * vrows, :], pt))

    def body(j, carry):
        for h in range(nh):
            produce((h + 1) % nh, j + (h + 1) // nh)
            consume(h, j)
        return carry

    m_ref[...] = jnp.full(m_ref.shape, NEG_BIG, F32)
    acc_ref[...] = jnp.zeros(acc_ref.shape, F32)
    first = kpq * i
    produce(0, 0)
    lax.fori_loop(0, i, lambda t, c: body(kpq * t + 1, body(kpq * t, c)), 0)
    late_mask = diag_mask[:, :tk]
    for h in range(nh):
        if h + 1 < nh:
            produce(h + 1, first, mask=diag_mask)
        else:
            produce(0, first + 1, late, late_mask)
        consume(h, first, diag_mask if h == 0 else None)
    for h in range(nh):
        if h + 1 < nh:
            produce(h + 1, first + 1, late, late_mask)
        consume(h, first + 1, None, late)
        acc = acc_ref[h]
        o_ref[h * MLA_V:(h + 1) * MLA_V, :] = (acc[:MLA_V] / acc[MLA_V:MLA_V + 1]).astype(BF16)


def _mla_layer(x, bsz, seq, g_mix, p, g_ffn, w1, w2):
    t, d = x.shape
    assert TOK_TILE == ATT_TILE
    hp = MLA_HEADS * LANES
    hv = MLA_HEADS * MLA_VROWS
    nt = seq // TOK_TILE
    row = lambda i: (i, 0)
    ttab = lambda i: (i % nt, 0)
    ftab = lambda i: (0, i % nt)
    qt, k, vt = pl.pallas_call(
        _mla_proj_kernel,
        grid=(t // TOK_TILE,),
        in_specs=[pl.BlockSpec((TOK_TILE, d), row), _const_spec((1, d)),
                  _const_spec((d, MLA_Q_RANK)), _const_spec((1, MLA_Q_RANK)),
                  _const_spec((MLA_HEADS * MLA_QK, MLA_Q_RANK)),
                  _const_spec((d, MLA_KV_RANK + LANES)), _const_spec((1, MLA_KV_RANK)),
                  _const_spec((MLA_HEADS * MLA_NOPE, MLA_KV_RANK)), _const_spec((hv, MLA_KV_RANK)),
                  _const_spec((hv, TOK_TILE))] + [pl.BlockSpec((LANES, TOK_TILE), ftab)] * 6,
        out_specs=[pl.BlockSpec((hp, TOK_TILE), lambda i: (0, i)), pl.BlockSpec((TOK_TILE, hp), row),
                   pl.BlockSpec((1, hv, TOK_TILE), lambda i: (i, 0, 0))],
        out_shape=[jax.ShapeDtypeStruct((hp, t), BF16), jax.ShapeDtypeStruct((t, hp), BF16),
                   jax.ShapeDtypeStruct((t // TOK_TILE, hv, TOK_TILE), BF16)],
        compiler_params=_cparams("parallel"),
        name="mla_proj",
    )(x, g_mix, p["w_dq"], p["q_norm"], p["w_uq_t"], p["w_dkv"], p["kv_norm"], p["w_uk_t"], p["w_uv_t"],
      p["vone"], p["qc"], p["qsa"], p["qsb"], p["kc"], p["ksa"], p["ksb"])

    nq = seq // ATT_Q_TILE
    ot = pl.pallas_call(
        _attn_kernel,
        grid=(bsz, MLA_HEADS // ATT_HEADS, nq),
        in_specs=[pl.BlockSpec((ATT_HEADS * LANES, ATT_Q_TILE), lambda b, h, i: (h, b * nq + i)),
                  pl.BlockSpec((seq, ATT_HEADS * LANES), lambda b, h, i: (b, h)),
                  pl.BlockSpec((seq // ATT_TILE, ATT_HEADS * MLA_VROWS, ATT_TILE), lambda b, h, i: (b, h, 0))],
        out_specs=pl.BlockSpec((ATT_HEADS * MLA_V, ATT_Q_TILE), lambda b, h, i: (h, b * nq + i)),
        out_shape=jax.ShapeDtypeStruct((MLA_HEADS * MLA_V, t), BF16),
        scratch_shapes=[pltpu.VMEM((ATT_HEADS, ATT_TILE, ATT_Q_TILE), F32),
                        pltpu.VMEM((ATT_HEADS, MLA_VROWS, ATT_Q_TILE), F32),
                        pltpu.VMEM((ATT_HEADS, 1, ATT_Q_TILE), F32), pltpu.VMEM((ATT_HEADS, 1, ATT_Q_TILE), F32)],
        compiler_params=_cparams("parallel", "parallel", "arbitrary"),
        name="mla_attn",
    )(qt, k, vt)
    return _mlp_call(x, ot, p["w_o"], None, g_ffn, w1, w2, a_transposed=True)


def _mla_params(seq, w_dq, q_norm, w_uq, w_dkv, kv_norm, w_ukv, q_gain, k_gain, w_o):
    pad_head = LANES - MLA_QK
    half = MLA_ROPE // 2
    w_ukv_r = w_ukv.reshape(MLA_KV_RANK, MLA_HEADS, MLA_NOPE + MLA_V)
    w_uv_p = jnp.pad(w_ukv_r[:, :, MLA_NOPE:], ((0, 0), (0, 0), (0, MLA_VROWS - MLA_V)))
    w_rope = jnp.pad(w_dkv[:, MLA_KV_RANK:], ((0, 0), (MLA_NOPE, pad_head)))
    w_dkv_p = jnp.concatenate([w_dkv[:, :MLA_KV_RANK], w_rope], axis=1)
    vone = np.zeros((MLA_HEADS * MLA_VROWS, TOK_TILE), np.float32)
    vone[np.arange(MLA_HEADS) * MLA_VROWS + MLA_V] = 1.0
    gq = q_gain * ((MLA_QK ** -0.5) * math.log2(math.e))
    inv_freq = ROPE_BASE ** (-jnp.arange(half, dtype=F32) / half)
    ang = jnp.arange(seq).astype(F32)[:, None] * inv_freq[None, :]
    cos, sin = jnp.cos(ang), jnp.sin(ang)
    g1 = lambda g: g[MLA_NOPE:MLA_NOPE + half][None, :]
    g2 = lambda g: g[MLA_NOPE + half:MLA_QK][None, :]
    z = lambda n: jnp.zeros((seq, n), F32)

    def tables(g):
        nope = jnp.broadcast_to(g[None, :MLA_NOPE], (seq, MLA_NOPE))
        c = jnp.concatenate([nope, g1(g) * cos, g2(g) * cos, z(pad_head)], axis=1)
        sa = jnp.concatenate([z(MLA_NOPE), -g2(g) * sin, z(half + pad_head)], axis=1)
        sb = jnp.concatenate([z(MLA_NOPE + half), g1(g) * sin, z(pad_head)], axis=1)
        return c.T, sa.T, sb.T

    qc, qsa, qsb = tables(gq)
    kc, ksa, ksb = tables(k_gain)
    return dict(w_dq=w_dq.astype(BF16), q_norm=q_norm.reshape(1, -1),
                w_uq_t=w_uq.T.astype(BF16), w_dkv=w_dkv_p.astype(BF16), kv_norm=kv_norm.reshape(1, -1),
                w_uk_t=w_ukv_r[:, :, :MLA_NOPE].reshape(MLA_KV_RANK, -1).T.astype(BF16),
                w_uv_t=w_uv_p.reshape(MLA_KV_RANK, -1).T.astype(BF16), vone=jnp.asarray(vone, BF16),
                qc=qc, qsa=qsa, qsb=qsb, kc=kc, ksa=ksa, ksb=ksb, w_o=w_o.astype(BF16))


def _mlstm_proj_kernel(x_ref, g_ref, wt_ref, wk_ref, wr_hi_ref, br_ref,
                       qt_ref, k_ref, vt_ref, ot_ref, gr_ref):
    hn = _rms_rows(x_ref[...], g_ref[...])
    hi, lo = _split2(hn)
    hq, hv = ML_HEADS * ML_QK, ML_HEADS * ML_V
    yt = _dot_nt(wt_ref[...], hi)
    qt_ref[...] = yt[:hq].astype(BF16)
    vt_ref[...] = yt[hq:hq + hv].astype(BF16)
    ot_ref[...] = yt[hq + hv:hq + 2 * hv].astype(BF16)
    k_ref[...] = (_dot(hi, wk_ref[...]) * (ML_QK ** -0.5)).astype(BF16)
    g0 = hq + 2 * hv
    gr_ref[...] = (yt[g0:g0 + SUBLANES] + yt[g0 + SUBLANES:] + _dot_nt(wr_hi_ref[...], lo)) + br_ref[...]


def _cap(g):
    return GATE_CAP * jnp.tanh(g * (1.0 / GATE_CAP))


def _mlstm_rec_kernel(qt_ref, k_ref, vt_ref, opt_ref, gr_ref, hnorm_ref, triu_ref, sel_ref,
                      o_ref, c_ref, m_ref):
    L = REC_TILE

    @pl.when(pl.program_id(1) == 0)
    def _():
        c_ref[...] = jnp.zeros_like(c_ref)
        m_ref[...] = jnp.zeros_like(m_ref)

    triu = triu_ref[...]
    sel = sel_ref[...]
    grow = lax.broadcasted_iota(jnp.int32, (SUBLANES, L), 0)
    src = lax.broadcasted_iota(jnp.int32, (L, L), 0)
    tgt = lax.broadcasted_iota(jnp.int32, (L, L), 1)
    causal = src <= tgt
    ones = jnp.ones((LANES, L), BF16)

    for sub in range(ML_SUBTILES):
        tok = slice(sub * L, (sub + 1) * L)
        gr = _cap(gr_ref[:, tok])
        r1, r2, r3 = _split3(_log_sigmoid(gr))
        bcum_r = _dot(r1, triu) + _dot(r2, triu) + _dot(r3, triu)
        xr = jnp.concatenate([jnp.where(grow < ML_HEADS, gr, bcum_r), jnp.zeros((LANES - SUBLANES, L), F32)],
                             axis=0)
        x1, x2, x3 = _split3(xr)
        cb_all = _dot_tn(x1, sel) + _dot_tn(x2, sel) + _dot_tn(x3, sel)

        for h in range(ML_HEADS):
            fh = ML_HEADS + h
            cb = cb_all[:, h * L:(h + 1) * L]
            bt = bcum_r[fh:fh + 1, :]
            li_r = gr[h:h + 1, :]
            m_prev = m_ref[h:h + 1, 0:1]
            qt = qt_ref[h * ML_QK:(h + 1) * ML_QK, tok]
            k = k_ref[tok, h * ML_QK:(h + 1) * ML_QK]
            vaug = jnp.concatenate([vt_ref[h * ML_V:(h + 1) * ML_V, tok], ones], axis=0)
            c_st = c_ref[h]

            dmat = jnp.where(causal, cb + bt, NEG_BIG)
            inter = bt + m_prev
            m_t = jnp.maximum(inter, jnp.max(dmat, axis=0, keepdims=True))
            w_intra = jnp.exp(dmat - m_t)
            w_inter = jnp.exp(inter - m_t)
            pt = (_dot(k, qt) * w_intra).astype(BF16)
            nd = _dot(vaug, pt) + w_inter * _dot(c_st.astype(BF16), qt)
            den = jnp.maximum(jnp.abs(nd[ML_V:ML_V + 1]), jnp.exp(-m_t))
            hc = nd[:ML_V] / den

            b_last = bt[:, L - 1:L]
            m_new = jnp.maximum(b_last + m_prev, jnp.max(b_last - bt + li_r, axis=-1, keepdims=True))
            ws = jnp.exp(cb[:, :ML_QK] + (b_last - m_new))
            wc = jnp.exp(b_last + m_prev - m_new)
            kw = (k.astype(F32) * ws).astype(BF16)
            c_ref[h] = wc * c_st + _dot(vaug, kw)
            m_ref[h:h + 1, :] = jnp.broadcast_to(m_new, (1, LANES))

            rows = slice(h * ML_V, (h + 1) * ML_V)
            hs = hc * lax.rsqrt(jnp.mean(hc * hc, axis=0, keepdims=True) + EPS) * hnorm_ref[rows, :]
            o_ref[rows, tok] = (_sigmoid(opt_ref[rows, tok].astype(F32)) * hs).astype(BF16)


def _mlstm_consts(n):
    r = np.arange(n)
    triu = (r[:, None] <= r[None, :]).astype(np.float32)
    sel = np.zeros((LANES, ML_HEADS * n), np.float32)
    for h in range(ML_HEADS):
        sel[h, h * n:(h + 1) * n] = 1.0
        sel[ML_HEADS + h, h * n:(h + 1) * n] = -1.0
    return jnp.asarray(triu, BF16), jnp.asarray(sel, BF16)


def _mlstm_layer(x, bsz, seq, g_mix, p, g_ffn, w1, w2):
    t, d = x.shape
    hq, hv = ML_HEADS * ML_QK, ML_HEADS * ML_V
    row = lambda i: (i, 0)
    col = lambda i: (0, i)
    qt, k, vt, opt, grow = pl.pallas_call(
        _mlstm_proj_kernel,
        grid=(t // PROJ_TILE,),
        in_specs=[pl.BlockSpec((PROJ_TILE, d), row), _const_spec((1, d)),
                  _const_spec((hq + 2 * hv + 2 * SUBLANES, d)),
                  _const_spec((d, hq)), _const_spec((SUBLANES, d)), _const_spec((SUBLANES, 1))],
        out_specs=[pl.BlockSpec((hq, PROJ_TILE), col), pl.BlockSpec((PROJ_TILE, hq), row),
                   pl.BlockSpec((hv, PROJ_TILE), col), pl.BlockSpec((hv, PROJ_TILE), col),
                   pl.BlockSpec((SUBLANES, PROJ_TILE), col)],
        out_shape=[jax.ShapeDtypeStruct((hq, t), BF16), jax.ShapeDtypeStruct((t, hq), BF16),
                   jax.ShapeDtypeStruct((hv, t), BF16), jax.ShapeDtypeStruct((hv, t), BF16),
                   jax.ShapeDtypeStruct((SUBLANES, t), F32)],
        compiler_params=_cparams("parallel"),
        name="mlstm_proj",
    )(x, g_mix, p["w_t"], p["w_k"], p["wr_hi"], p["b_row"])

    step = ML_SUBTILES * REC_TILE
    nc = seq // step
    triu, sel = _mlstm_consts(REC_TILE)
    hnorm = jnp.broadcast_to(p["head_norm"].reshape(hv, 1), (hv, REC_TILE))
    rblk = lambda b, c: (b * nc + c, 0)
    cblk = lambda b, c: (0, b * nc + c)
    at = pl.pallas_call(
        _mlstm_rec_kernel,
        grid=(bsz, nc),
        in_specs=[pl.BlockSpec((hq, step), cblk), pl.BlockSpec((step, hq), rblk),
                  pl.BlockSpec((hv, step), cblk), pl.BlockSpec((hv, step), cblk),
                  pl.BlockSpec((SUBLANES, step), cblk),
                  _const_spec((hv, REC_TILE)), _const_spec((REC_TILE, REC_TILE)),
                  _const_spec((LANES, ML_HEADS * REC_TILE))],
        out_specs=pl.BlockSpec((hv, step), cblk),
        out_shape=jax.ShapeDtypeStruct((hv, t), BF16),
        scratch_shapes=[pltpu.VMEM((ML_HEADS, ML_V + LANES, ML_QK), F32), pltpu.VMEM((SUBLANES, LANES), F32)],
        compiler_params=_cparams("parallel", "arbitrary"),
        name="mlstm_rec",
    )(qt, k, vt, opt, grow, hnorm, triu, sel)
    return _mlp_call(x, at, p["w_o"], None, g_ffn, w1, w2, a_transposed=True)


def _mlstm_params(w_in, w_if, b_if, head_norm, w_o):
    ng = 2 * ML_HEADS
    hq = ML_HEADS * ML_QK
    w_row = w_if.T
    wr_hi = w_row.astype(BF16)
    wr_lo = (w_row - wr_hi.astype(F32)).astype(BF16)
    w_t = jnp.concatenate([w_in[:, :hq], w_in[:, 2 * hq:]], axis=1).T.astype(BF16)
    return dict(w_t=jnp.concatenate([w_t, wr_hi, wr_lo], axis=0), w_k=w_in[:, hq:2 * hq].astype(BF16),
                wr_hi=wr_hi, b_row=b_if.reshape(ng, 1),
                head_norm=head_norm, w_o=w_o.astype(BF16))


def _gla_proj_kernel(x_ref, g_ref, win_ref, wa1_ref, wa2_ref, ba_ref, q_ref, k_ref, v_ref, r_ref, la_ref):
    hn = _rms_rows(x_ref[...], g_ref[...]).astype(BF16)
    y = _dot(hn, win_ref[...])
    hk, hv = GLA_HEADS * GLA_K, GLA_HEADS * GLA_V
    q_ref[...] = (y[:, :hk] * (GLA_K ** -0.5)).astype(BF16)
    k_ref[...] = y[:, hk:2 * hk].astype(BF16)
    v_ref[...] = y[:, 2 * hk:2 * hk + hv].astype(BF16)
    r_ref[...] = y[:, 2 * hk + hv:].astype(BF16)
    z = _dot(_dot(hn, wa1_ref[...]).astype(BF16), wa2_ref[...]) + ba_ref[...]
    la_ref[...] = _log_sigmoid(z) * (1.0 / GLA_TAU)


def _gla_levels():
    return [2 ** j for j in range(1, int(math.log2(REC_TILE)) + 1)]


def _gla_consts():
    n = REC_TILE
    t = np.arange(n)[:, None]
    u = np.arange(n)[None, :]
    mats = []
    for p in _gla_levels():
        ref_row = (t // p) * p + p // 2
        upper = (t % p) >= p // 2
        m_up = (u > ref_row) & (u <= t)
        m_lo = (u > t) & (u <= ref_row)
        mats.append(np.where(upper, m_up, m_lo))
    mats.append(u <= t)
    mats.append(u > t)
    return jnp.asarray(np.stack(mats).astype(np.float32), BF16)


def _gla_rec_kernel(q_ref, k_ref, v_ref, r_ref, la_ref, hnorm_ref, w_ref, o_ref, c_ref):
    L = REC_TILE
    levels = _gla_levels()
    nl = len(levels)

    @pl.when(pl.program_id(1) == 0)
    def _():
        c_ref[...] = jnp.zeros_like(c_ref)

    la_hi, la_lo = _split2(la_ref[...])

    def decay(j, two_term):
        w = w_ref[j]
        x = _dot(w, la_hi)
        if two_term:
            x = x + _dot(w, la_lo)
        return jnp.exp(x)

    row = lax.broadcasted_iota(jnp.int32, (L, L), 0)
    col = lax.broadcasted_iota(jnp.int32, (L, L), 1)
    row_xor_col = row ^ col
    rowk = lax.broadcasted_iota(jnp.int32, (L, GLA_K), 0)
    e_cum = decay(nl, True)
    e_rev = decay(nl + 1, True)
    e_lvl = [decay(j, False) for j in range(nl)]

    for h in range(GLA_HEADS):
        ks = slice(h * GLA_K, (h + 1) * GLA_K)
        vs = slice(h * GLA_V, (h + 1) * GLA_V)
        q = q_ref[:, ks].astype(F32)
        k = k_ref[:, ks].astype(F32)
        v = v_ref[:, vs]
        att = jnp.where(row == col, _dot_nt(q_ref[:, ks], k_ref[:, ks]), 0.0)
        for j, p in enumerate(levels):
            upper = (rowk & (p - 1)) >= (p // 2)
            e = e_lvl[j][:, ks]
            qf = jnp.where(upper, q * e, 0.0).astype(BF16)
            kf = jnp.where(upper, 0.0, k * e).astype(BF16)
            a = _dot_nt(qf, kf)
            if p < L:
                a = jnp.where(row_xor_col < p, a, 0.0)
            att = att + a
        c_st = c_ref[h]
        o = _dot(att.astype(BF16), v) + _dot_nt((q * e_cum[:, ks]).astype(BF16), c_st.astype(BF16))
        c_ref[h] = e_cum[L - 1:L, ks] * c_st + _dot_tn(v, (k * e_rev[:, ks]).astype(BF16))
        o = _rms_rows(o, hnorm_ref[:, vs])
        rr = r_ref[:, vs].astype(F32)
        o_ref[:, vs] = (o * rr * _sigmoid(rr)).astype(BF16)


def _gla_layer(x, bsz, seq, g_mix, p, g_ffn, w1, w2):
    t, d = x.shape
    hk, hv = GLA_HEADS * GLA_K, GLA_HEADS * GLA_V
    row = lambda i: (i, 0)
    q, k, v, r, la = pl.pallas_call(
        _gla_proj_kernel,
        grid=(t // PROJ_TILE,),
        in_specs=[pl.BlockSpec((PROJ_TILE, d), row), _const_spec((1, d)), _const_spec((d, 2 * hk + 2 * hv)),
                  _const_spec((d, LANES)), _const_spec((LANES, hk)), _const_spec((1, hk))],
        out_specs=[pl.BlockSpec((PROJ_TILE, hk), row), pl.BlockSpec((PROJ_TILE, hk), row),
                   pl.BlockSpec((PROJ_TILE, hv), row), pl.BlockSpec((PROJ_TILE, hv), row),
                   pl.BlockSpec((PROJ_TILE, hk), row)],
        out_shape=[jax.ShapeDtypeStruct((t, hk), BF16), jax.ShapeDtypeStruct((t, hk), BF16),
                   jax.ShapeDtypeStruct((t, hv), BF16), jax.ShapeDtypeStruct((t, hv), BF16),
                   jax.ShapeDtypeStruct((t, hk), F32)],
        compiler_params=_cparams("parallel"),
        name="gla_proj",
    )(x, g_mix, p["w_in"], p["w_a1"], p["w_a2"], p["b_a"])

    nc = seq // REC_TILE
    wmats = _gla_consts()
    blk = lambda b, c: (b * nc + c, 0)
    a = pl.pallas_call(
        _gla_rec_kernel,
        grid=(bsz, nc),
        in_specs=[pl.BlockSpec((REC_TILE, hk), blk), pl.BlockSpec((REC_TILE, hk), blk),
                  pl.BlockSpec((REC_TILE, hv), blk), pl.BlockSpec((REC_TILE, hv), blk),
                  pl.BlockSpec((REC_TILE, hk), blk), _const_spec((1, hv)), _const_spec(wmats.shape)],
        out_specs=pl.BlockSpec((REC_TILE, hv), blk),
        out_shape=jax.ShapeDtypeStruct((t, hv), BF16),
        scratch_shapes=[pltpu.VMEM((GLA_HEADS, GLA_V, GLA_K), F32)],
        compiler_params=_cparams("parallel", "arbitrary"),
        name="gla_rec",
    )(q, k, v, r, la, p["head_norm"], wmats)
    return _mlp_call(x, a, p["w_o"], None, g_ffn, w1, w2)


def _gla_params(w_in, w_a1, w_a2, b_a, head_norm, w_o):
    return dict(w_in=w_in.astype(BF16),
                w_a1=jnp.pad(w_a1, ((0, 0), (0, LANES - GLA_GATE_RANK))).astype(BF16),
                w_a2=jnp.pad(w_a2, ((0, LANES - GLA_GATE_RANK), (0, 0))).astype(BF16),
                b_a=b_a.reshape(1, -1), head_norm=head_norm.reshape(1, -1), w_o=w_o.astype(BF16))


def _conv_kernel(x_ref, g_ref, w1_ref, b1_ref, wdw_ref, lng_ref, lnb_ref, o_ref, u_ref, c_ref):
    tm = TOK_TILE
    d = D_MODEL
    nbuf = CONV_HALO + tm

    @pl.when(pl.program_id(1) == 0)
    def _():
        u_ref[0, :, 0:CONV_HALO, :] = jnp.zeros((d // LANES, CONV_HALO, LANES), F32)

    @pl.when(pl.program_id(1) != 0)
    def _():
        u_ref[0, :, 0:CONV_HALO, :] = u_ref[0, :, tm:tm + CONV_HALO, :]

    hn = _rms_rows(x_ref[...], g_ref[...]).astype(BF16)
    y = _dot(hn, w1_ref[...]) + b1_ref[...]
    u = y[:, :d] * _sigmoid(y[:, d:])
    lead = CONV_HALO - (CONV_WIDTH - 1)
    groups = CONV_ROWS // SUBLANES
    tiles = nbuf // SUBLANES
    sub = lax.broadcasted_iota(jnp.int32, (tiles, SUBLANES, LANES), 1)

    for cb in range(d // LANES):
        cols = slice(cb * LANES, (cb + 1) * LANES)
        u_ref[0, cb, CONV_HALO:, :] = u[:, cols]
        full = u_ref[0, cb].reshape(tiles, SUBLANES, LANES)
        for s in range(1, SUBLANES):
            rot = pltpu.roll(full, SUBLANES - s, 1)
            nxt = jnp.concatenate([rot[1:], rot[:1]], axis=0)
            u_ref[s, cb] = jnp.where(sub < SUBLANES - s, rot, nxt).reshape(nbuf, LANES)

        def taps(c, carry):
            r0 = pl.multiple_of(c * CONV_ROWS, CONV_ROWS)
            acc = jnp.broadcast_to(wdw_ref[CONV_WIDTH, :, cols][None], (groups, SUBLANES, LANES))
            for j in range(CONV_WIDTH):
                off = lead + j
                start = pl.multiple_of(r0 + (off // SUBLANES) * SUBLANES, SUBLANES)
                uj = u_ref[off % SUBLANES, cb, pl.ds(start, CONV_ROWS), :].reshape(groups, SUBLANES, LANES)
                acc = acc + uj * wdw_ref[j, :, cols][None]
            c_ref[pl.ds(r0, CONV_ROWS), cols] = acc.reshape(CONV_ROWS, LANES)
            return carry

        lax.fori_loop(0, tm // CONV_ROWS, taps, 0)

    def norm(c, carry):
        r0 = pl.multiple_of(c * CONV_NORM_ROWS, CONV_NORM_ROWS)
        acc = c_ref[pl.ds(r0, CONV_NORM_ROWS), :]
        mu = jnp.mean(acc, axis=-1, keepdims=True)
        cen = acc - mu
        var = jnp.mean(cen * cen, axis=-1, keepdims=True)
        z = cen * lax.rsqrt(var + EPS) * lng_ref[...] + lnb_ref[...]
        o_ref[pl.ds(r0, CONV_NORM_ROWS), :] = (z * _sigmoid(z)).astype(BF16)
        return carry

    lax.fori_loop(0, tm // CONV_NORM_ROWS, norm, 0, unroll=8)


def _conv_layer(x, bsz, seq, g_mix, p, g_ffn, w1, w2):
    t, d = x.shape
    nt = seq // TOK_TILE
    blk = lambda b, i: (b * nt + i, 0)
    c = pl.pallas_call(
        _conv_kernel,
        grid=(bsz, nt),
        in_specs=[pl.BlockSpec((TOK_TILE, d), blk), _const_spec((1, d)), _const_spec((d, 2 * d)),
                  _const_spec((1, 2 * d)), _const_spec((CONV_WIDTH + 1, SUBLANES, d)),
                  _const_spec((1, d)), _const_spec((1, d))],
        out_specs=pl.BlockSpec((TOK_TILE, d), blk),
        out_shape=jax.ShapeDtypeStruct((t, d), BF16),
        scratch_shapes=[pltpu.VMEM((SUBLANES, d // LANES, CONV_HALO + TOK_TILE, LANES), F32),
                        pltpu.VMEM((TOK_TILE, d), F32)],
        compiler_params=_cparams("parallel", "arbitrary"),
        name="conv",
    )(x, g_mix, p["w_pw1"], p["b_pw1"], p["w_dw"], p["ln_g"], p["ln_b"])
    return _mlp_call(x, c, p["w_pw2"], p["b_pw2"], g_ffn, w1, w2)


def _conv_params(w_pw1, b_pw1, w_dw, b_dw, ln_g, ln_b, w_pw2, b_pw2):
    return dict(w_pw1=w_pw1.astype(BF16), b_pw1=b_pw1.reshape(1, -1),
                w_dw=jnp.broadcast_to(jnp.concatenate([w_dw, b_dw[None]], axis=0)[:, None, :],
                                      (CONV_WIDTH + 1, SUBLANES, w_dw.shape[1])),
                ln_g=ln_g.reshape(1, -1), ln_b=ln_b.reshape(1, -1),
                w_pw2=w_pw2.astype(BF16), b_pw2=b_pw2.reshape(1, -1))


def kernel(x, norm_mix, norm_ffn, mla_w_dq, mla_q_norm, mla_w_uq, mla_w_dkv, mla_kv_norm, mla_w_ukv, mla_q_gain, mla_k_gain, mla_w_o, mlstm_w_in, mlstm_w_if, mlstm_b_if, mlstm_head_norm, mlstm_w_o, gla_w_in, gla_w_a1, gla_w_a2, gla_b_a, gla_head_norm, gla_w_o, conv_w_pw1, conv_b_pw1, conv_w_dw, conv_b_dw, conv_ln_g, conv_ln_b, conv_w_pw2, conv_b_pw2, ffn_w1, ffn_w2):
    bsz, seq, d = x.shape
    depth = norm_mix.shape[0]
    assert d == D_MODEL and all(seq % tile == 0 for tile in (ATT_Q_TILE, TOK_TILE, MLP_TILE, PROJ_TILE, REC_TILE))
    h = x.reshape(bsz * seq, d)
    ffn_w1, ffn_w2 = ffn_w1.astype(BF16), ffn_w2.astype(BF16)
    for i in range(depth):
        kind, j = i % 4, i // 4
        g_mix = norm_mix[i].reshape(1, d)
        g_ffn = norm_ffn[i].reshape(1, d)
        w1, w2 = (ffn_w1, i), (ffn_w2, i)
        if kind == 0:
            p = _mla_params(seq, mla_w_dq[j], mla_q_norm[j], mla_w_uq[j], mla_w_dkv[j], mla_kv_norm[j],
                            mla_w_ukv[j], mla_q_gain[j], mla_k_gain[j], mla_w_o[j])
            h = _mla_layer(h, bsz, seq, g_mix, p, g_ffn, w1, w2)
        elif kind == 1:
            p = _mlstm_params(mlstm_w_in[j], mlstm_w_if[j], mlstm_b_if[j], mlstm_head_norm[j], mlstm_w_o[j])
            h = _mlstm_layer(h, bsz, seq, g_mix, p, g_ffn, w1, w2)
        elif kind == 2:
            p = _gla_params(gla_w_in[j], gla_w_a1[j], gla_w_a2[j], gla_b_a[j], gla_head_norm[j], gla_w_o[j])
            h = _gla_layer(h, bsz, seq, g_mix, p, g_ffn, w1, w2)
        else:
            p = _conv_params(conv_w_pw1[j], conv_b_pw1[j], conv_w_dw[j], conv_b_dw[j], conv_ln_g[j],
                             conv_ln_b[j], conv_w_pw2[j], conv_b_pw2[j])
            h = _conv_layer(h, bsz, seq, g_mix, p, g_ffn, w1, w2)
    return h.reshape(bsz, seq, d)
```

```python
import functools
import math

import numpy as np
import jax
import jax.numpy as jnp
from jax import lax
from jax.experimental import pallas as pl
from jax.experimental.pallas import tpu as pltpu

F32 = jnp.float32
BF16 = jnp.bfloat16

D_MODEL = 1024
D_FF = 4 * D_MODEL
EPS = 1e-6
CHUNK = 64

MLA_HEADS = 16
MLA_NOPE = 64
MLA_ROPE = 32
MLA_QK = MLA_NOPE + MLA_ROPE
MLA_V = 64
MLA_Q_RANK = 384
MLA_KV_RANK = 256
ROPE_BASE = 10000.0
MLA_VROWS = MLA_V + 16

ML_HEADS = 4
ML_QK = D_MODEL // 8
ML_V = D_MODEL // 4
GATE_CAP = 15.0

GLA_HEADS = 4
GLA_K = D_MODEL // 8
GLA_V = D_MODEL // 4
GLA_GATE_RANK = 16
GLA_TAU = 16.0

CONV_WIDTH = 31

LANES = 128
SUBLANES = 8
VMEM_LIMIT = 56 * 1024 * 1024

TOK_TILE = 512
MLP_TILE = 1024
PROJ_TILE = 1024
FF_TILE = 1024
ATT_TILE = 512
ATT_Q_TILE = 1024
ATT_HEADS = 4
REC_TILE = 256
ML_SUBTILES = 2
GLA_SUBTILES = 2
CONV_HALO = 32
CONV_ROWS = 128
CONV_NORM_ROWS = 32
NEG_BIG = -1e30


def _cparams(*sem):
    return pltpu.CompilerParams(dimension_semantics=sem, vmem_limit_bytes=VMEM_LIMIT)


def _const_spec(shape):
    nd = len(shape)
    return pl.BlockSpec(shape, lambda *_: (0,) * nd, pipeline_mode=pl.Buffered(1))


def _dot(a, b):
    return jnp.dot(a, b, preferred_element_type=F32)


def _dot_nt(a, b):
    return lax.dot_general(a, b, (((1,), (1,)), ((), ())), preferred_element_type=F32)


def _dot_tn(a, b):
    return lax.dot_general(a, b, (((0,), (0,)), ((), ())), preferred_element_type=F32)


def _split2(a):
    hi = a.astype(BF16)
    lo = (a - hi.astype(F32)).astype(BF16)
    return hi, lo


def _split3(a):
    hi = a.astype(BF16)
    r = a - hi.astype(F32)
    mid = r.astype(BF16)
    lo = (r - mid.astype(F32)).astype(BF16)
    return hi, mid, lo


def _rms_rows(x, g):
    return x * lax.rsqrt(jnp.mean(x * x, axis=-1, keepdims=True) + EPS) * g


def _log_sigmoid(z):
    return jnp.minimum(z, 0.0) - jnp.log1p(jnp.exp(-jnp.abs(z)))


def _sigmoid(z):
    return 1.0 / (1.0 + jnp.exp(-z))


def _mlp_kernel(*refs, has_bias, a_transposed):
    if has_bias:
        x_ref, a_ref, wo_ref, bo_ref, g_ref, w1_ref, w2_ref, o_ref = refs
    else:
        x_ref, a_ref, wo_ref, g_ref, w1_ref, w2_ref, o_ref = refs
    if a_transposed:
        x1 = x_ref[...] + _dot_tn(a_ref[...], wo_ref[...])
    else:
        x1 = x_ref[...] + _dot(a_ref[...], wo_ref[...])
    if has_bias:
        x1 = x1 + bo_ref[...]
    hn = _rms_rows(x1, g_ref[...]).astype(BF16)
    acc = x1
    for c in range(D_FF // FF_TILE):
        h = _dot(hn, w1_ref[:, c * FF_TILE:(c + 1) * FF_TILE])
        h = jnp.maximum(h, 0.0)
        acc = acc + _dot((h * h).astype(BF16), w2_ref[c * FF_TILE:(c + 1) * FF_TILE, :])
    o_ref[...] = acc


def _mlp_call(x, a, w_o, b_o, g, w1, w2, a_transposed=False):
    t, d = x.shape
    din = w_o.shape[0]
    has_bias = b_o is not None
    (w1, layer), (w2, _) = w1, w2
    layer_spec = lambda shape: pl.BlockSpec((None,) + shape, lambda *_: (layer, 0, 0), pipeline_mode=pl.Buffered(1))
    row = lambda i: (i, 0)
    a_spec = pl.BlockSpec((din, MLP_TILE), lambda i: (0, i)) if a_transposed else pl.BlockSpec((MLP_TILE, din), row)
    in_specs = [pl.BlockSpec((MLP_TILE, d), row), a_spec, _const_spec((din, d))]
    args = [x, a, w_o]
    if has_bias:
        in_specs.append(_const_spec((1, d)))
        args.append(b_o)
    in_specs += [_const_spec((1, d)), layer_spec((d, D_FF)), layer_spec((D_FF, d))]
    args += [g, w1, w2]
    return pl.pallas_call(
        functools.partial(_mlp_kernel, has_bias=has_bias, a_transposed=a_transposed),
        grid=(t // MLP_TILE,),
        in_specs=in_specs,
        out_specs=pl.BlockSpec((MLP_TILE, d), row),
        out_shape=jax.ShapeDtypeStruct((t, d), F32),
        compiler_params=_cparams("parallel"),
        name="mlp",
    )(*args)


def _mla_proj_kernel(x_ref, g_ref, wdq_ref, qn_ref, wuqt_ref, wdkv_ref, kvn_ref, wukt_ref, wuvt_ref,
                     vone_ref, qc_ref, qsa_ref, qsb_ref, kc_ref, ksa_ref, ksb_ref,
                     qt_ref, k_ref, vt_ref):
    half = MLA_ROPE // 2
    hn = _rms_rows(x_ref[...], g_ref[...]).astype(BF16)
    cq = _rms_rows(_dot(hn, wdq_ref[...]), qn_ref[...]).astype(BF16)
    dkv = _dot(hn, wdkv_ref[...])
    ckv = _rms_rows(dkv[:, :MLA_KV_RANK], kvn_ref[...]).astype(BF16)

    vt_ref[0] = (_dot_nt(wuvt_ref[...], ckv) + vone_ref[...]).astype(BF16)

    n0, a0, b0, e0 = 0, MLA_NOPE, MLA_NOPE + half, MLA_QK

    def ssq(t):
        return jnp.sum(t * t, axis=0, keepdims=True)

    def rope(a, b, c, sa, sb):
        return a * c[a0:b0] + b * sa[a0:b0], b * c[b0:e0] + a * sb[b0:e0]

    def inv_rms(s):
        return lax.rsqrt(s * (1.0 / MLA_QK) + EPS)

    tm = x_ref.shape[0]
    pad_rows = jnp.zeros((LANES - MLA_QK, tm), F32)

    qt = _dot_nt(wuqt_ref[...], cq)
    qc, qsa, qsb = qc_ref[...], qsa_ref[...], qsb_ref[...]
    for h in range(MLA_HEADS):
        t = qt[h * MLA_QK:(h + 1) * MLA_QK]
        n, a, b = t[n0:a0], t[a0:b0], t[b0:e0]
        r = inv_rms(ssq(n) + ssq(a) + ssq(b))
        ra, rb = rope(a, b, qc, qsa, qsb)
        qt_ref[h * LANES:h * LANES + MLA_QK, :] = jnp.concatenate(
            [n * qc[n0:a0] * r, ra * r, rb * r], axis=0).astype(BF16)
        qt_ref[h * LANES + MLA_QK:(h + 1) * LANES, :] = pad_rows.astype(BF16)

    krt = dkv[:, MLA_KV_RANK:].T
    kt = _dot_nt(wukt_ref[...], ckv)
    kc, ksa, ksb = kc_ref[...], ksa_ref[...], ksb_ref[...]
    ka, kb = krt[a0:b0], krt[b0:e0]
    ssq_rope = ssq(ka) + ssq(kb)
    ra, rb = rope(ka, kb, kc, ksa, ksb)
    for h in range(MLA_HEADS):
        n = kt[h * MLA_NOPE:(h + 1) * MLA_NOPE]
        r = inv_rms(ssq(n) + ssq_rope)
        kh = jnp.concatenate([n * kc[n0:a0] * r, ra * r, rb * r, pad_rows], axis=0)
        k_ref[:, h * LANES:(h + 1) * LANES] = kh.T.astype(BF16)


def _attn_kernel(qt_ref, k_ref, vt_ref, o_ref, s_ref, acc_ref, m_ref, mb_ref):
    i = pl.program_id(2)
    tq, tk = ATT_Q_TILE, ATT_TILE
    nh = ATT_HEADS
    kpq = tq // tk
    assert kpq == 2
    vrows = vt_ref.shape[1] // nh
    krow = lax.broadcasted_iota(jnp.int32, (tk, tq), 0)
    qcol = lax.broadcasted_iota(jnp.int32, (tk, tq), 1)
    diag_mask = (krow // CHUNK) <= (qcol // CHUNK)
    late = slice(tk, tq)

    def produce(h, kb, qcols=slice(None), mask=None):
        start = pl.multiple_of(kb * tk, tk)
        st = _dot(k_ref[pl.ds(start, tk), h * LANES:(h + 1) * LANES], qt_ref[h * LANES:(h + 1) * LANES, qcols])
        if mask is not None:
            st = jnp.where(mask, st, NEG_BIG)
        s_ref[h, :, qcols] = st
        mb_ref[h, :, qcols] = jnp.max(st, axis=0, keepdims=True)

    def consume(h, kb, mask=None, qcols=slice(None)):
        st = s_ref[h, :, qcols]
        if mask is None:
            mblk = mb_ref[h, :, qcols]
        else:
            st = jnp.where(mask, st, NEG_BIG)
            mblk = jnp.max(st, axis=0, keepdims=True)
        m = m_ref[h, :, qcols]
        m_new = jnp.maximum(m, mblk)
        m_ref[h, :, qcols] = m_new
        pt = jnp.exp2(st - m_new).astype(BF16)
        acc_ref[h, :, qcols] = (jnp.exp2(m - m_new) * acc_ref[h, :, qcols]
                                + _dot(vt_ref[kb, h * vrows:(h + 1) * vrows, :], pt))

    def body(j, carry):
        for h in range(nh):
            produce((h + 1) % nh, j + (h + 1) // nh)
            consume(h, j)
        return carry

    m_ref[...] = jnp.full(m_ref.shape, NEG_BIG, F32)
    acc_ref[...] = jnp.zeros(acc_ref.shape, F32)
    first = kpq * i
    produce(0, 0)
    lax.fori_loop(0, i, lambda t, c: body(kpq * t + 1, body(kpq * t, c)), 0)
    late_mask = diag_mask[:, :tk]
    for h in range(nh):
        if h + 1 < nh:
            produce(h + 1, first, mask=diag_mask)
        else:
            produce(0, first + 1, late, late_mask)
        consume(h, first, diag_mask if h == 0 else None)
    for h in range(nh):
        if h + 1 < nh:
            produce(h + 1, first + 1, late, late_mask)
        consume(h, first + 1, None, late)
        acc = acc_ref[h]
        o_ref[h * MLA_V:(h + 1) * MLA_V, :] = (acc[:MLA_V] / acc[MLA_V:MLA_V + 1]).astype(BF16)


def _mla_layer(x, bsz, seq, g_mix, p, g_ffn, w1, w2):
    t, d = x.shape
    assert TOK_TILE == ATT_TILE
    hp = MLA_HEADS * LANES
    hv = MLA_HEADS * MLA_VROWS
    nt = seq // TOK_TILE
    row = lambda i: (i, 0)
    ttab = lambda i: (i % nt, 0)
    ftab = lambda i: (0, i % nt)
    qt, k, vt = pl.pallas_call(
        _mla_proj_kernel,
        grid=(t // TOK_TILE,),
        in_specs=[pl.BlockSpec((TOK_TILE, d), row), _const_spec((1, d)),
                  _const_spec((d, MLA_Q_RANK)), _const_spec((1, MLA_Q_RANK)),
                  _const_spec((MLA_HEADS * MLA_QK, MLA_Q_RANK)),
                  _const_spec((d, MLA_KV_RANK + LANES)), _const_spec((1, MLA_KV_RANK)),
                  _const_spec((MLA_HEADS * MLA_NOPE, MLA_KV_RANK)), _const_spec((hv, MLA_KV_RANK)),
                  _const_spec((hv, TOK_TILE))] + [pl.BlockSpec((LANES, TOK_TILE), ftab)] * 6,
        out_specs=[pl.BlockSpec((hp, TOK_TILE), lambda i: (0, i)), pl.BlockSpec((TOK_TILE, hp), row),
                   pl.BlockSpec((1, hv, TOK_TILE), lambda i: (i, 0, 0))],
        out_shape=[jax.ShapeDtypeStruct((hp, t), BF16), jax.ShapeDtypeStruct((t, hp), BF16),
                   jax.ShapeDtypeStruct((t // TOK_TILE, hv, TOK_TILE), BF16)],
        compiler_params=_cparams("parallel"),
        name="mla_proj",
    )(x, g_mix, p["w_dq"], p["q_norm"], p["w_uq_t"], p["w_dkv"], p["kv_norm"], p["w_uk_t"], p["w_uv_t"],
      p["vone"], p["qc"], p["qsa"], p["qsb"], p["kc"], p["ksa"], p["ksb"])

    nq = seq // ATT_Q_TILE
    ot = pl.pallas_call(
        _attn_kernel,
        grid=(bsz, MLA_HEADS // ATT_HEADS, nq),
        in_specs=[pl.BlockSpec((ATT_HEADS * LANES, ATT_Q_TILE), lambda b, h, i: (h, b * nq + i)),
                  pl.BlockSpec((seq, ATT_HEADS * LANES), lambda b, h, i: (b, h)),
                  pl.BlockSpec((seq // ATT_TILE, ATT_HEADS * MLA_VROWS, ATT_TILE), lambda b, h, i: (b, h, 0))],
        out_specs=pl.BlockSpec((ATT_HEADS * MLA_V, ATT_Q_TILE), lambda b, h, i: (h, b * nq + i)),
        out_shape=jax.ShapeDtypeStruct((MLA_HEADS * MLA_V, t), BF16),
        scratch_shapes=[pltpu.VMEM((ATT_HEADS, ATT_TILE, ATT_Q_TILE), F32),
                        pltpu.VMEM((ATT_HEADS, MLA_VROWS, ATT_Q_TILE), F32),
                        pltpu.VMEM((ATT_HEADS, 1, ATT_Q_TILE), F32), pltpu.VMEM((ATT_HEADS, 1, ATT_Q_TILE), F32)],
        compiler_params=_cparams("parallel", "parallel", "arbitrary"),
        name="mla_attn",
    )(qt, k, vt)
    return _mlp_call(x, ot, p["w_o"], None, g_ffn, w1, w2, a_transposed=True)


def _mla_params(seq, w_dq, q_norm, w_uq, w_dkv, kv_norm, w_ukv, q_gain, k_gain, w_o):
    pad_head = LANES - MLA_QK
    half = MLA_ROPE // 2
    w_ukv_r = w_ukv.reshape(MLA_KV_RANK, MLA_HEADS, MLA_NOPE + MLA_V)
    w_uv_p = jnp.pad(w_ukv_r[:, :, MLA_NOPE:], ((0, 0), (0, 0), (0, MLA_VROWS - MLA_V)))
    w_rope = jnp.pad(w_dkv[:, MLA_KV_RANK:], ((0, 0), (MLA_NOPE, pad_head)))
    w_dkv_p = jnp.concatenate([w_dkv[:, :MLA_KV_RANK], w_rope], axis=1)
    vone = np.zeros((MLA_HEADS * MLA_VROWS, TOK_TILE), np.float32)
    vone[np.arange(MLA_HEADS) * MLA_VROWS + MLA_V] = 1.0
    gq = q_gain * ((MLA_QK ** -0.5) * math.log2(math.e))
    inv_freq = ROPE_BASE ** (-jnp.arange(half, dtype=F32) / half)
    ang = jnp.arange(seq).astype(F32)[:, None] * inv_freq[None, :]
    cos, sin = jnp.cos(ang), jnp.sin(ang)
    g1 = lambda g: g[MLA_NOPE:MLA_NOPE + half][None, :]
    g2 = lambda g: g[MLA_NOPE + half:MLA_QK][None, :]
    z = lambda n: jnp.zeros((seq, n), F32)

    def tables(g):
        nope = jnp.broadcast_to(g[None, :MLA_NOPE], (seq, MLA_NOPE))
        c = jnp.concatenate([nope, g1(g) * cos, g2(g) * cos, z(pad_head)], axis=1)
        sa = jnp.concatenate([z(MLA_NOPE), -g2(g) * sin, z(half + pad_head)], axis=1)
        sb = jnp.concatenate([z(MLA_NOPE + half), g1(g) * sin, z(pad_head)], axis=1)
        return c.T, sa.T, sb.T

    qc, qsa, qsb = tables(gq)
    kc, ksa, ksb = tables(k_gain)
    return dict(w_dq=w_dq.astype(BF16), q_norm=q_norm.reshape(1, -1),
                w_uq_t=w_uq.T.astype(BF16), w_dkv=w_dkv_p.astype(BF16), kv_norm=kv_norm.reshape(1, -1),
                w_uk_t=w_ukv_r[:, :, :MLA_NOPE].reshape(MLA_KV_RANK, -1).T.astype(BF16),
                w_uv_t=w_uv_p.reshape(MLA_KV_RANK, -1).T.astype(BF16), vone=jnp.asarray(vone, BF16),
                qc=qc, qsa=qsa, qsb=qsb, kc=kc, ksa=ksa, ksb=ksb, w_o=w_o.astype(BF16))


def _mlstm_proj_kernel(x_ref, g_ref, wt_ref, wk_ref, wr_hi_ref, br_ref,
                       qt_ref, k_ref, vt_ref, ot_ref, gr_ref):
    hn = _rms_rows(x_ref[...], g_ref[...])
    hi, lo = _split2(hn)
    hq, hv = ML_HEADS * ML_QK, ML_HEADS * ML_V
    yt = _dot_nt(wt_ref[...], hi)
    qt_ref[...] = yt[:hq].astype(BF16)
    vt_ref[...] = yt[hq:hq + hv].astype(BF16)
    ot_ref[...] = yt[hq + hv:hq + 2 * hv].astype(BF16)
    k_ref[...] = (_dot(hi, wk_ref[...]) * (ML_QK ** -0.5)).astype(BF16)
    g0 = hq + 2 * hv
    gr_ref[...] = (yt[g0:g0 + SUBLANES] + yt[g0 + SUBLANES:] + _dot_nt(wr_hi_ref[...], lo)) + br_ref[...]


def _cap(g):
    return GATE_CAP * jnp.tanh(g * (1.0 / GATE_CAP))


def _mlstm_rec_kernel(qt_ref, k_ref, vt_ref, opt_ref, gr_ref, hnorm_ref, triu_ref, sel_ref,
                      o_ref, c_ref, m_ref):
    L = REC_TILE

    @pl.when(pl.program_id(1) == 0)
    def _():
        c_ref[...] = jnp.zeros_like(c_ref)
        m_ref[...] = jnp.zeros_like(m_ref)

    triu = triu_ref[...]
    sel = sel_ref[...]
    grow = lax.broadcasted_iota(jnp.int32, (SUBLANES, L), 0)
    src = lax.broadcasted_iota(jnp.int32, (L, L), 0)
    tgt = lax.broadcasted_iota(jnp.int32, (L, L), 1)
    causal = src <= tgt
    ones = jnp.ones((LANES, L), BF16)

    for sub in range(ML_SUBTILES):
        tok = slice(sub * L, (sub + 1) * L)
        gr = _cap(gr_ref[:, tok])
        r1, r2, r3 = _split3(_log_sigmoid(gr))
        bcum_r = _dot(r1, triu) + _dot(r2, triu) + _dot(r3, triu)
        xr = jnp.concatenate([jnp.where(grow < ML_HEADS, gr, bcum_r), jnp.zeros((LANES - SUBLANES, L), F32)],
                             axis=0)
        x1, x2, x3 = _split3(xr)
        cb_all = _dot_tn(x1, sel) + _dot_tn(x2, sel) + _dot_tn(x3, sel)

        for h in range(ML_HEADS):
            fh = ML_HEADS + h
            cb = cb_all[:, h * L:(h + 1) * L]
            bt = bcum_r[fh:fh + 1, :]
            li_r = gr[h:h + 1, :]
            m_prev = m_ref[h:h + 1, 0:1]
            qt = qt_ref[h * ML_QK:(h + 1) * ML_QK, tok]
            k = k_ref[tok, h * ML_QK:(h + 1) * ML_QK]
            vaug = jnp.concatenate([vt_ref[h * ML_V:(h + 1) * ML_V, tok], ones], axis=0)
            c_st = c_ref[h]

            dmat = jnp.where(causal, cb + bt, NEG_BIG)
            inter = bt + m_prev
            m_t = jnp.maximum(inter, jnp.max(dmat, axis=0, keepdims=True))
            w_intra = jnp.exp(dmat - m_t)
            w_inter = jnp.exp(inter - m_t)
            pt = (_dot(k, qt) * w_intra).astype(BF16)
            nd = _dot(vaug, pt) + w_inter * _dot(c_st.astype(BF16), qt)
            den = jnp.maximum(jnp.abs(nd[ML_V:ML_V + 1]), jnp.exp(-m_t))
            hc = nd[:ML_V] / den

            b_last = bt[:, L - 1:L]
            m_new = jnp.maximum(b_last + m_prev, jnp.max(b_last - bt + li_r, axis=-1, keepdims=True))
            ws = jnp.exp(cb[:, :ML_QK] + (b_last - m_new))
            wc = jnp.exp(b_last + m_prev - m_new)
            kw = (k.astype(F32) * ws).astype(BF16)
            c_ref[h] = wc * c_st + _dot(vaug, kw)
            m_ref[h:h + 1, :] = jnp.broadcast_to(m_new, (1, LANES))

            rows = slice(h * ML_V, (h + 1) * ML_V)
            hs = hc * lax.rsqrt(jnp.mean(hc * hc, axis=0, keepdims=True) + EPS) * hnorm_ref[rows, :]
            o_ref[rows, tok] = (_sigmoid(opt_ref[rows, tok].astype(F32)) * hs).astype(BF16)


def _mlstm_consts(n):
    r = np.arange(n)
    triu = (r[:, None] <= r[None, :]).astype(np.float32)
    sel = np.zeros((LANES, ML_HEADS * n), np.float32)
    for h in range(ML_HEADS):
        sel[h, h * n:(h + 1) * n] = 1.0
        sel[ML_HEADS + h, h * n:(h + 1) * n] = -1.0
    return jnp.asarray(triu, BF16), jnp.asarray(sel, BF16)


def _mlstm_layer(x, bsz, seq, g_mix, p, g_ffn, w1, w2):
    t, d = x.shape
    hq, hv = ML_HEADS * ML_QK, ML_HEADS * ML_V
    row = lambda i: (i, 0)
    col = lambda i: (0, i)
    qt, k, vt, opt, grow = pl.pallas_call(
        _mlstm_proj_kernel,
        grid=(t // PROJ_TILE,),
        in_specs=[pl.BlockSpec((PROJ_TILE, d), row), _const_spec((1, d)),
                  _const_spec((hq + 2 * hv + 2 * SUBLANES, d)),
                  _const_spec((d, hq)), _const_spec((SUBLANES, d)), _const_spec((SUBLANES, 1))],
        out_specs=[pl.BlockSpec((hq, PROJ_TILE), col), pl.BlockSpec((PROJ_TILE, hq), row),
                   pl.BlockSpec((hv, PROJ_TILE), col), pl.BlockSpec((hv, PROJ_TILE), col),
                   pl.BlockSpec((SUBLANES, PROJ_TILE), col)],
        out_shape=[jax.ShapeDtypeStruct((hq, t), BF16), jax.ShapeDtypeStruct((t, hq), BF16),
                   jax.ShapeDtypeStruct((hv, t), BF16), jax.ShapeDtypeStruct((hv, t), BF16),
                   jax.ShapeDtypeStruct((SUBLANES, t), F32)],
        compiler_params=_cparams("parallel"),
        name="mlstm_proj",
    )(x, g_mix, p["w_t"], p["w_k"], p["wr_hi"], p["b_row"])

    step = ML_SUBTILES * REC_TILE
    nc = seq // step
    triu, sel = _mlstm_consts(REC_TILE)
    hnorm = jnp.broadcast_to(p["head_norm"].reshape(hv, 1), (hv, REC_TILE))
    rblk = lambda b, c: (b * nc + c, 0)
    cblk = lambda b, c: (0, b * nc + c)
    at = pl.pallas_call(
        _mlstm_rec_kernel,
        grid=(bsz, nc),
        in_specs=[pl.BlockSpec((hq, step), cblk), pl.BlockSpec((step, hq), rblk),
                  pl.BlockSpec((hv, step), cblk), pl.BlockSpec((hv, step), cblk),
                  pl.BlockSpec((SUBLANES, step), cblk),
                  _const_spec((hv, REC_TILE)), _const_spec((REC_TILE, REC_TILE)),
                  _const_spec((LANES, ML_HEADS * REC_TILE))],
        out_specs=pl.BlockSpec((hv, step), cblk),
        out_shape=jax.ShapeDtypeStruct((hv, t), BF16),
        scratch_shapes=[pltpu.VMEM((ML_HEADS, ML_V + LANES, ML_QK), F32), pltpu.VMEM((SUBLANES, LANES), F32)],
        compiler_params=_cparams("parallel", "arbitrary"),
        name="mlstm_rec",
    )(qt, k, vt, opt, grow, hnorm, triu, sel)
    return _mlp_call(x, at, p["w_o"], None, g_ffn, w1, w2, a_transposed=True)


def _mlstm_params(w_in, w_if, b_if, head_norm, w_o):
    ng = 2 * ML_HEADS
    hq = ML_HEADS * ML_QK
    w_row = w_if.T
    wr_hi = w_row.astype(BF16)
    wr_lo = (w_row - wr_hi.astype(F32)).astype(BF16)
    w_t = jnp.concatenate([w_in[:, :hq], w_in[:, 2 * hq:]], axis=1).T.astype(BF16)
    return dict(w_t=jnp.concatenate([w_t, wr_hi, wr_lo], axis=0), w_k=w_in[:, hq:2 * hq].astype(BF16),
                wr_hi=wr_hi, b_row=b_if.reshape(ng, 1),
                head_norm=head_norm, w_o=w_o.astype(BF16))


def _gla_proj_kernel(x_ref, g_ref, win_ref, wa1_ref, wa2_ref, ba_ref, q_ref, k_ref, v_ref, r_ref, la_ref):
    hn = _rms_rows(x_ref[...], g_ref[...]).astype(BF16)
    y = _dot(hn, win_ref[...])
    hk, hv = GLA_HEADS * GLA_K, GLA_HEADS * GLA_V
    q_ref[...] = (y[:, :hk] * (GLA_K ** -0.5)).astype(BF16)
    k_ref[...] = y[:, hk:2 * hk].astype(BF16)
    v_ref[...] = y[:, 2 * hk:2 * hk + hv].astype(BF16)
    r_ref[...] = y[:, 2 * hk + hv:].astype(BF16)
    z = _dot(_dot(hn, wa1_ref[...]).astype(BF16), wa2_ref[...]) + ba_ref[...]
    la_ref[...] = _log_sigmoid(z) * (1.0 / GLA_TAU)


def _gla_levels():
    return [2 ** j for j in range(1, int(math.log2(REC_TILE)) + 1)]


def _gla_consts():
    n = REC_TILE
    t = np.arange(n)[:, None]
    u = np.arange(n)[None, :]
    mats = []
    for p in _gla_levels():
        ref_row = (t // p) * p + p // 2
        upper = (t % p) >= p // 2
        m_up = (u > ref_row) & (u <= t)
        m_lo = (u > t) & (u <= ref_row)
        mats.append(np.where(upper, m_up, m_lo))
    mats.append(u <= t)
    mats.append(u > t)
    return jnp.asarray(np.stack(mats).astype(np.float32), BF16)


def _gla_rec_kernel(q_ref, k_ref, v_ref, r_ref, la_ref, hnorm_ref, w_ref, o_ref, c_ref):
    L = REC_TILE
    levels = _gla_levels()
    nl = len(levels)

    @pl.when(pl.program_id(1) == 0)
    def _():
        c_ref[...] = jnp.zeros_like(c_ref)

    row = lax.broadcasted_iota(jnp.int32, (L, L), 0)
    col = lax.broadcasted_iota(jnp.int32, (L, L), 1)
    row_xor_col = row ^ col
    rowk = lax.broadcasted_iota(jnp.int32, (L, GLA_K), 0)

    for sub in range(GLA_SUBTILES):
        tok = slice(sub * L, (sub + 1) * L)
        la_hi, la_lo = _split2(la_ref[tok, :])

        def decay(j, two_term):
            w = w_ref[j]
            x = _dot(w, la_hi)
            if two_term:
                x = x + _dot(w, la_lo)
            return jnp.exp(x)

        e_cum = decay(nl, True)
        e_rev = decay(nl + 1, True)
        e_lvl = [decay(j, False) for j in range(nl)]

        for h in range(GLA_HEADS):
            ks = slice(h * GLA_K, (h + 1) * GLA_K)
            vs = slice(h * GLA_V, (h + 1) * GLA_V)
            q = q_ref[tok, ks].astype(F32)
            k = k_ref[tok, ks].astype(F32)
            v = v_ref[tok, vs]
            att = jnp.where(row == col, _dot_nt(q_ref[tok, ks], k_ref[tok, ks]), 0.0)
            for j, p in enumerate(levels):
                upper = (rowk & (p - 1)) >= (p // 2)
                e = e_lvl[j][:, ks]
                qf = jnp.where(upper, q * e, 0.0).astype(BF16)
                kf = jnp.where(upper, 0.0, k * e).astype(BF16)
                a = _dot_nt(qf, kf)
                if p < L:
                    a = jnp.where(row_xor_col < p, a, 0.0)
                att = att + a
            c_st = c_ref[h]
            o = _dot(att.astype(BF16), v) + _dot_nt((q * e_cum[:, ks]).astype(BF16), c_st.astype(BF16))
            c_ref[h] = e_cum[L - 1:L, ks] * c_st + _dot_tn(v, (k * e_rev[:, ks]).astype(BF16))
            o = _rms_rows(o, hnorm_ref[:, vs])
            rr = r_ref[tok, vs].astype(F32)
            o_ref[tok, vs] = (o * rr * _sigmoid(rr)).astype(BF16)


def _gla_layer(x, bsz, seq, g_mix, p, g_ffn, w1, w2):
    t, d = x.shape
    hk, hv = GLA_HEADS * GLA_K, GLA_HEADS * GLA_V
    row = lambda i: (i, 0)
    q, k, v, r, la = pl.pallas_call(
        _gla_proj_kernel,
        grid=(t // PROJ_TILE,),
        in_specs=[pl.BlockSpec((PROJ_TILE, d), row), _const_spec((1, d)), _const_spec((d, 2 * hk + 2 * hv)),
                  _const_spec((d, LANES)), _const_spec((LANES, hk)), _const_spec((1, hk))],
        out_specs=[pl.BlockSpec((PROJ_TILE, hk), row), pl.BlockSpec((PROJ_TILE, hk), row),
                   pl.BlockSpec((PROJ_TILE, hv), row), pl.BlockSpec((PROJ_TILE, hv), row),
                   pl.BlockSpec((PROJ_TILE, hk), row)],
        out_shape=[jax.ShapeDtypeStruct((t, hk), BF16), jax.ShapeDtypeStruct((t, hk), BF16),
                   jax.ShapeDtypeStruct((t, hv), BF16), jax.ShapeDtypeStruct((t, hv), BF16),
                   jax.ShapeDtypeStruct((t, hk), F32)],
        compiler_params=_cparams("parallel"),
        name="gla_proj",
    )(x, g_mix, p["w_in"], p["w_a1"], p["w_a2"], p["b_a"])

    step = GLA_SUBTILES * REC_TILE
    nc = seq // step
    wmats = _gla_consts()
    blk = lambda b, c: (b * nc + c, 0)
    a = pl.pallas_call(
        _gla_rec_kernel,
        grid=(bsz, nc),
        in_specs=[pl.BlockSpec((step, hk), blk), pl.BlockSpec((step, hk), blk),
                  pl.BlockSpec((step, hv), blk), pl.BlockSpec((step, hv), blk),
                  pl.BlockSpec((step, hk), blk), _const_spec((1, hv)), _const_spec(wmats.shape)],
        out_specs=pl.BlockSpec((step, hv), blk),
        out_shape=jax.ShapeDtypeStruct((t, hv), BF16),
        scratch_shapes=[pltpu.VMEM((GLA_HEADS, GLA_V, GLA_K), F32)],
        compiler_params=_cparams("parallel", "arbitrary"),
        name="gla_rec",
    )(q, k, v, r, la, p["head_norm"], wmats)
    return _mlp_call(x, a, p["w_o"], None, g_ffn, w1, w2)


def _gla_params(w_in, w_a1, w_a2, b_a, head_norm, w_o):
    return dict(w_in=w_in.astype(BF16),
                w_a1=jnp.pad(w_a1, ((0, 0), (0, LANES - GLA_GATE_RANK))).astype(BF16),
                w_a2=jnp.pad(w_a2, ((0, LANES - GLA_GATE_RANK), (0, 0))).astype(BF16),
                b_a=b_a.reshape(1, -1), head_norm=head_norm.reshape(1, -1), w_o=w_o.astype(BF16))


def _conv_kernel(x_ref, g_ref, w1_ref, b1_ref, wdw_ref, lng_ref, lnb_ref, o_ref, u_ref, c_ref):
    tm = TOK_TILE
    d = D_MODEL
    nbuf = CONV_HALO + tm

    @pl.when(pl.program_id(1) == 0)
    def _():
        u_ref[0, :, 0:CONV_HALO, :] = jnp.zeros((d // LANES, CONV_HALO, LANES), F32)

    @pl.when(pl.program_id(1) != 0)
    def _():
        u_ref[0, :, 0:CONV_HALO, :] = u_ref[0, :, tm:tm + CONV_HALO, :]

    hn = _rms_rows(x_ref[...], g_ref[...]).astype(BF16)
    y = _dot(hn, w1_ref[...]) + b1_ref[...]
    u = y[:, :d] * _sigmoid(y[:, d:])
    lead = CONV_HALO - (CONV_WIDTH - 1)
    groups = CONV_ROWS // SUBLANES
    tiles = nbuf // SUBLANES
    sub = lax.broadcasted_iota(jnp.int32, (tiles, SUBLANES, LANES), 1)

    for cb in range(d // LANES):
        cols = slice(cb * LANES, (cb + 1) * LANES)
        u_ref[0, cb, CONV_HALO:, :] = u[:, cols]
        full = u_ref[0, cb].reshape(tiles, SUBLANES, LANES)
        for s in range(1, SUBLANES):
            rot = pltpu.roll(full, SUBLANES - s, 1)
            nxt = jnp.concatenate([rot[1:], rot[:1]], axis=0)
            u_ref[s, cb] = jnp.where(sub < SUBLANES - s, rot, nxt).reshape(nbuf, LANES)

        def taps(c, carry):
            r0 = pl.multiple_of(c * CONV_ROWS, CONV_ROWS)
            acc = jnp.broadcast_to(wdw_ref[CONV_WIDTH, :, cols][None], (groups, SUBLANES, LANES))
            for j in range(CONV_WIDTH):
                off = lead + j
                start = pl.multiple_of(r0 + (off // SUBLANES) * SUBLANES, SUBLANES)
                uj = u_ref[off % SUBLANES, cb, pl.ds(start, CONV_ROWS), :].reshape(groups, SUBLANES, LANES)
                acc = acc + uj * wdw_ref[j, :, cols][None]
            c_ref[pl.ds(r0, CONV_ROWS), cols] = acc.reshape(CONV_ROWS, LANES)
            return carry

        lax.fori_loop(0, tm // CONV_ROWS, taps, 0)

    def norm(c, carry):
        r0 = pl.multiple_of(c * CONV_NORM_ROWS, CONV_NORM_ROWS)
        acc = c_ref[pl.ds(r0, CONV_NORM_ROWS), :]
        mu = jnp.mean(acc, axis=-1, keepdims=True)
        cen = acc - mu
        var = jnp.mean(cen * cen, axis=-1, keepdims=True)
        z = cen * lax.rsqrt(var + EPS) * lng_ref[...] + lnb_ref[...]
        o_ref[pl.ds(r0, CONV_NORM_ROWS), :] = (z * _sigmoid(z)).astype(BF16)
        return carry

    lax.fori_loop(0, tm // CONV_NORM_ROWS, norm, 0, unroll=8)


def _conv_layer(x, bsz, seq, g_mix, p, g_ffn, w1, w2):
    t, d = x.shape
    nt = seq // TOK_TILE
    blk = lambda b, i: (b * nt + i, 0)
    c = pl.pallas_call(
        _conv_kernel,
        grid=(bsz, nt),
        in_specs=[pl.BlockSpec((TOK_TILE, d), blk), _const_spec((1, d)), _const_spec((d, 2 * d)),
                  _const_spec((1, 2 * d)), _const_spec((CONV_WIDTH + 1, SUBLANES, d)),
                  _const_spec((1, d)), _const_spec((1, d))],
        out_specs=pl.BlockSpec((TOK_TILE, d), blk),
        out_shape=jax.ShapeDtypeStruct((t, d), BF16),
        scratch_shapes=[pltpu.VMEM((SUBLANES, d // LANES, CONV_HALO + TOK_TILE, LANES), F32),
                        pltpu.VMEM((TOK_TILE, d), F32)],
        compiler_params=_cparams("parallel", "arbitrary"),
        name="conv",
    )(x, g_mix, p["w_pw1"], p["b_pw1"], p["w_dw"], p["ln_g"], p["ln_b"])
    return _mlp_call(x, c, p["w_pw2"], p["b_pw2"], g_ffn, w1, w2)


def _conv_params(w_pw1, b_pw1, w_dw, b_dw, ln_g, ln_b, w_pw2, b_pw2):
    return dict(w_pw1=w_pw1.astype(BF16), b_pw1=b_pw1.reshape(1, -1),
                w_dw=jnp.broadcast_to(jnp.concatenate([w_dw, b_dw[None]], axis=0)[:, None, :],
                                      (CONV_WIDTH + 1, SUBLANES, w_dw.shape[1])),
                ln_g=ln_g.reshape(1, -1), ln_b=ln_b.reshape(1, -1),
                w_pw2=w_pw2.astype(BF16), b_pw2=b_pw2.reshape(1, -1))


def kernel(x, norm_mix, norm_ffn, mla_w_dq, mla_q_norm, mla_w_uq, mla_w_dkv, mla_kv_norm, mla_w_ukv, mla_q_gain, mla_k_gain, mla_w_o, mlstm_w_in, mlstm_w_if, mlstm_b_if, mlstm_head_norm, mlstm_w_o, gla_w_in, gla_w_a1, gla_w_a2, gla_b_a, gla_head_norm, gla_w_o, conv_w_pw1, conv_b_pw1, conv_w_dw, conv_b_dw, conv_ln_g, conv_ln_b, conv_w_pw2, conv_b_pw2, ffn_w1, ffn_w2):
    bsz, seq, d = x.shape
    depth = norm_mix.shape[0]
    assert d == D_MODEL and all(seq % tile == 0 for tile in (ATT_Q_TILE, TOK_TILE, MLP_TILE, PROJ_TILE, REC_TILE))
    h = x.reshape(bsz * seq, d)
    ffn_w1, ffn_w2 = ffn_w1.astype(BF16), ffn_w2.astype(BF16)
    for i in range(depth):
        kind, j = i % 4, i // 4
        g_mix = norm_mix[i].reshape(1, d)
        g_ffn = norm_ffn[i].reshape(1, d)
        w1, w2 = (ffn_w1, i), (ffn_w2, i)
        if kind == 0:
            p = _mla_params(seq, mla_w_dq[j], mla_q_norm[j], mla_w_uq[j], mla_w_dkv[j], mla_kv_norm[j],
                            mla_w_ukv[j], mla_q_gain[j], mla_k_gain[j], mla_w_o[j])
            h = _mla_layer(h, bsz, seq, g_mix, p, g_ffn, w1, w2)
        elif kind == 1:
            p = _mlstm_params(mlstm_w_in[j], mlstm_w_if[j], mlstm_b_if[j], mlstm_head_norm[j], mlstm_w_o[j])
            h = _mlstm_layer(h, bsz, seq, g_mix, p, g_ffn, w1, w2)
        elif kind == 2:
            p = _gla_params(gla_w_in[j], gla_w_a1[j], gla_w_a2[j], gla_b_a[j], gla_head_norm[j], gla_w_o[j])
            h = _gla_layer(h, bsz, seq, g_mix, p, g_ffn, w1, w2)
        else:
            p = _conv_params(conv_w_pw1[j], conv_b_pw1[j], conv_w_dw[j], conv_b_dw[j], conv_ln_g[j],
                             conv_ln_b[j], conv_w_pw2[j], conv_b_pw2[j])
            h = _conv_layer(h, bsz, seq, g_mix, p, g_ffn, w1, w2)
    return h.reshape(bsz, seq, d)
```
